```python
import jax
import jax.numpy as jnp
from jax import lax
import numpy as np

D_MODEL = 2048
BATCH = 16
SEQ = 256
DEPTH = 2
DEC_BATCH = 8
DEC_SEQ = 1024
PAST_LEN = 256

GRID_W = 64
N_EVEN = (DEPTH + 1) // 2
N_ODD = DEPTH // 2
A_HEADS = 8
A_KV_HEADS = 2
A_HEAD_DIM = 128
Q_BLOCK = 128
ROPE_BASE = 10000.0
B_HEADS = 8
B_DK = 128
B_DV = 128
B_CHUNK = 16
C_HEADS = 8
C_DK = 256
C_DV = 512
C_CHUNK = 128
D_FF = 5632
EPS = 1e-6

A_Q = A_HEADS * A_HEAD_DIM
A_KV = A_KV_HEADS * A_HEAD_DIM
B_QK = B_HEADS * B_DK
B_V = B_HEADS * B_DV
C_QK = C_HEADS * C_DK
C_V = C_HEADS * C_DV
EVEN_SIZES = (A_Q, A_KV, A_KV, B_QK, B_V, B_QK, B_QK, B_V)
ODD_SIZES = (C_QK, C_QK, C_V, C_V)
EVEN_IN = A_Q + 2 * A_KV + 3 * B_QK + 2 * B_V
ODD_IN = 2 * C_QK + 2 * C_V
EVEN_OUT = A_Q + B_V

kernel_name = 'hybrid_diffusion_attn_hgrn2_retention_step'


def _splits(sizes):
    return [int(s) for s in np.cumsum(sizes)[:-1]]


def rms_norm(x, w):
    xf = x.astype(jnp.float32)
    y = xf * lax.rsqrt(jnp.mean(xf * xf, axis=-1, keepdims=True) + EPS)
    return (y * w.astype(jnp.float32)).astype(x.dtype)


def axial_rope_tables(n_tokens, head_dim):
    rows = n_tokens // GRID_W
    row = jnp.repeat(jnp.arange(rows, dtype=jnp.float32), GRID_W)
    col = jnp.tile(jnp.arange(GRID_W, dtype=jnp.float32), rows)
    quarter = head_dim // 4
    inv_freq = jnp.power(ROPE_BASE, -jnp.arange(quarter, dtype=jnp.float32) / quarter)
    ang = jnp.concatenate([row[:, None] * inv_freq, col[:, None] * inv_freq], axis=-1)
    return jnp.cos(ang), jnp.sin(ang)


def apply_rope(x, rope):
    cos, sin = rope
    xf = x.astype(jnp.float32)
    half = xf.shape[-1] // 2
    x1, x2 = xf[..., :half], xf[..., half:]
    cos = cos[None, :, None, :]
    sin = sin[None, :, None, :]
    return jnp.concatenate([x1 * cos - x2 * sin, x1 * sin + x2 * cos], axis=-1).astype(x.dtype)


def to_heads(x, n_heads):
    bsz, n, _ = x.shape
    return x.reshape(bsz, n, n_heads, -1).transpose(0, 2, 1, 3).astype(jnp.float32)


def from_heads(o, norm_w, dtype):
    bsz, h, n, d = o.shape
    o = rms_norm(o.transpose(0, 2, 1, 3), norm_w)
    return o.reshape(bsz, n, h * d).astype(dtype)


def modulate(cond, w_mod, b_mod):
    m = jax.nn.silu(cond) @ w_mod + b_mod
    return [t[:, None, :] for t in jnp.split(m, 6, axis=-1)]


def block_attention(q, k, v):
    bsz, n, hq, hd = q.shape
    kvh = k.shape[2]
    g = hq // kvh
    nb = n // Q_BLOCK
    qb = q.reshape(bsz, nb, Q_BLOCK, kvh, g, hd).transpose(1, 0, 2, 3, 4, 5)
    scale = hd ** -0.5

    def one_block(qblk):
        s = jnp.einsum('bqkgd,bskd->bkgqs', qblk, k).astype(jnp.float32) * scale
        p = jax.nn.softmax(s, axis=-1).astype(v.dtype)
        return jnp.einsum('bkgqs,bskd->bqkgd', p, v)

    o = lax.map(one_block, qb)
    return o.transpose(1, 0, 2, 3, 4, 5).reshape(bsz, n, hq * hd)


def gla_chunkwise(q, k, v, log_f, s0):
    bsz, h, n, dk = q.shape
    dv = v.shape[-1]
    c = B_CHUNK
    nc = n // c
    q = q.reshape(bsz, h, nc, c, dk)
    k = k.reshape(bsz, h, nc, c, dk)
    v = v.reshape(bsz, h, nc, c, dv)
    b = jnp.cumsum(log_f.reshape(bsz, h, nc, c, dk), axis=3)
    b_last = b[:, :, :, -1, :]
    lower = jnp.tril(jnp.ones((c, c), dtype=bool))
    rel = b[:, :, :, :, None, :] - b[:, :, :, None, :, :]
    decay = jnp.exp(jnp.where(lower[:, :, None], rel, -jnp.inf))
    scores = jnp.einsum('bhntd,bhnsd,bhntsd->bhnts', q, k, decay)
    o_intra = jnp.einsum('bhnts,bhnse->bhnte', scores, v)
    u = jnp.einsum('bhnsd,bhnse->bhnde', k * jnp.exp(b_last[:, :, :, None, :] - b), v)
    chunk_decay = jnp.exp(b_last)

    def step(s, inp):
        d, uu = inp
        return d[..., None] * s + uu, s

    s_final, s_before = lax.scan(step, s0, (jnp.moveaxis(chunk_decay, 2, 0), jnp.moveaxis(u, 2, 0)))
    o_inter = jnp.einsum('bhntd,nbhde->bhnte', q * jnp.exp(b), s_before)
    return (o_intra + o_inter).reshape(bsz, h, n, dv), s_final


def retention_chunkwise(q, k, v, log_g, s0):
    bsz, h, n, dk = q.shape
    dv = v.shape[-1]
    c = C_CHUNK
    nc = n // c
    q = q.reshape(bsz, h, nc, c, dk)
    k = k.reshape(bsz, h, nc, c, dk)
    v = v.reshape(bsz, h, nc, c, dv)
    pos = jnp.arange(c, dtype=jnp.float32)
    diff = pos[:, None] - pos[None, :]
    dmat = jnp.where(diff >= 0, jnp.exp(log_g[:, None, None] * jnp.maximum(diff, 0.0)), 0.0)
    scores = jnp.einsum('bhntd,bhnsd->bhnts', q, k) * dmat[None, :, None]
    o_intra = jnp.einsum('bhnts,bhnse->bhnte', scores, v)
    q_dec = jnp.exp(log_g[:, None] * (pos + 1.0))
    k_dec = jnp.exp(log_g[:, None] * (c - 1.0 - pos))
    u = jnp.einsum('bhnsd,hs,bhnse->bhnde', k, k_dec, v)
    chunk_decay = jnp.exp(log_g * c)[None, :, None, None]

    def step(s, uu):
        return chunk_decay * s + uu, s

    s_final, s_before = lax.scan(step, s0, jnp.moveaxis(u, 2, 0))
    o_inter = jnp.einsum('bhntd,ht,nbhde->bhnte', q, q_dec, s_before)
    return (o_intra + o_inter).reshape(bsz, h, n, dv), s_final


def _flip(t):
    return jnp.flip(t, axis=2)


def even_mixer(h, w_in, w_out, q_norm_w, k_norm_w, lb, o_norm_w, rope, ctx_k, ctx_v, s0_f, s0_b):
    bsz, n, _ = h.shape
    q_a, k_a, v_a, q_b, i_b, z_f, z_b, g_b = jnp.split(h @ w_in, _splits(EVEN_SIZES), axis=-1)
    q_a = rms_norm(q_a.reshape(bsz, n, A_HEADS, A_HEAD_DIM), q_norm_w)
    k_a = rms_norm(k_a.reshape(bsz, n, A_KV_HEADS, A_HEAD_DIM), k_norm_w)
    v_a = v_a.reshape(bsz, n, A_KV_HEADS, A_HEAD_DIM)
    if rope is None:
        attn_out = block_attention(q_a, k_a, v_a)
    else:
        k_all = jnp.concatenate([ctx_k.astype(h.dtype), apply_rope(k_a, rope)], axis=1)
        v_all = jnp.concatenate([ctx_v.astype(h.dtype), v_a], axis=1)
        attn_out = block_attention(apply_rope(q_a, rope), k_all, v_all)
    q = to_heads(q_b, B_HEADS) * B_DK ** -0.5
    i = to_heads(i_b, B_HEADS)
    lbb = lb[None, :, None, :]
    f_f = lbb + (1.0 - lbb) * jax.nn.sigmoid(to_heads(z_f, B_HEADS))
    f_b = lbb + (1.0 - lbb) * jax.nn.sigmoid(to_heads(z_b, B_HEADS))
    o_f, s_f = gla_chunkwise(q, 1.0 - f_f, i, jnp.log(f_f), s0_f)
    o_b, s_b = gla_chunkwise(_flip(q), _flip(1.0 - f_b), _flip(i), _flip(jnp.log(f_b)), s0_b)
    hgrn_out = from_heads(o_f + _flip(o_b), o_norm_w, h.dtype) * jax.nn.silu(g_b)
    out = jnp.concatenate([attn_out, hgrn_out], axis=-1) @ w_out
    return out, k_a, v_a, s_f.astype(h.dtype), s_b.astype(h.dtype)


def odd_mixer(h, w_in, w_out, decay_f, decay_b, o_norm_w, rope, s0_f, s0_b):
    bsz, n, _ = h.shape
    q, k, v, g = jnp.split(h @ w_in, _splits(ODD_SIZES), axis=-1)
    q = q.reshape(bsz, n, C_HEADS, C_DK)
    k = k.reshape(bsz, n, C_HEADS, C_DK)
    if rope is not None:
        q = apply_rope(q, rope)
        k = apply_rope(k, rope)
    q = q.transpose(0, 2, 1, 3).astype(jnp.float32)
    k = k.transpose(0, 2, 1, 3).astype(jnp.float32) * C_DK ** -0.5
    v = to_heads(v, C_HEADS)
    log_gf = jax.nn.log_sigmoid(decay_f.astype(jnp.float32))
    log_gb = jax.nn.log_sigmoid(decay_b.astype(jnp.float32))
    o_f, s_f = retention_chunkwise(q, k, v, log_gf, s0_f)
    o_b, s_b = retention_chunkwise(_flip(q), _flip(k), _flip(v), log_gb, s0_b)
    o = from_heads(o_f + _flip(o_b), o_norm_w, h.dtype) * jax.nn.silu(g)
    return o @ w_out, s_f.astype(h.dtype), s_b.astype(h.dtype)


def dwconv3(u, w, b):
    up = jnp.pad(u, ((0, 0), (1, 1), (0, 0)))
    return up[:, :-2] * w[0] + up[:, 1:-1] * w[1] + up[:, 2:] * w[2] + b


def conv_ffn(h, w_up, conv_w, conv_b, w_down):
    u = dwconv3(h @ w_up, conv_w, conv_b)
    a, val = jnp.split(u, 2, axis=-1)
    return (jax.nn.silu(a) * val) @ w_down


def setup_inputs(seed: int = 0) -> dict:
    key = jax.random.key(seed)
    ks = jax.random.split(key, 29)
    f32 = jnp.float32

    def nrm(k, shape, scale=1.0):
        return jax.random.normal(k, shape, f32) * scale

    decay_logits = jnp.log(jnp.power(2.0, 5.0 + jnp.arange(C_HEADS, dtype=f32)) - 1.0)
    return {
        'x_prompt': nrm(ks[0], (BATCH, SEQ, D_MODEL)),
        'x_sample': nrm(ks[1], (DEC_BATCH, DEC_SEQ, D_MODEL)),
        'cache_attn_k': nrm(ks[2], (DEC_BATCH, N_EVEN, PAST_LEN, A_KV_HEADS, A_HEAD_DIM)),
        'cache_attn_v': nrm(ks[3], (DEC_BATCH, N_EVEN, PAST_LEN, A_KV_HEADS, A_HEAD_DIM)),
        'state_hgrn_fwd': nrm(ks[4], (DEC_BATCH, N_EVEN, B_HEADS, B_DK, B_DV), 0.5),
        'state_hgrn_bwd': nrm(ks[5], (DEC_BATCH, N_EVEN, B_HEADS, B_DK, B_DV), 0.5),
        'state_ret_fwd': nrm(ks[6], (DEC_BATCH, N_ODD, C_HEADS, C_DK, C_DV), 0.5),
        'state_ret_bwd': nrm(ks[7], (DEC_BATCH, N_ODD, C_HEADS, C_DK, C_DV), 0.5),
        'c': nrm(ks[8], (DEC_BATCH, D_MODEL)),
        'c_ctx': nrm(ks[9], (D_MODEL,)),
        'w_mod': nrm(ks[10], (DEPTH, D_MODEL, 6 * D_MODEL), D_MODEL ** -0.5),
        'b_mod': nrm(ks[11], (DEPTH, 6 * D_MODEL), 0.01),
        'norm_mix_w': 1.0 + nrm(ks[12], (DEPTH, D_MODEL), 0.05),
        'norm_ffn_w': 1.0 + nrm(ks[13], (DEPTH, D_MODEL), 0.05),
        'w_in_even': nrm(ks[14], (N_EVEN, D_MODEL, EVEN_IN), D_MODEL ** -0.5),
        'w_out_even': nrm(ks[15], (N_EVEN, EVEN_OUT, D_MODEL), EVEN_OUT ** -0.5),
        'attn_q_norm_w': 1.0 + nrm(ks[16], (N_EVEN, A_HEAD_DIM), 0.05),
        'attn_k_norm_w': 1.0 + nrm(ks[17], (N_EVEN, A_HEAD_DIM), 0.05),
        'hgrn_lb': nrm(ks[18], (N_EVEN + 1, B_HEADS, B_DK), 0.5),
        'hgrn_o_norm_w': 1.0 + nrm(ks[19], (N_EVEN, B_DV), 0.05),
        'w_in_odd': nrm(ks[20], (N_ODD, D_MODEL, ODD_IN), D_MODEL ** -0.5),
        'w_out_odd': nrm(ks[21], (N_ODD, C_V, D_MODEL), C_V ** -0.5),
        'ret_decay_fwd': decay_logits[None, :] + nrm(ks[22], (N_ODD, C_HEADS), 0.1),
        'ret_decay_bwd': decay_logits[None, :] + nrm(ks[23], (N_ODD, C_HEADS), 0.1),
        'ret_o_norm_w': 1.0 + nrm(ks[24], (N_ODD, C_DV), 0.05),
        'w_up': nrm(ks[25], (DEPTH, D_MODEL, 2 * D_FF), D_MODEL ** -0.5),
        'conv_w': nrm(ks[26], (DEPTH, 3, 2 * D_FF), 3 ** -0.5),
        'conv_b': nrm(ks[27], (DEPTH, 2 * D_FF), 0.01),
        'w_down': nrm(ks[28], (DEPTH, D_FF, D_MODEL), D_FF ** -0.5),
    }


def reference(x_prompt, x_sample, cache_attn_k, cache_attn_v, state_hgrn_fwd, state_hgrn_bwd,
              state_ret_fwd, state_ret_bwd, c, c_ctx, w_mod, b_mod, norm_mix_w, norm_ffn_w,
              w_in_even, w_out_even, attn_q_norm_w, attn_k_norm_w, hgrn_lb, hgrn_o_norm_w,
              w_in_odd, w_out_odd, ret_decay_fwd, ret_decay_bwd, ret_o_norm_w,
              w_up, conv_w, conv_b, w_down):
    f32 = jnp.float32
    lb_all = jnp.cumsum(jax.nn.softmax(hgrn_lb.astype(f32), axis=0), axis=0)

    y = x_prompt
    bsz_p = x_prompt.shape[0]
    new_k, new_v, new_hf, new_hb, new_rf, new_rb = [], [], [], [], [], []
    for l in range(DEPTH):
        sh1, sc1, g1, sh2, sc2, g2 = modulate(c_ctx[None, :], w_mod[l], b_mod[l])
        h = rms_norm(y, norm_mix_w[l]) * (1.0 + sc1) + sh1
        if l % 2 == 0:
            e = l // 2
            zeros = jnp.zeros((bsz_p, B_HEADS, B_DK, B_DV), f32)
            out, k_c, v_c, s_f, s_b = even_mixer(h, w_in_even[e], w_out_even[e], attn_q_norm_w[e],
                                                 attn_k_norm_w[e], lb_all[e], hgrn_o_norm_w[e],
                                                 None, None, None, zeros, zeros)
            new_k.append(k_c)
            new_v.append(v_c)
            new_hf.append(s_f)
            new_hb.append(s_b)
        else:
            o = l // 2
            zeros = jnp.zeros((bsz_p, C_HEADS, C_DK, C_DV), f32)
            out, s_f, s_b = odd_mixer(h, w_in_odd[o], w_out_odd[o], ret_decay_fwd[o], ret_decay_bwd[o],
                                      ret_o_norm_w[o], None, zeros, zeros)
            new_rf.append(s_f)
            new_rb.append(s_b)
        y = y + g1 * out
        h = rms_norm(y, norm_ffn_w[l]) * (1.0 + sc2) + sh2
        y = y + g2 * conv_ffn(h, w_up[l], conv_w[l], conv_b[l], w_down[l])

    z = x_sample
    n_lat = x_sample.shape[1]
    rope_a = axial_rope_tables(n_lat, A_HEAD_DIM)
    rope_c = axial_rope_tables(n_lat, C_DK)
    for l in range(DEPTH):
        sh1, sc1, g1, sh2, sc2, g2 = modulate(c, w_mod[l], b_mod[l])
        h = rms_norm(z, norm_mix_w[l]) * (1.0 + sc1) + sh1
        if l % 2 == 0:
            e = l // 2
            out, _, _, _, _ = even_mixer(h, w_in_even[e], w_out_even[e], attn_q_norm_w[e],
                                         attn_k_norm_w[e], lb_all[e], hgrn_o_norm_w[e], rope_a,
                                         cache_attn_k[:, e], cache_attn_v[:, e],
                                         state_hgrn_fwd[:, e].astype(f32), state_hgrn_bwd[:, e].astype(f32))
        else:
            o = l // 2
            out, _, _ = odd_mixer(h, w_in_odd[o], w_out_odd[o], ret_decay_fwd[o], ret_decay_bwd[o],
                                  ret_o_norm_w[o], rope_c,
                                  state_ret_fwd[:, o].astype(f32), state_ret_bwd[:, o].astype(f32))
        z = z + g1 * out
        h = rms_norm(z, norm_ffn_w[l]) * (1.0 + sc2) + sh2
        z = z + g2 * conv_ffn(h, w_up[l], conv_w[l], conv_b[l], w_down[l])

    return (y, z, jnp.stack(new_k, axis=1), jnp.stack(new_v, axis=1), jnp.stack(new_hf, axis=1),
            jnp.stack(new_hb, axis=1), jnp.stack(new_rf, axis=1), jnp.stack(new_rb, axis=1))
```

```python
import functools

import jax
import jax.numpy as jnp
import numpy as np
from jax import lax
from jax.experimental import pallas as pl
from jax.experimental.pallas import tpu as pltpu

F32 = jnp.float32
BF16 = jnp.bfloat16

GRID_W = 64
A_HEADS = 8
A_KV_HEADS = 2
A_HEAD_DIM = 128
ROPE_BASE = 10000.0
B_HEADS = 8
B_DK = 128
B_DV = 128
C_HEADS = 8
C_DK = 256
C_DV = 512
EPS = 1e-6

A_Q = A_HEADS * A_HEAD_DIM
A_KV = A_KV_HEADS * A_HEAD_DIM
A_GROUP = A_HEADS // A_KV_HEADS
B_QK = B_HEADS * B_DK
B_V = B_HEADS * B_DV
C_QK = C_HEADS * C_DK
C_V = C_HEADS * C_DV

VMEM_LIMIT_BYTES = 56 * 1024 * 1024
ROW_TILE = 1024
COL_TILE = 512
MIX_BLOCK = 128
HGRN_BASE = 16
EXP_CLAMP = 80.0


def _cparams(*sem):
    return pltpu.CompilerParams(dimension_semantics=sem, vmem_limit_bytes=VMEM_LIMIT_BYTES)


def _dot(a, b):
    return jnp.dot(a, b, preferred_element_type=F32)


def _dot_nt(a, b):
    return lax.dot_general(a, b, (((1,), (1,)), ((), ())), preferred_element_type=F32)


def _dot_tn(a, b):
    return lax.dot_general(a, b, (((0,), (0,)), ((), ())), preferred_element_type=F32)


def _sigmoid(x):
    return 1.0 / (1.0 + jnp.exp(-x))


def _rms(x, w):
    return x * lax.rsqrt(jnp.mean(x * x, axis=-1, keepdims=True) + EPS) * w


def _mod_kernel(c_ref, w_ref, b_ref, o_ref):
    c = c_ref[...]
    s = (c * _sigmoid(c)).astype(BF16)
    o_ref[...] = _dot(s, w_ref[...].astype(BF16)) + b_ref[...]


def _modulation(cond, w_mod, b_mod):
    n_layers, d, n_out = w_mod.shape
    r = cond.shape[0]
    tn = 1024
    return pl.pallas_call(
        _mod_kernel,
        grid=(n_layers, n_out // tn),
        in_specs=[
            pl.BlockSpec((r, d), lambda l, j: (0, 0)),
            pl.BlockSpec((None, d, tn), lambda l, j: (l, 0, j)),
            pl.BlockSpec((None, 1, tn), lambda l, j: (l, 0, j)),
        ],
        out_specs=pl.BlockSpec((None, r, tn), lambda l, j: (l, 0, j)),
        out_shape=jax.ShapeDtypeStruct((n_layers, r, n_out), F32),
        compiler_params=_cparams("arbitrary", "arbitrary"),
        name="modulation",
    )(cond, w_mod, b_mod.reshape(n_layers, 1, n_out))


def _norm_mod(x_ref, mod_ref, nw_ref, h_scr, row):
    h = _rms(x_ref[...], nw_ref[...])
    h = h * (1.0 + mod_ref[row + 1:row + 2, :]) + mod_ref[row:row + 1, :]
    h_scr[...] = h.astype(BF16)


def _norm_proj_kernel(x_ref, mod_ref, nw_ref, w_ref, o_ref, h_scr, *, row):
    @pl.when(pl.program_id(1) == 0)
    def _():
        _norm_mod(x_ref, mod_ref, nw_ref, h_scr, row)

    o_ref[...] = _dot(h_scr[...], w_ref[...].astype(BF16)).astype(o_ref.dtype)


def _norm_proj(x, mod, mod_row0, rows_per_cond, norm_w, layer, w, w_idx, *, row, out_dtype):
    m, d = x.shape
    n_out = w.shape[-1]
    tm, tn = ROW_TILE, COL_TILE
    cond_of = lambda i: mod_row0 + (i * tm) // rows_per_cond
    return pl.pallas_call(
        functools.partial(_norm_proj_kernel, row=row),
        grid=(m // tm, n_out // tn),
        in_specs=[
            pl.BlockSpec((tm, d), lambda i, j: (i, 0)),
            pl.BlockSpec((None, 6, d), lambda i, j: (cond_of(i), 0, 0)),
            pl.BlockSpec((None, 1, d), lambda i, j: (layer, 0, 0)),
            pl.BlockSpec((None, d, tn), lambda i, j: (w_idx, 0, j)),
        ],
        out_specs=pl.BlockSpec((tm, tn), lambda i, j: (i, j)),
        out_shape=jax.ShapeDtypeStruct((m, n_out), out_dtype),
        scratch_shapes=[pltpu.VMEM((tm, d), BF16)],
        compiler_params=_cparams("parallel", "arbitrary"),
        name="norm_proj",
    )(x, mod, norm_w, w)


def _conv3(u, cw, cb, first, last):
    t = u.shape[0]
    left = jnp.where(first, 0.0, pltpu.roll(u, 1, axis=0))
    right = jnp.where(last, 0.0, pltpu.roll(u, t - 1, axis=0))
    return left * cw[0:1, :] + u * cw[1:2, :] + right * cw[2:3, :] + cb


def _ffn_up_kernel(x_ref, mod_ref, nw_ref, wa_ref, wv_ref, cwa_ref, cwv_ref, cba_ref, cbv_ref,
                   o_ref, h_scr, *, seq_len):
    @pl.when(pl.program_id(1) == 0)
    def _():
        _norm_mod(x_ref, mod_ref, nw_ref, h_scr, 3)

    h = h_scr[...]
    tm = h.shape[0]
    pos = lax.broadcasted_iota(jnp.int32, (tm, 1), 0) % seq_len
    first = pos == 0
    last = pos == seq_len - 1
    a = _conv3(_dot(h, wa_ref[...].astype(BF16)), cwa_ref[...], cba_ref[...], first, last)
    v = _conv3(_dot(h, wv_ref[...].astype(BF16)), cwv_ref[...], cbv_ref[...], first, last)
    o_ref[...] = (a * _sigmoid(a) * v).astype(o_ref.dtype)


def _ffn_up(x, mod, mod_row0, rows_per_cond, norm_w, layer, w_up, conv_w, conv_b, seq_len):
    m, d = x.shape
    d_ff = w_up.shape[-1] // 2
    tm, tn = ROW_TILE, COL_TILE
    nj = d_ff // tn
    cond_of = lambda i: mod_row0 + (i * tm) // rows_per_cond
    conv_b3 = conv_b.reshape(conv_b.shape[0], 1, 2 * d_ff)
    return pl.pallas_call(
        functools.partial(_ffn_up_kernel, seq_len=seq_len),
        grid=(m // tm, nj),
        in_specs=[
            pl.BlockSpec((tm, d), lambda i, j: (i, 0)),
            pl.BlockSpec((None, 6, d), lambda i, j: (cond_of(i), 0, 0)),
            pl.BlockSpec((None, 1, d), lambda i, j: (layer, 0, 0)),
            pl.BlockSpec((None, d, tn), lambda i, j: (layer, 0, j)),
            pl.BlockSpec((None, d, tn), lambda i, j: (layer, 0, nj + j)),
            pl.BlockSpec((None, 3, tn), lambda i, j: (layer, 0, j)),
            pl.BlockSpec((None, 3, tn), lambda i, j: (layer, 0, nj + j)),
            pl.BlockSpec((None, 1, tn), lambda i, j: (layer, 0, j)),
            pl.BlockSpec((None, 1, tn), lambda i, j: (layer, 0, nj + j)),
        ],
        out_specs=pl.BlockSpec((tm, tn), lambda i, j: (i, j)),
        out_shape=jax.ShapeDtypeStruct((m, d_ff), BF16),
        scratch_shapes=[pltpu.VMEM((tm, d), BF16)],
        compiler_params=_cparams("parallel", "arbitrary"),
        name="ffn_up",
    )(x, mod, norm_w, w_up, w_up, conv_w, conv_w, conv_b3, conv_b3)


def _proj_res_kernel(*refs, n_in, row):
    a_refs = refs[:n_in]
    w_refs = refs[n_in:2 * n_in]
    y_ref, mod_ref, o_ref = refs[2 * n_in:]
    acc = _dot(a_refs[0][...], w_refs[0][...].astype(BF16))
    for a_ref, w_ref in zip(a_refs[1:], w_refs[1:]):
        acc += _dot(a_ref[...], w_ref[...].astype(BF16))
    o_ref[...] = y_ref[...] + mod_ref[row:row + 1, :] * acc


def _proj_res(acts, w, w_idx, y, mod, mod_row0, rows_per_cond, *, row, tn):
    m, d = y.shape
    tm = ROW_TILE
    n_in = len(acts)
    cond_of = lambda i: mod_row0 + (i * tm) // rows_per_cond
    in_specs = [pl.BlockSpec((tm, a.shape[1]), lambda i, j: (i, 0)) for a in acts]
    k_blk = acts[0].shape[1]
    assert all(a.shape[1] == k_blk for a in acts)
    for k in range(n_in):
        in_specs.append(pl.BlockSpec((None, k_blk, tn), lambda i, j, k=k: (w_idx, k, j)))
    in_specs += [
        pl.BlockSpec((tm, tn), lambda i, j: (i, j)),
        pl.BlockSpec((None, 6, tn), lambda i, j: (cond_of(i), 0, j)),
    ]
    return pl.pallas_call(
        functools.partial(_proj_res_kernel, n_in=n_in, row=row),
        grid=(m // tm, d // tn),
        in_specs=in_specs,
        out_specs=pl.BlockSpec((tm, tn), lambda i, j: (i, j)),
        out_shape=jax.ShapeDtypeStruct((m, d), F32),
        compiler_params=_cparams("parallel", "arbitrary"),
        name="proj_res",
    )(*acts, *([w] * n_in), y, mod)


def _rope_half_roll(x, cos2, sin2):
    return x * cos2 + pltpu.roll(x, x.shape[1] // 2, axis=1) * sin2


def _attn_kernel(*refs, n, n_ctx, rope, emit_kv, tq):
    q_ref, k_ref, v_ref, qw_ref, kw_ref = refs[:5]
    pos = 5
    if rope:
        cos_ref, sin_ref, ck_ref, cv_ref = refs[pos:pos + 4]
        pos += 4
    o_ref = refs[pos]
    pos += 1
    if emit_kv:
        nk_ref, nv_ref = refs[pos:pos + 2]
        pos += 2
    kall, vall = refs[pos:pos + 2]

    kn = _rms(k_ref[...].astype(F32), kw_ref[...])
    vv = v_ref[...]
    if emit_kv:
        nk_ref[...] = kn
        nv_ref[...] = vv.astype(F32)
    if rope:
        kn = _rope_half_roll(kn, cos_ref[...], sin_ref[...])
        kall[0:n_ctx, :] = ck_ref[...].astype(BF16)
        vall[0:n_ctx, :] = cv_ref[...].astype(BF16)
    kall[n_ctx:n_ctx + n, :] = kn.astype(BF16)
    vall[n_ctx:n_ctx + n, :] = vv.astype(BF16)

    scale = A_HEAD_DIM ** -0.5

    def chunk(c, carry):
        r0 = pl.multiple_of(c * tq, tq)
        rows = pl.ds(r0, tq)
        for g in range(A_GROUP):
            cols = slice(g * A_HEAD_DIM, (g + 1) * A_HEAD_DIM)
            qh = _rms(q_ref[rows, cols].astype(F32), qw_ref[...])
            if rope:
                qh = _rope_half_roll(qh, cos_ref[rows, :], sin_ref[rows, :])
            s = _dot_nt((qh * scale).astype(BF16), kall[...])
            p = jnp.exp(s - jnp.max(s, axis=-1, keepdims=True))
            l = jnp.sum(p, axis=-1, keepdims=True)
            o = _dot(p.astype(BF16), vall[...]) / l
            o_ref[rows, cols] = o.astype(o_ref.dtype)
        return carry

    lax.fori_loop(0, n // tq, chunk, 0)


def _attention(proj, n_seq, n, q_norm_w, k_norm_w, e, rope_tabs, cache_k, cache_v, emit_kv):
    rope = rope_tabs is not None
    n_ctx = cache_k.shape[2] if rope else 0
    hd = A_HEAD_DIM
    qcols = A_GROUP * hd
    in_specs = [
        pl.BlockSpec((n, qcols), lambda b, kv: (b, kv)),
        pl.BlockSpec((n, hd), lambda b, kv: (b, A_Q // hd + kv)),
        pl.BlockSpec((n, hd), lambda b, kv: (b, (A_Q + A_KV) // hd + kv)),
        pl.BlockSpec((None, 1, hd), lambda b, kv: (e, 0, 0)),
        pl.BlockSpec((None, 1, hd), lambda b, kv: (e, 0, 0)),
    ]
    args = [proj, proj, proj, q_norm_w, k_norm_w]
    if rope:
        cos2, sin2 = rope_tabs
        in_specs += [
            pl.BlockSpec((n, hd), lambda b, kv: (0, 0)),
            pl.BlockSpec((n, hd), lambda b, kv: (0, 0)),
            pl.BlockSpec((None, None, n_ctx, hd), lambda b, kv: (b, e, 0, kv)),
            pl.BlockSpec((None, None, n_ctx, hd), lambda b, kv: (b, e, 0, kv)),
        ]
        ck = cache_k.reshape(cache_k.shape[0], cache_k.shape[1], n_ctx, A_KV)
        cv = cache_v.reshape(cache_v.shape[0], cache_v.shape[1], n_ctx, A_KV)
        args += [cos2, sin2, ck, cv]
    out_specs = [pl.BlockSpec((n, qcols), lambda b, kv: (b, kv))]
    out_shape = [jax.ShapeDtypeStruct((n_seq * n, A_Q), BF16)]
    if emit_kv:
        out_specs += [pl.BlockSpec((n, hd), lambda b, kv: (b, kv))] * 2
        out_shape += [jax.ShapeDtypeStruct((n_seq * n, A_KV), F32)] * 2
    return pl.pallas_call(
        functools.partial(_attn_kernel, n=n, n_ctx=n_ctx, rope=rope, emit_kv=emit_kv,
                          tq=min(n, 256)),
        grid=(n_seq, A_KV_HEADS),
        in_specs=in_specs,
        out_specs=out_specs,
        out_shape=out_shape,
        scratch_shapes=[pltpu.VMEM((n_ctx + n, hd), BF16), pltpu.VMEM((n_ctx + n, hd), BF16)],
        compiler_params=_cparams("parallel", "arbitrary"),
        name="attention",
    )(*args)


def _gla_levels(c, rev):
    row = lax.broadcasted_iota(jnp.int32, (c, c), 0)
    col = lax.broadcasted_iota(jnp.int32, (c, c), 1)
    shift = HGRN_BASE.bit_length() - 1
    x = (row >> shift) ^ (col >> shift)
    lvl = jnp.zeros((c, c), jnp.int32)
    for l in range(1, (c // HGRN_BASE).bit_length()):
        lvl = jnp.where(x >= (1 << (l - 1)), l, lvl)
    causal = (col >= row) if rev else (col <= row)
    return jnp.where(causal, lvl, -1)


def _gla_block(q, k, v, lf, st_t, lvl, rev):
    c, dk = q.shape
    tri = jnp.where(lvl >= 0, 1.0, 0.0).astype(BF16)
    hi = lf.astype(BF16)
    r1 = lf - hi.astype(F32)
    mid = r1.astype(BF16)
    lo = (r1 - mid.astype(F32)).astype(BF16)
    b = _dot(tri, hi) + _dot(tri, mid) + _dot(tri, lo)
    tot = b[0:1, :] if rev else b[c - 1:c, :]

    o = _dot_nt((q * jnp.exp(b)).astype(BF16), st_t.astype(BF16))
    khat = (k * jnp.exp(tot - b)).astype(BF16)
    vb = v.astype(BF16)

    b3 = b.reshape(c // HGRN_BASE, HGRN_BASE, dk)
    mid_row = HGRN_BASE // 2 if rev else HGRN_BASE // 2 - 1
    a = jnp.clip(b3 - b3[:, mid_row:mid_row + 1, :], -EXP_CLAMP, EXP_CLAMP).reshape(c, dk)
    p = _dot_nt((q * jnp.exp(a)).astype(BF16), (k * jnp.exp(-a)).astype(BF16))
    scores = jnp.where(lvl == 0, p, 0.0)
    h, level = HGRN_BASE, 1
    while h < c:
        b3 = b.reshape(c // (2 * h), 2 * h, dk)
        ref_row = h if rev else h - 1
        a = (b3 - b3[:, ref_row:ref_row + 1, :]).reshape(c, dk)
        p = _dot_nt((q * jnp.exp(jnp.minimum(a, 0.0))).astype(BF16),
                    (k * jnp.exp(jnp.minimum(-a, 0.0))).astype(BF16))
        scores = jnp.where(lvl == level, p, scores)
        h, level = 2 * h, level + 1
    o = o + _dot(scores.astype(BF16), vb)
    st_new = st_t * jnp.exp(tot) + _dot_tn(vb, khat)
    return o, st_new


def _hgrn_kernel(*refs, n, has_state, emit_state):
    q_ref, i_ref, zf_ref, zb_ref, g_ref, lb_ref, nw_ref = refs[:7]
    pos = 7
    if has_state:
        s0f_ref, s0b_ref = refs[pos:pos + 2]
        pos += 2
    o_ref = refs[pos]
    pos += 1
    if emit_state:
        sf_ref, sb_ref = refs[pos:pos + 2]
        pos += 2
    of_scr = refs[pos]

    c = MIX_BLOCK
    nb = n // c
    lb = lb_ref[...]
    qscale = B_DK ** -0.5

    def load(z_ref, rows):
        f = lb + (1.0 - lb) * _sigmoid(z_ref[rows, :].astype(F32))
        return (q_ref[rows, :].astype(F32) * qscale, 1.0 - f, i_ref[rows, :].astype(F32),
                jnp.log(f))

    def fwd(j, st):
        rows = pl.ds(pl.multiple_of(j * c, c), c)
        o, st = _gla_block(*load(zf_ref, rows), st, lvl_f, False)
        of_scr[rows, :] = o
        return st

    def bwd(j, st):
        rows = pl.ds(pl.multiple_of((nb - 1 - j) * c, c), c)
        o, st = _gla_block(*load(zb_ref, rows), st, lvl_b, True)
        y = _rms(of_scr[rows, :] + o, nw_ref[...])
        g = g_ref[rows, :].astype(F32)
        o_ref[rows, :] = (y * (g * _sigmoid(g))).astype(o_ref.dtype)
        return st

    zero = jnp.zeros((B_DV, B_DK), F32)
    lvl_f = _gla_levels(c, False)
    lvl_b = _gla_levels(c, True)
    st_f = lax.fori_loop(0, nb, fwd, s0f_ref[...].T if has_state else zero)
    st_b = lax.fori_loop(0, nb, bwd, s0b_ref[...].T if has_state else zero)
    if emit_state:
        sf_ref[...] = st_f.T
        sb_ref[...] = st_b.T


def _hgrn(proj, n_seq, n, lb, o_norm_w, e, state_f, state_b, emit_state):
    has_state = state_f is not None
    d = B_DK
    base = (A_Q + 2 * A_KV) // d
    col = lambda k: (lambda b, h: (b, base + k * B_HEADS + h))
    in_specs = [pl.BlockSpec((n, d), col(k)) for k in range(5)]
    in_specs += [
        pl.BlockSpec((None, 1, d), lambda b, h: (h, 0, 0)),
        pl.BlockSpec((None, 1, B_DV), lambda b, h: (e, 0, 0)),
    ]
    args = [proj] * 5 + [lb, o_norm_w]
    if has_state:
        st_spec = pl.BlockSpec((None, None, None, B_DK, B_DV), lambda b, h: (b, e, h, 0, 0))
        in_specs += [st_spec, st_spec]
        args += [state_f, state_b]
    out_specs = [pl.BlockSpec((n, B_DV), lambda b, h: (b, h))]
    out_shape = [jax.ShapeDtypeStruct((n_seq * n, B_V), BF16)]
    if emit_state:
        so = pl.BlockSpec((None, None, B_DK, B_DV), lambda b, h: (b, h, 0, 0))
        out_specs += [so, so]
        out_shape += [jax.ShapeDtypeStruct((n_seq, B_HEADS, B_DK, B_DV), F32)] * 2
    return pl.pallas_call(
        functools.partial(_hgrn_kernel, n=n, has_state=has_state, emit_state=emit_state),
        grid=(n_seq, B_HEADS),
        in_specs=in_specs,
        out_specs=out_specs,
        out_shape=out_shape,
        scratch_shapes=[pltpu.VMEM((n, B_DV), F32)],
        compiler_params=_cparams("parallel", "arbitrary"),
        name="hgrn2",
    )(*args)


def _rope_split(x, cos, sin):
    half = x.shape[1] // 2
    x1, x2 = x[:, :half], x[:, half:]
    return jnp.concatenate([x1 * cos - x2 * sin, x1 * sin + x2 * cos], axis=1)


def _ret_kernel(*refs, n, rope, has_state, emit_state):
    q_ref, k_ref, v_ref, g_ref, lgf_ref, lgb_ref, nw_ref = refs[:7]
    pos = 7
    if rope:
        cos_ref, sin_ref = refs[pos:pos + 2]
        pos += 2
    if has_state:
        s0f_ref, s0b_ref = refs[pos:pos + 2]
        pos += 2
    o_ref = refs[pos]
    pos += 1
    if emit_state:
        sf_ref, sb_ref = refs[pos:pos + 2]
        pos += 2
    of_scr, st_scr = refs[pos:pos + 2]

    c = MIX_BLOCK
    nb = n // c
    kscale = C_DK ** -0.5
    rowi = lax.broadcasted_iota(jnp.int32, (c, c), 0)
    coli = lax.broadcasted_iota(jnp.int32, (c, c), 1)
    rowq = lax.broadcasted_iota(jnp.int32, (c, C_DK), 0).astype(F32)

    def decays(lg, rev):
        dist = (coli - rowi) if rev else (rowi - coli)
        dmat = jnp.where(dist >= 0, jnp.exp(lg[:, :c] * jnp.maximum(dist, 0).astype(F32)), 0.0)
        lgq = lg[:, :C_DK]
        q_dec = jnp.exp(lgq * ((c - rowq) if rev else (rowq + 1.0)))
        k_dec = jnp.exp(lgq * (rowq if rev else (c - 1.0 - rowq)))
        return dmat, q_dec, k_dec, jnp.exp(lg * float(c))

    def block(rows, dec):
        dmat, q_dec, k_dec, cd = dec
        q = q_ref[rows, :].astype(F32)
        k = k_ref[rows, :].astype(F32)
        if rope:
            q = _rope_split(q, cos_ref[rows, :], sin_ref[rows, :])
            k = _rope_split(k, cos_ref[rows, :], sin_ref[rows, :])
        k = k * kscale
        vb = v_ref[rows, :]
        s = _dot_nt(q.astype(BF16), k.astype(BF16)) * dmat
        st = st_scr[...]
        o = _dot(s.astype(BF16), vb) + _dot((q * q_dec).astype(BF16), st.astype(BF16))
        st_scr[...] = st * cd + _dot_tn((k * k_dec).astype(BF16), vb)
        return o

    dec_f = decays(lgf_ref[...], False)
    dec_b = decays(lgb_ref[...], True)

    def fwd(j, carry):
        rows = pl.ds(pl.multiple_of(j * c, c), c)
        of_scr[rows, :] = block(rows, dec_f)
        return carry

    def bwd(j, carry):
        rows = pl.ds(pl.multiple_of((nb - 1 - j) * c, c), c)
        o = of_scr[rows, :] + block(rows, dec_b)
        y = _rms(o, nw_ref[...])
        g = g_ref[rows, :].astype(F32)
        o_ref[rows, :] = (y * (g * _sigmoid(g))).astype(o_ref.dtype)
        return carry

    st_scr[...] = s0f_ref[...] if has_state else jnp.zeros((C_DK, C_DV), F32)
    lax.fori_loop(0, nb, fwd, 0)
    if emit_state:
        sf_ref[...] = st_scr[...]
    st_scr[...] = s0b_ref[...] if has_state else jnp.zeros((C_DK, C_DV), F32)
    lax.fori_loop(0, nb, bwd, 0)
    if emit_state:
        sb_ref[...] = st_scr[...]


def _retention(proj, n_seq, n, lg_f, lg_b, o_norm_w, o_idx, rope_tabs, state_f, state_b,
               emit_state):
    rope = rope_tabs is not None
    has_state = state_f is not None
    nq = C_QK // C_DK
    in_specs = [
        pl.BlockSpec((n, C_DK), lambda b, h: (b, h)),
        pl.BlockSpec((n, C_DK), lambda b, h: (b, nq + h)),
        pl.BlockSpec((n, C_DV), lambda b, h: (b, 2 * C_QK // C_DV + h)),
        pl.BlockSpec((n, C_DV), lambda b, h: (b, (2 * C_QK + C_V) // C_DV + h)),
        pl.BlockSpec((None, 1, C_DV), lambda b, h: (h, 0, 0)),
        pl.BlockSpec((None, 1, C_DV), lambda b, h: (h, 0, 0)),
        pl.BlockSpec((None, 1, C_DV), lambda b, h: (o_idx, 0, 0)),
    ]
    args = [proj] * 4 + [lg_f, lg_b, o_norm_w]
    if rope:
        in_specs += [pl.BlockSpec((n, C_DK // 2), lambda b, h: (0, 0))] * 2
        args += list(rope_tabs)
    if has_state:
        st_spec = pl.BlockSpec((None, None, None, C_DK, C_DV), lambda b, h: (b, o_idx, h, 0, 0))
        in_specs += [st_spec, st_spec]
        args += [state_f, state_b]
    out_specs = [pl.BlockSpec((n, C_DV), lambda b, h: (b, h))]
    out_shape = [jax.ShapeDtypeStruct((n_seq * n, C_V), BF16)]
    if emit_state:
        so = pl.BlockSpec((None, None, C_DK, C_DV), lambda b, h: (b, h, 0, 0))
        out_specs += [so, so]
        out_shape += [jax.ShapeDtypeStruct((n_seq, C_HEADS, C_DK, C_DV), F32)] * 2
    return pl.pallas_call(
        functools.partial(_ret_kernel, n=n, rope=rope, has_state=has_state,
                          emit_state=emit_state),
        grid=(n_seq, C_HEADS),
        in_specs=in_specs,
        out_specs=out_specs,
        out_shape=out_shape,
        scratch_shapes=[pltpu.VMEM((n, C_DV), F32), pltpu.VMEM((C_DK, C_DV), F32)],
        compiler_params=_cparams("parallel", "arbitrary"),
        name="retention",
    )(*args)


def _rope_tables(n_tokens, head_dim):
    rows = n_tokens // GRID_W
    row = jnp.repeat(jnp.arange(rows, dtype=F32), GRID_W)
    col = jnp.tile(jnp.arange(GRID_W, dtype=F32), rows)
    quarter = head_dim // 4
    inv_freq = jnp.power(ROPE_BASE, -jnp.arange(quarter, dtype=F32) / quarter)
    ang = jnp.concatenate([row[:, None] * inv_freq, col[:, None] * inv_freq], axis=-1)
    return jnp.cos(ang), jnp.sin(ang)


def kernel(x_prompt, x_sample, cache_attn_k, cache_attn_v, state_hgrn_fwd, state_hgrn_bwd,
           state_ret_fwd, state_ret_bwd, c, c_ctx, w_mod, b_mod, norm_mix_w, norm_ffn_w,
           w_in_even, w_out_even, attn_q_norm_w, attn_k_norm_w, hgrn_lb, hgrn_o_norm_w,
           w_in_odd, w_out_odd, ret_decay_fwd, ret_decay_bwd, ret_o_norm_w,
           w_up, conv_w, conv_b, w_down):
    depth, d_model = norm_mix_w.shape
    bp, np_, _ = x_prompt.shape
    bs, ns, _ = x_sample.shape

    lb_all = jnp.cumsum(jax.nn.softmax(hgrn_lb.astype(F32), axis=0), axis=0)
    lg_f = jnp.broadcast_to(jax.nn.log_sigmoid(ret_decay_fwd.astype(F32))[:, :, None, None],
                            ret_decay_fwd.shape + (1, C_DV))
    lg_b = jnp.broadcast_to(jax.nn.log_sigmoid(ret_decay_bwd.astype(F32))[:, :, None, None],
                            ret_decay_bwd.shape + (1, C_DV))
    cos_a, sin_a = _rope_tables(ns, A_HEAD_DIM)
    rope_a = (jnp.concatenate([cos_a, cos_a], axis=1), jnp.concatenate([-sin_a, sin_a], axis=1))
    rope_c = _rope_tables(ns, C_DK)
    nmw = norm_mix_w.reshape(depth, 1, d_model)
    nfw = norm_ffn_w.reshape(depth, 1, d_model)
    qnw = attn_q_norm_w.reshape(-1, 1, A_HEAD_DIM)
    knw = attn_k_norm_w.reshape(-1, 1, A_HEAD_DIM)
    hnw = hgrn_o_norm_w.reshape(-1, 1, B_DV)
    rnw = ret_o_norm_w.reshape(-1, 1, C_DV)

    n_cond = 1 + bs
    pad = (-n_cond) % 8
    cond = jnp.concatenate([c_ctx[None, :], c, jnp.zeros((pad, d_model), F32)], axis=0)
    mod_all = _modulation(cond, w_mod, b_mod).reshape(depth, n_cond + pad, 6, d_model)

    groups = (
        dict(x=x_prompt.reshape(bp * np_, d_model), n_seq=bp, n=np_, row0=0,
             rows_per_cond=bp * np_, latent=False),
        dict(x=x_sample.reshape(bs * ns, d_model), n_seq=bs, n=ns, row0=1,
             rows_per_cond=ns, latent=True),
    )
    results = []
    for grp in groups:
        y, n_seq, n = grp["x"], grp["n_seq"], grp["n"]
        row0, rpc, latent = grp["row0"], grp["rows_per_cond"], grp["latent"]
        new = dict(k=[], v=[], hf=[], hb=[], rf=[], rb=[])
        for l in range(depth):
            mod = mod_all[l]
            if l % 2 == 0:
                e = l // 2
                proj = _norm_proj(y, mod, row0, rpc, nmw, l, w_in_even, e, row=0, out_dtype=BF16)
                att = _attention(proj, n_seq, n, qnw, knw, e, rope_a if latent else None,
                                 cache_attn_k, cache_attn_v, emit_kv=not latent)
                lb = lb_all[e].reshape(B_HEADS, 1, B_DK)
                hg = _hgrn(proj, n_seq, n, lb, hnw, e,
                           state_hgrn_fwd if latent else None,
                           state_hgrn_bwd if latent else None, emit_state=not latent)
                if not latent:
                    new["k"].append(att[1].reshape(n_seq, n, A_KV_HEADS, A_HEAD_DIM))
                    new["v"].append(att[2].reshape(n_seq, n, A_KV_HEADS, A_HEAD_DIM))
                    new["hf"].append(hg[1])
                    new["hb"].append(hg[2])
                y = _proj_res([att[0], hg[0]], w_out_even, e, y, mod, row0, rpc, row=2, tn=512)
            else:
                o = l // 2
                proj = _norm_proj(y, mod, row0, rpc, nmw, l, w_in_odd, o, row=0, out_dtype=BF16)
                rt = _retention(proj, n_seq, n, lg_f[o], lg_b[o], rnw, o,
                                rope_c if latent else None,
                                state_ret_fwd if latent else None,
                                state_ret_bwd if latent else None, emit_state=not latent)
                if not latent:
                    new["rf"].append(rt[1])
                    new["rb"].append(rt[2])
                y = _proj_res([rt[0]], w_out_odd, o, y, mod, row0, rpc, row=2, tn=512)
            act = _ffn_up(y, mod, row0, rpc, nfw, l, w_up, conv_w, conv_b, n)
            y = _proj_res([act], w_down, l, y, mod, row0, rpc, row=5, tn=256)
        results.append((y.reshape(n_seq, n, d_model), new))

    (y_p, new), (y_s, _) = results
    stack = lambda xs: jnp.stack(xs, axis=1)
    return (y_p, y_s, stack(new["k"]), stack(new["v"]), stack(new["hf"]), stack(new["hb"]),
            stack(new["rf"]), stack(new["rb"]))
```

```python
import functools

import jax
import jax.numpy as jnp
import numpy as np
from jax import lax
from jax.experimental import pallas as pl
from jax.experimental.pallas import tpu as pltpu

F32 = jnp.float32
BF16 = jnp.bfloat16

GRID_W = 64
A_HEADS = 8
A_KV_HEADS = 2
A_HEAD_DIM = 128
ROPE_BASE = 10000.0
B_HEADS = 8
B_DK = 128
B_DV = 128
C_HEADS = 8
C_DK = 256
C_DV = 512
EPS = 1e-6

A_Q = A_HEADS * A_HEAD_DIM
A_KV = A_KV_HEADS * A_HEAD_DIM
A_GROUP = A_HEADS // A_KV_HEADS
B_QK = B_HEADS * B_DK
B_V = B_HEADS * B_DV
C_QK = C_HEADS * C_DK
C_V = C_HEADS * C_DV

VMEM_LIMIT_BYTES = 56 * 1024 * 1024
ROW_TILE = 1024
COL_TILE = 512
MIX_BLOCK = 128
RET_BLOCK = 256
HGRN_BASE = 16
EXP_CLAMP = 80.0


def _cparams(*sem):
    return pltpu.CompilerParams(dimension_semantics=sem, vmem_limit_bytes=VMEM_LIMIT_BYTES)


def _dot(a, b):
    return jnp.dot(a, b, preferred_element_type=F32)


def _dot_nt(a, b):
    return lax.dot_general(a, b, (((1,), (1,)), ((), ())), preferred_element_type=F32)


def _dot_tn(a, b):
    return lax.dot_general(a, b, (((0,), (0,)), ((), ())), preferred_element_type=F32)


def _sigmoid(x):
    return 1.0 / (1.0 + jnp.exp(-x))


def _rms(x, w):
    return x * lax.rsqrt(jnp.mean(x * x, axis=-1, keepdims=True) + EPS) * w


def _mod_kernel(c_ref, w_ref, b_ref, o_ref):
    c = c_ref[...]
    s = (c * _sigmoid(c)).astype(BF16)
    o_ref[...] = _dot(s, w_ref[...].astype(BF16)) + b_ref[...]


def _modulation(cond, w_mod, b_mod):
    n_layers, d, n_out = w_mod.shape
    r = cond.shape[0]
    tn = 1024
    return pl.pallas_call(
        _mod_kernel,
        grid=(n_layers, n_out // tn),
        in_specs=[
            pl.BlockSpec((r, d), lambda l, j: (0, 0)),
            pl.BlockSpec((None, d, tn), lambda l, j: (l, 0, j)),
            pl.BlockSpec((None, 1, tn), lambda l, j: (l, 0, j)),
        ],
        out_specs=pl.BlockSpec((None, r, tn), lambda l, j: (l, 0, j)),
        out_shape=jax.ShapeDtypeStruct((n_layers, r, n_out), F32),
        compiler_params=_cparams("arbitrary", "arbitrary"),
        name="modulation",
    )(cond, w_mod, b_mod.reshape(n_layers, 1, n_out))


def _norm_mod(x_ref, mod_ref, nw_ref, h_scr, row):
    h = _rms(x_ref[...], nw_ref[...])
    h = h * (1.0 + mod_ref[row + 1:row + 2, :]) + mod_ref[row:row + 1, :]
    h_scr[...] = h.astype(BF16)


def _norm_proj_kernel(x_ref, mod_ref, nw_ref, w_ref, o_ref, h_scr, *, row):
    @pl.when(pl.program_id(1) == 0)
    def _():
        _norm_mod(x_ref, mod_ref, nw_ref, h_scr, row)

    o_ref[...] = _dot(h_scr[...], w_ref[...]).astype(o_ref.dtype)


def _norm_proj(x, mod, mod_row0, rows_per_cond, norm_w, layer, w, w_idx, *, row, out_dtype):
    m, d = x.shape
    n_out = w.shape[-1]
    tm, tn = ROW_TILE, COL_TILE
    cond_of = lambda i: mod_row0 + (i * tm) // rows_per_cond
    return pl.pallas_call(
        functools.partial(_norm_proj_kernel, row=row),
        grid=(m // tm, n_out // tn),
        in_specs=[
            pl.BlockSpec((tm, d), lambda i, j: (i, 0)),
            pl.BlockSpec((None, 6, d), lambda i, j: (cond_of(i), 0, 0)),
            pl.BlockSpec((None, 1, d), lambda i, j: (layer, 0, 0)),
            pl.BlockSpec((None, d, tn), lambda i, j: (w_idx, 0, j)),
        ],
        out_specs=pl.BlockSpec((tm, tn), lambda i, j: (i, j)),
        out_shape=jax.ShapeDtypeStruct((m, n_out), out_dtype),
        scratch_shapes=[pltpu.VMEM((tm, d), BF16)],
        compiler_params=_cparams("parallel", "arbitrary"),
        name="norm_proj",
    )(x, mod, norm_w, w)


def _conv3(u, cw, cb, first, last):
    t = u.shape[0]
    left = jnp.where(first, 0.0, pltpu.roll(u, 1, axis=0))
    right = jnp.where(last, 0.0, pltpu.roll(u, t - 1, axis=0))
    return left * cw[0:1, :] + u * cw[1:2, :] + right * cw[2:3, :] + cb


def _ffn_up_kernel(x_ref, mod_ref, nw_ref, wa_ref, wv_ref, cwa_ref, cwv_ref, cba_ref, cbv_ref,
                   o_ref, h_scr, *, seq_len):
    @pl.when(pl.program_id(1) == 0)
    def _():
        _norm_mod(x_ref, mod_ref, nw_ref, h_scr, 3)

    h = h_scr[...]
    tm = h.shape[0]
    pos = lax.broadcasted_iota(jnp.int32, (tm, 1), 0) % seq_len
    first = pos == 0
    last = pos == seq_len - 1
    a = _conv3(_dot(h, wa_ref[...]), cwa_ref[...], cba_ref[...], first, last)
    v = _conv3(_dot(h, wv_ref[...]), cwv_ref[...], cbv_ref[...], first, last)
    o_ref[...] = (a * _sigmoid(a) * v).astype(o_ref.dtype)


def _ffn_up(x, mod, mod_row0, rows_per_cond, norm_w, layer, w_up, conv_w, conv_b, seq_len):
    m, d = x.shape
    d_ff = w_up.shape[-1] // 2
    tm, tn = ROW_TILE, COL_TILE
    nj = d_ff // tn
    cond_of = lambda i: mod_row0 + (i * tm) // rows_per_cond
    conv_b3 = conv_b.reshape(conv_b.shape[0], 1, 2 * d_ff)
    return pl.pallas_call(
        functools.partial(_ffn_up_kernel, seq_len=seq_len),
        grid=(m // tm, nj),
        in_specs=[
            pl.BlockSpec((tm, d), lambda i, j: (i, 0)),
            pl.BlockSpec((None, 6, d), lambda i, j: (cond_of(i), 0, 0)),
            pl.BlockSpec((None, 1, d), lambda i, j: (layer, 0, 0)),
            pl.BlockSpec((None, d, tn), lambda i, j: (layer, 0, j)),
            pl.BlockSpec((None, d, tn), lambda i, j: (layer, 0, nj + j)),
            pl.BlockSpec((None, 3, tn), lambda i, j: (layer, 0, j)),
            pl.BlockSpec((None, 3, tn), lambda i, j: (layer, 0, nj + j)),
            pl.BlockSpec((None, 1, tn), lambda i, j: (layer, 0, j)),
            pl.BlockSpec((None, 1, tn), lambda i, j: (layer, 0, nj + j)),
        ],
        out_specs=pl.BlockSpec((tm, tn), lambda i, j: (i, j)),
        out_shape=jax.ShapeDtypeStruct((m, d_ff), BF16),
        scratch_shapes=[pltpu.VMEM((tm, d), BF16)],
        compiler_params=_cparams("parallel", "arbitrary"),
        name="ffn_up",
    )(x, mod, norm_w, w_up, w_up, conv_w, conv_w, conv_b3, conv_b3)


def _proj_res_kernel(*refs, n_in, row):
    a_refs = refs[:n_in]
    w_refs = refs[n_in:2 * n_in]
    y_ref, mod_ref, o_ref = refs[2 * n_in:]
    acc = _dot(a_refs[0][...], w_refs[0][...])
    for a_ref, w_ref in zip(a_refs[1:], w_refs[1:]):
        acc += _dot(a_ref[...], w_ref[...])
    o_ref[...] = y_ref[...] + mod_ref[row:row + 1, :] * acc


def _proj_res(acts, w, w_idx, y, mod, mod_row0, rows_per_cond, *, row, tn):
    m, d = y.shape
    tm = ROW_TILE
    n_in = len(acts)
    cond_of = lambda i: mod_row0 + (i * tm) // rows_per_cond
    in_specs = [pl.BlockSpec((tm, a.shape[1]), lambda i, j: (i, 0)) for a in acts]
    k_blk = acts[0].shape[1]
    assert all(a.shape[1] == k_blk for a in acts)
    for k in range(n_in):
        in_specs.append(pl.BlockSpec((None, k_blk, tn), lambda i, j, k=k: (w_idx, k, j)))
    in_specs += [
        pl.BlockSpec((tm, tn), lambda i, j: (i, j)),
        pl.BlockSpec((None, 6, tn), lambda i, j: (cond_of(i), 0, j)),
    ]
    return pl.pallas_call(
        functools.partial(_proj_res_kernel, n_in=n_in, row=row),
        grid=(m // tm, d // tn),
        in_specs=in_specs,
        out_specs=pl.BlockSpec((tm, tn), lambda i, j: (i, j)),
        out_shape=jax.ShapeDtypeStruct((m, d), F32),
        compiler_params=_cparams("parallel", "arbitrary"),
        name="proj_res",
    )(*acts, *([w] * n_in), y, mod)


def _rope_half_roll(x, cos2, sin2):
    return x * cos2 + pltpu.roll(x, x.shape[1] // 2, axis=1) * sin2


def _attn_kernel(*refs, n, n_ctx, rope, emit_kv, tq):
    q_ref, k_ref, v_ref, qw_ref, kw_ref = refs[:5]
    pos = 5
    if rope:
        cos_ref, sin_ref, ck_ref, cv_ref = refs[pos:pos + 4]
        pos += 4
    o_ref = refs[pos]
    pos += 1
    if emit_kv:
        nk_ref, nv_ref = refs[pos:pos + 2]
        pos += 2
    kall, vall = refs[pos:pos + 2]

    kn = _rms(k_ref[...].astype(F32), kw_ref[...])
    vv = v_ref[...]
    if emit_kv:
        nk_ref[...] = kn
        nv_ref[...] = vv.astype(F32)
    if rope:
        kn = _rope_half_roll(kn, cos_ref[...], sin_ref[...])
        kall[0:n_ctx, :] = ck_ref[...].astype(BF16)
        vall[0:n_ctx, :] = cv_ref[...].astype(BF16)
    kall[n_ctx:n_ctx + n, :] = kn.astype(BF16)
    vall[n_ctx:n_ctx + n, :] = vv.astype(BF16)

    scale = A_HEAD_DIM ** -0.5

    def chunk(c, carry):
        r0 = pl.multiple_of(c * tq, tq)
        rows = pl.ds(r0, tq)
        for g in range(A_GROUP):
            cols = slice(g * A_HEAD_DIM, (g + 1) * A_HEAD_DIM)
            qh = _rms(q_ref[rows, cols].astype(F32), qw_ref[...])
            if rope:
                qh = _rope_half_roll(qh, cos_ref[rows, :], sin_ref[rows, :])
            s = _dot_nt((qh * scale).astype(BF16), kall[...])
            p = jnp.exp(s - jnp.max(s, axis=-1, keepdims=True))
            l = jnp.sum(p, axis=-1, keepdims=True)
            o = _dot(p.astype(BF16), vall[...]) / l
            o_ref[rows, cols] = o.astype(o_ref.dtype)
        return carry

    lax.fori_loop(0, n // tq, chunk, 0)


def _attention(proj, n_seq, n, q_norm_w, k_norm_w, e, rope_tabs, cache_k, cache_v, emit_kv):
    rope = rope_tabs is not None
    n_ctx = cache_k.shape[2] if rope else 0
    hd = A_HEAD_DIM
    qcols = A_GROUP * hd
    in_specs = [
        pl.BlockSpec((n, qcols), lambda b, kv: (b, kv)),
        pl.BlockSpec((n, hd), lambda b, kv: (b, A_Q // hd + kv)),
        pl.BlockSpec((n, hd), lambda b, kv: (b, (A_Q + A_KV) // hd + kv)),
        pl.BlockSpec((None, 1, hd), lambda b, kv: (e, 0, 0)),
        pl.BlockSpec((None, 1, hd), lambda b, kv: (e, 0, 0)),
    ]
    args = [proj, proj, proj, q_norm_w, k_norm_w]
    if rope:
        cos2, sin2 = rope_tabs
        in_specs += [
            pl.BlockSpec((n, hd), lambda b, kv: (0, 0)),
            pl.BlockSpec((n, hd), lambda b, kv: (0, 0)),
            pl.BlockSpec((None, None, n_ctx, hd), lambda b, kv: (b, e, 0, kv)),
            pl.BlockSpec((None, None, n_ctx, hd), lambda b, kv: (b, e, 0, kv)),
        ]
        ck = cache_k.reshape(cache_k.shape[0], cache_k.shape[1], n_ctx, A_KV)
        cv = cache_v.reshape(cache_v.shape[0], cache_v.shape[1], n_ctx, A_KV)
        args += [cos2, sin2, ck, cv]
    out_specs = [pl.BlockSpec((n, qcols), lambda b, kv: (b, kv))]
    out_shape = [jax.ShapeDtypeStruct((n_seq * n, A_Q), BF16)]
    if emit_kv:
        out_specs += [pl.BlockSpec((n, hd), lambda b, kv: (b, kv))] * 2
        out_shape += [jax.ShapeDtypeStruct((n_seq * n, A_KV), F32)] * 2
    return pl.pallas_call(
        functools.partial(_attn_kernel, n=n, n_ctx=n_ctx, rope=rope, emit_kv=emit_kv,
                          tq=min(n, 256)),
        grid=(n_seq, A_KV_HEADS),
        in_specs=in_specs,
        out_specs=out_specs,
        out_shape=out_shape,
        scratch_shapes=[pltpu.VMEM((n_ctx + n, hd), BF16), pltpu.VMEM((n_ctx + n, hd), BF16)],
        compiler_params=_cparams("parallel", "arbitrary"),
        name="attention",
    )(*args)


def _gla_levels(c, rev):
    row = lax.broadcasted_iota(jnp.int32, (c, c), 0)
    col = lax.broadcasted_iota(jnp.int32, (c, c), 1)
    shift = HGRN_BASE.bit_length() - 1
    x = (row >> shift) ^ (col >> shift)
    lvl = jnp.zeros((c, c), jnp.int32)
    for l in range(1, (c // HGRN_BASE).bit_length()):
        lvl = jnp.where(x >= (1 << (l - 1)), l, lvl)
    causal = (col >= row) if rev else (col <= row)
    return jnp.where(causal, lvl, -1)


def _gla_block(q, k, v, lf, st_t, lvl, tri, rev):
    c, dk = q.shape
    hi = lf.astype(BF16)
    r1 = lf - hi.astype(F32)
    mid = r1.astype(BF16)
    lo = (r1 - mid.astype(F32)).astype(BF16)
    b = _dot(tri, hi) + _dot(tri, mid) + _dot(tri, lo)
    tot = b[0:1, :] if rev else b[c - 1:c, :]

    o = _dot_nt((q * jnp.exp(b)).astype(BF16), st_t.astype(BF16))
    khat = (k * jnp.exp(tot - b)).astype(BF16)
    vb = v.astype(BF16)

    b3 = b.reshape(c // HGRN_BASE, HGRN_BASE, dk)
    mid_row = HGRN_BASE // 2 if rev else HGRN_BASE // 2 - 1
    a = jnp.clip(b3 - b3[:, mid_row:mid_row + 1, :], -EXP_CLAMP, EXP_CLAMP).reshape(c, dk)
    p = _dot_nt((q * jnp.exp(a)).astype(BF16), (k * jnp.exp(-a)).astype(BF16))
    scores = jnp.where(lvl == 0, p, 0.0)
    h, level = HGRN_BASE, 1
    while h < c:
        b3 = b.reshape(c // (2 * h), 2 * h, dk)
        ref_row = h if rev else h - 1
        e = jnp.exp(-jnp.abs(b3 - b3[:, ref_row:ref_row + 1, :])).reshape(c, dk)
        p = _dot_nt((q * e).astype(BF16), (k * e).astype(BF16))
        scores = jnp.where(lvl == level, p, scores)
        h, level = 2 * h, level + 1
    o = o + _dot(scores.astype(BF16), vb)
    st_new = st_t * jnp.exp(tot) + _dot_tn(vb, khat)
    return o, st_new


def _hgrn_kernel(*refs, n, has_state, emit_state):
    q_ref, i_ref, zf_ref, zb_ref, g_ref, lb_ref, nw_ref = refs[:7]
    pos = 7
    if has_state:
        s0f_ref, s0b_ref = refs[pos:pos + 2]
        pos += 2
    o_ref = refs[pos]
    pos += 1
    if emit_state:
        sf_ref, sb_ref = refs[pos:pos + 2]
        pos += 2
    of_scr, ob_scr = refs[pos:pos + 2]

    c = MIX_BLOCK
    nb = n // c
    lb = lb_ref[...]
    qscale = B_DK ** -0.5
    lvl_f = _gla_levels(c, False)
    lvl_b = _gla_levels(c, True)
    tri_f = jnp.where(lvl_f >= 0, 1.0, 0.0).astype(BF16)
    tri_b = jnp.where(lvl_b >= 0, 1.0, 0.0).astype(BF16)

    def load(z_ref, rows):
        f = lb + (1.0 - lb) * _sigmoid(z_ref[rows, :].astype(F32))
        return (q_ref[rows, :].astype(F32) * qscale, 1.0 - f, i_ref[rows, :].astype(F32),
                jnp.log(f))

    def step(j, carry):
        st_f, st_b = carry
        rows_f = pl.ds(pl.multiple_of(j * c, c), c)
        rows_b = pl.ds(pl.multiple_of((nb - 1 - j) * c, c), c)
        o_f, st_f = _gla_block(*load(zf_ref, rows_f), st_f, lvl_f, tri_f, False)
        o_b, st_b = _gla_block(*load(zb_ref, rows_b), st_b, lvl_b, tri_b, True)
        of_scr[rows_f, :] = o_f
        ob_scr[rows_b, :] = o_b
        return st_f, st_b

    def finish(j, carry):
        rows = pl.ds(pl.multiple_of(j * c, c), c)
        y = _rms(of_scr[rows, :] + ob_scr[rows, :], nw_ref[...])
        g = g_ref[rows, :].astype(F32)
        o_ref[rows, :] = (y * (g * _sigmoid(g))).astype(o_ref.dtype)
        return carry

    zero = jnp.zeros((B_DV, B_DK), F32)
    st_f, st_b = lax.fori_loop(
        0, nb, step,
        (s0f_ref[...].T if has_state else zero, s0b_ref[...].T if has_state else zero),
        unroll=2)
    lax.fori_loop(0, nb, finish, 0)
    if emit_state:
        sf_ref[...] = st_f.T
        sb_ref[...] = st_b.T


def _hgrn(proj, n_seq, n, lb, o_norm_w, e, state_f, state_b, emit_state):
    has_state = state_f is not None
    d = B_DK
    base = (A_Q + 2 * A_KV) // d
    col = lambda k: (lambda b, h: (b, base + k * B_HEADS + h))
    in_specs = [pl.BlockSpec((n, d), col(k)) for k in range(5)]
    in_specs += [
        pl.BlockSpec((None, 1, d), lambda b, h: (h, 0, 0)),
        pl.BlockSpec((None, 1, B_DV), lambda b, h: (e, 0, 0)),
    ]
    args = [proj] * 5 + [lb, o_norm_w]
    if has_state:
        st_spec = pl.BlockSpec((None, None, None, B_DK, B_DV), lambda b, h: (b, e, h, 0, 0))
        in_specs += [st_spec, st_spec]
        args += [state_f, state_b]
    out_specs = [pl.BlockSpec((n, B_DV), lambda b, h: (b, h))]
    out_shape = [jax.ShapeDtypeStruct((n_seq * n, B_V), BF16)]
    if emit_state:
        so = pl.BlockSpec((None, None, B_DK, B_DV), lambda b, h: (b, h, 0, 0))
        out_specs += [so, so]
        out_shape += [jax.ShapeDtypeStruct((n_seq, B_HEADS, B_DK, B_DV), F32)] * 2
    return pl.pallas_call(
        functools.partial(_hgrn_kernel, n=n, has_state=has_state, emit_state=emit_state),
        grid=(n_seq, B_HEADS),
        in_specs=in_specs,
        out_specs=out_specs,
        out_shape=out_shape,
        scratch_shapes=[pltpu.VMEM((n, B_DV), F32), pltpu.VMEM((n, B_DV), F32)],
        compiler_params=_cparams("parallel", "arbitrary"),
        name="hgrn2",
    )(*args)


def _rope_split(x, cos, sin):
    half = x.shape[1] // 2
    x1, x2 = x[:, :half], x[:, half:]
    return jnp.concatenate([x1 * cos - x2 * sin, x1 * sin + x2 * cos], axis=1)


def _ret_kernel(*refs, n, rope, has_state, emit_state):
    q_ref, k_ref, v_ref, g_ref, lgf_ref, lgb_ref, nw_ref = refs[:7]
    pos = 7
    if rope:
        cos_ref, sin_ref = refs[pos:pos + 2]
        pos += 2
    if has_state:
        s0f_ref, s0b_ref = refs[pos:pos + 2]
        pos += 2
    o_ref = refs[pos]
    pos += 1
    if emit_state:
        sf_ref, sb_ref = refs[pos:pos + 2]
        pos += 2
    qs_scr, ks_scr, of_scr, ob_scr, stf_scr, stb_scr = refs[pos:pos + 6]

    c = min(RET_BLOCK, n)
    nb = n // c
    skip_inter = (not has_state) and nb == 1
    kscale = C_DK ** -0.5
    rowi = lax.broadcasted_iota(jnp.int32, (c, c), 0)
    coli = lax.broadcasted_iota(jnp.int32, (c, c), 1)
    rowq = lax.broadcasted_iota(jnp.int32, (c, C_DK), 0).astype(F32)

    def prep(j, carry):
        rows = pl.ds(pl.multiple_of(j * MIX_BLOCK, MIX_BLOCK), MIX_BLOCK)
        q = q_ref[rows, :].astype(F32)
        k = k_ref[rows, :].astype(F32)
        if rope:
            q = _rope_split(q, cos_ref[rows, :], sin_ref[rows, :])
            k = _rope_split(k, cos_ref[rows, :], sin_ref[rows, :])
        qs_scr[rows, :] = q
        ks_scr[rows, :] = k * kscale
        return carry

    def decays(lg, rev):
        dist = (coli - rowi) if rev else (rowi - coli)
        dmat = jnp.where(dist >= 0, jnp.exp(lg[:, :c] * jnp.maximum(dist, 0).astype(F32)), 0.0)
        lgq = lg[:, :C_DK]
        q_dec = jnp.exp(lgq * ((c - rowq) if rev else (rowq + 1.0)))
        k_dec = jnp.exp(lgq * (rowq if rev else (c - 1.0 - rowq)))
        return dmat, q_dec, k_dec, jnp.exp(lg * float(c))

    def block(rows, dec, st_scr):
        dmat, q_dec, k_dec, cd = dec
        q = qs_scr[rows, :]
        k = ks_scr[rows, :]
        vb = v_ref[rows, :]
        s = _dot_nt(q.astype(BF16), k.astype(BF16)) * dmat
        o = _dot(s.astype(BF16), vb)
        u = _dot_tn((k * k_dec).astype(BF16), vb)
        if skip_inter:
            st_scr[...] = u
        else:
            st = st_scr[...]
            o = o + _dot((q * q_dec).astype(BF16), st.astype(BF16))
            st_scr[...] = st * cd + u
        return o

    dec_f = decays(lgf_ref[...], False)
    dec_b = decays(lgb_ref[...], True)

    def step(j, carry):
        rows_f = pl.ds(pl.multiple_of(j * c, c), c)
        rows_b = pl.ds(pl.multiple_of((nb - 1 - j) * c, c), c)
        of_scr[rows_f, :] = block(rows_f, dec_f, stf_scr)
        ob_scr[rows_b, :] = block(rows_b, dec_b, stb_scr)
        return carry

    def finish(j, carry):
        rows = pl.ds(pl.multiple_of(j * MIX_BLOCK, MIX_BLOCK), MIX_BLOCK)
        y = _rms(of_scr[rows, :] + ob_scr[rows, :], nw_ref[...])
        g = g_ref[rows, :].astype(F32)
        o_ref[rows, :] = (y * (g * _sigmoid(g))).astype(o_ref.dtype)
        return carry

    lax.fori_loop(0, n // MIX_BLOCK, prep, 0)
    if has_state:
        stf_scr[...] = s0f_ref[...]
        stb_scr[...] = s0b_ref[...]
    elif not skip_inter:
        stf_scr[...] = jnp.zeros((C_DK, C_DV), F32)
        stb_scr[...] = jnp.zeros((C_DK, C_DV), F32)
    lax.fori_loop(0, nb, step, 0)
    lax.fori_loop(0, n // MIX_BLOCK, finish, 0)
    if emit_state:
        sf_ref[...] = stf_scr[...]
        sb_ref[...] = stb_scr[...]


def _retention(proj, n_seq, n, lg_f, lg_b, o_norm_w, o_idx, rope_tabs, state_f, state_b,
               emit_state):
    rope = rope_tabs is not None
    has_state = state_f is not None
    nq = C_QK // C_DK
    in_specs = [
        pl.BlockSpec((n, C_DK), lambda b, h: (b, h)),
        pl.BlockSpec((n, C_DK), lambda b, h: (b, nq + h)),
        pl.BlockSpec((n, C_DV), lambda b, h: (b, 2 * C_QK // C_DV + h)),
        pl.BlockSpec((n, C_DV), lambda b, h: (b, (2 * C_QK + C_V) // C_DV + h)),
        pl.BlockSpec((None, 1, C_DV), lambda b, h: (h, 0, 0)),
        pl.BlockSpec((None, 1, C_DV), lambda b, h: (h, 0, 0)),
        pl.BlockSpec((None, 1, C_DV), lambda b, h: (o_idx, 0, 0)),
    ]
    args = [proj] * 4 + [lg_f, lg_b, o_norm_w]
    if rope:
        in_specs += [pl.BlockSpec((n, C_DK // 2), lambda b, h: (0, 0))] * 2
        args += list(rope_tabs)
    if has_state:
        st_spec = pl.BlockSpec((None, None, None, C_DK, C_DV), lambda b, h: (b, o_idx, h, 0, 0))
        in_specs += [st_spec, st_spec]
        args += [state_f, state_b]
    out_specs = [pl.BlockSpec((n, C_DV), lambda b, h: (b, h))]
    out_shape = [jax.ShapeDtypeStruct((n_seq * n, C_V), BF16)]
    if emit_state:
        so = pl.BlockSpec((None, None, C_DK, C_DV), lambda b, h: (b, h, 0, 0))
        out_specs += [so, so]
        out_shape += [jax.ShapeDtypeStruct((n_seq, C_HEADS, C_DK, C_DV), F32)] * 2
    return pl.pallas_call(
        functools.partial(_ret_kernel, n=n, rope=rope, has_state=has_state,
                          emit_state=emit_state),
        grid=(n_seq, C_HEADS),
        in_specs=in_specs,
        out_specs=out_specs,
        out_shape=out_shape,
        scratch_shapes=[pltpu.VMEM((n, C_DK), F32), pltpu.VMEM((n, C_DK), F32),
                        pltpu.VMEM((n, C_DV), F32), pltpu.VMEM((n, C_DV), F32),
                        pltpu.VMEM((C_DK, C_DV), F32), pltpu.VMEM((C_DK, C_DV), F32)],
        compiler_params=_cparams("parallel", "arbitrary"),
        name="retention",
    )(*args)


def _rope_tables(n_tokens, head_dim):
    rows = n_tokens // GRID_W
    row = jnp.repeat(jnp.arange(rows, dtype=F32), GRID_W)
    col = jnp.tile(jnp.arange(GRID_W, dtype=F32), rows)
    quarter = head_dim // 4
    inv_freq = jnp.power(ROPE_BASE, -jnp.arange(quarter, dtype=F32) / quarter)
    ang = jnp.concatenate([row[:, None] * inv_freq, col[:, None] * inv_freq], axis=-1)
    return jnp.cos(ang), jnp.sin(ang)


def kernel(x_prompt, x_sample, cache_attn_k, cache_attn_v, state_hgrn_fwd, state_hgrn_bwd,
           state_ret_fwd, state_ret_bwd, c, c_ctx, w_mod, b_mod, norm_mix_w, norm_ffn_w,
           w_in_even, w_out_even, attn_q_norm_w, attn_k_norm_w, hgrn_lb, hgrn_o_norm_w,
           w_in_odd, w_out_odd, ret_decay_fwd, ret_decay_bwd, ret_o_norm_w,
           w_up, conv_w, conv_b, w_down):
    depth, d_model = norm_mix_w.shape
    bp, np_, _ = x_prompt.shape
    bs, ns, _ = x_sample.shape

    lb_all = jnp.cumsum(jax.nn.softmax(hgrn_lb.astype(F32), axis=0), axis=0)
    lg_f = jnp.broadcast_to(jax.nn.log_sigmoid(ret_decay_fwd.astype(F32))[:, :, None, None],
                            ret_decay_fwd.shape + (1, C_DV))
    lg_b = jnp.broadcast_to(jax.nn.log_sigmoid(ret_decay_bwd.astype(F32))[:, :, None, None],
                            ret_decay_bwd.shape + (1, C_DV))
    cos_a, sin_a = _rope_tables(ns, A_HEAD_DIM)
    rope_a = (jnp.concatenate([cos_a, cos_a], axis=1), jnp.concatenate([-sin_a, sin_a], axis=1))
    rope_c = _rope_tables(ns, C_DK)
    nmw = norm_mix_w.reshape(depth, 1, d_model)
    nfw = norm_ffn_w.reshape(depth, 1, d_model)
    qnw = attn_q_norm_w.reshape(-1, 1, A_HEAD_DIM)
    knw = attn_k_norm_w.reshape(-1, 1, A_HEAD_DIM)
    hnw = hgrn_o_norm_w.reshape(-1, 1, B_DV)
    rnw = ret_o_norm_w.reshape(-1, 1, C_DV)
    w_in_even, w_out_even, w_in_odd, w_out_odd, w_up, w_down = (
        w.astype(BF16) for w in (w_in_even, w_out_even, w_in_odd, w_out_odd, w_up, w_down))

    n_cond = 1 + bs
    pad = (-n_cond) % 8
    cond = jnp.concatenate([c_ctx[None, :], c, jnp.zeros((pad, d_model), F32)], axis=0)
    mod_all = _modulation(cond, w_mod, b_mod).reshape(depth, n_cond + pad, 6, d_model)

    groups = (
        dict(x=x_prompt.reshape(bp * np_, d_model), n_seq=bp, n=np_, row0=0,
             rows_per_cond=bp * np_, latent=False),
        dict(x=x_sample.reshape(bs * ns, d_model), n_seq=bs, n=ns, row0=1,
             rows_per_cond=ns, latent=True),
    )
    results = []
    for grp in groups:
        y, n_seq, n = grp["x"], grp["n_seq"], grp["n"]
        row0, rpc, latent = grp["row0"], grp["rows_per_cond"], grp["latent"]
        new = dict(k=[], v=[], hf=[], hb=[], rf=[], rb=[])
        for l in range(depth):
            mod = mod_all[l]
            if l % 2 == 0:
                e = l // 2
                proj = _norm_proj(y, mod, row0, rpc, nmw, l, w_in_even, e, row=0, out_dtype=BF16)
                att = _attention(proj, n_seq, n, qnw, knw, e, rope_a if latent else None,
                                 cache_attn_k, cache_attn_v, emit_kv=not latent)
                lb = lb_all[e].reshape(B_HEADS, 1, B_DK)
                hg = _hgrn(proj, n_seq, n, lb, hnw, e,
                           state_hgrn_fwd if latent else None,
                           state_hgrn_bwd if latent else None, emit_state=not latent)
                if not latent:
                    new["k"].append(att[1].reshape(n_seq, n, A_KV_HEADS, A_HEAD_DIM))
                    new["v"].append(att[2].reshape(n_seq, n, A_KV_HEADS, A_HEAD_DIM))
                    new["hf"].append(hg[1])
                    new["hb"].append(hg[2])
                y = _proj_res([att[0], hg[0]], w_out_even, e, y, mod, row0, rpc, row=2, tn=512)
            else:
                o = l // 2
                proj = _norm_proj(y, mod, row0, rpc, nmw, l, w_in_odd, o, row=0, out_dtype=BF16)
                rt = _retention(proj, n_seq, n, lg_f[o], lg_b[o], rnw, o,
                                rope_c if latent else None,
                                state_ret_fwd if latent else None,
                                state_ret_bwd if latent else None, emit_state=not latent)
                if not latent:
                    new["rf"].append(rt[1])
                    new["rb"].append(rt[2])
                y = _proj_res([rt[0]], w_out_odd, o, y, mod, row0, rpc, row=2, tn=512)
            act = _ffn_up(y, mod, row0, rpc, nfw, l, w_up, conv_w, conv_b, n)
            y = _proj_res([act], w_down, l, y, mod, row0, rpc, row=5, tn=512)
        results.append((y.reshape(n_seq, n, d_model), new))

    (y_p, new), (y_s, _) = results
    stack = lambda xs: jnp.stack(xs, axis=1)
    return (y_p, y_s, stack(new["k"]), stack(new["v"]), stack(new["hf"]), stack(new["hb"]),
            stack(new["rf"]), stack(new["rb"]))
```

```python
import functools

import jax
import jax.numpy as jnp
import numpy as np
from jax import lax
from jax.experimental import pallas as pl
from jax.experimental.pallas import tpu as pltpu

F32 = jnp.float32
BF16 = jnp.bfloat16

GRID_W = 64
A_HEADS = 8
A_KV_HEADS = 2
A_HEAD_DIM = 128
ROPE_BASE = 10000.0
B_HEADS = 8
B_DK = 128
B_DV = 128
C_HEADS = 8
C_DK = 256
C_DV = 512
EPS = 1e-6

A_Q = A_HEADS * A_HEAD_DIM
A_KV = A_KV_HEADS * A_HEAD_DIM
A_GROUP = A_HEADS // A_KV_HEADS
B_QK = B_HEADS * B_DK
B_V = B_HEADS * B_DV
C_QK = C_HEADS * C_DK
C_V = C_HEADS * C_DV

VMEM_LIMIT_BYTES = 56 * 1024 * 1024
ROW_TILE = 1024
COL_TILE = 512
MIX_BLOCK = 128
RET_BLOCK = 256
HGRN_BASE = 16
HGRN_HEADS_PER_STEP = 4
EXP_CLAMP = 80.0


def _cparams(*sem):
    return pltpu.CompilerParams(dimension_semantics=sem, vmem_limit_bytes=VMEM_LIMIT_BYTES)


def _dot(a, b):
    return jnp.dot(a, b, preferred_element_type=F32)


def _dot_nt(a, b):
    return lax.dot_general(a, b, (((1,), (1,)), ((), ())), preferred_element_type=F32)


def _dot_tn(a, b):
    return lax.dot_general(a, b, (((0,), (0,)), ((), ())), preferred_element_type=F32)


def _sigmoid(x):
    return 1.0 / (1.0 + jnp.exp(-x))


def _rms(x, w):
    return x * lax.rsqrt(jnp.mean(x * x, axis=-1, keepdims=True) + EPS) * w


def _mod_kernel(c_ref, w_ref, b_ref, o_ref):
    c = c_ref[...]
    s = (c * _sigmoid(c)).astype(BF16)
    o_ref[...] = _dot(s, w_ref[...].astype(BF16)) + b_ref[...]


def _modulation(cond, w_mod, b_mod):
    n_layers, d, n_out = w_mod.shape
    r = cond.shape[0]
    tn = 1024
    return pl.pallas_call(
        _mod_kernel,
        grid=(n_layers, n_out // tn),
        in_specs=[
            pl.BlockSpec((r, d), lambda l, j: (0, 0)),
            pl.BlockSpec((None, d, tn), lambda l, j: (l, 0, j)),
            pl.BlockSpec((None, 1, tn), lambda l, j: (l, 0, j)),
        ],
        out_specs=pl.BlockSpec((None, r, tn), lambda l, j: (l, 0, j)),
        out_shape=jax.ShapeDtypeStruct((n_layers, r, n_out), F32),
        compiler_params=_cparams("arbitrary", "arbitrary"),
        name="modulation",
    )(cond, w_mod, b_mod.reshape(n_layers, 1, n_out))


def _norm_mod(x_ref, mod_ref, nw_ref, h_scr, row):
    h = _rms(x_ref[...], nw_ref[...])
    h = h * (1.0 + mod_ref[row + 1:row + 2, :]) + mod_ref[row:row + 1, :]
    h_scr[...] = h.astype(BF16)


def _norm_proj_kernel(x_ref, mod_ref, nw_ref, w_ref, o_ref, h_scr, *, row):
    @pl.when(pl.program_id(1) == 0)
    def _():
        _norm_mod(x_ref, mod_ref, nw_ref, h_scr, row)

    o_ref[...] = _dot(h_scr[...], w_ref[...]).astype(o_ref.dtype)


def _norm_proj(x, mod, mod_row0, rows_per_cond, norm_w, layer, w, w_idx, *, row, out_dtype):
    m, d = x.shape
    n_out = w.shape[-1]
    tm, tn = ROW_TILE, COL_TILE
    cond_of = lambda i: mod_row0 + (i * tm) // rows_per_cond
    return pl.pallas_call(
        functools.partial(_norm_proj_kernel, row=row),
        grid=(m // tm, n_out // tn),
        in_specs=[
            pl.BlockSpec((tm, d), lambda i, j: (i, 0)),
            pl.BlockSpec((None, 6, d), lambda i, j: (cond_of(i), 0, 0)),
            pl.BlockSpec((None, 1, d), lambda i, j: (layer, 0, 0)),
            pl.BlockSpec((None, d, tn), lambda i, j: (w_idx, 0, j)),
        ],
        out_specs=pl.BlockSpec((tm, tn), lambda i, j: (i, j)),
        out_shape=jax.ShapeDtypeStruct((m, n_out), out_dtype),
        scratch_shapes=[pltpu.VMEM((tm, d), BF16)],
        compiler_params=_cparams("parallel", "arbitrary"),
        name="norm_proj",
    )(x, mod, norm_w, w)


def _conv3(u, cw, cb, first, last):
    t = u.shape[0]
    left = jnp.where(first, 0.0, pltpu.roll(u, 1, axis=0))
    right = jnp.where(last, 0.0, pltpu.roll(u, t - 1, axis=0))
    return left * cw[0:1, :] + u * cw[1:2, :] + right * cw[2:3, :] + cb


def _ffn_up_kernel(x_ref, mod_ref, nw_ref, wa_ref, wv_ref, cwa_ref, cwv_ref, cba_ref, cbv_ref,
                   o_ref, h_scr, *, seq_len):
    @pl.when(pl.program_id(1) == 0)
    def _():
        _norm_mod(x_ref, mod_ref, nw_ref, h_scr, 3)

    h = h_scr[...]
    tm = h.shape[0]
    pos = lax.broadcasted_iota(jnp.int32, (tm, 1), 0) % seq_len
    first = pos == 0
    last = pos == seq_len - 1
    a = _conv3(_dot(h, wa_ref[...]), cwa_ref[...], cba_ref[...], first, last)
    v = _conv3(_dot(h, wv_ref[...]), cwv_ref[...], cbv_ref[...], first, last)
    o_ref[...] = (a * _sigmoid(a) * v).astype(o_ref.dtype)


def _ffn_up(x, mod, mod_row0, rows_per_cond, norm_w, layer, w_up, conv_w, conv_b, seq_len):
    m, d = x.shape
    d_ff = w_up.shape[-1] // 2
    tm, tn = ROW_TILE, COL_TILE
    nj = d_ff // tn
    cond_of = lambda i: mod_row0 + (i * tm) // rows_per_cond
    conv_b3 = conv_b.reshape(conv_b.shape[0], 1, 2 * d_ff)
    return pl.pallas_call(
        functools.partial(_ffn_up_kernel, seq_len=seq_len),
        grid=(m // tm, nj),
        in_specs=[
            pl.BlockSpec((tm, d), lambda i, j: (i, 0)),
            pl.BlockSpec((None, 6, d), lambda i, j: (cond_of(i), 0, 0)),
            pl.BlockSpec((None, 1, d), lambda i, j: (layer, 0, 0)),
            pl.BlockSpec((None, d, tn), lambda i, j: (layer, 0, j)),
            pl.BlockSpec((None, d, tn), lambda i, j: (layer, 0, nj + j)),
            pl.BlockSpec((None, 3, tn), lambda i, j: (layer, 0, j)),
            pl.BlockSpec((None, 3, tn), lambda i, j: (layer, 0, nj + j)),
            pl.BlockSpec((None, 1, tn), lambda i, j: (layer, 0, j)),
            pl.BlockSpec((None, 1, tn), lambda i, j: (layer, 0, nj + j)),
        ],
        out_specs=pl.BlockSpec((tm, tn), lambda i, j: (i, j)),
        out_shape=jax.ShapeDtypeStruct((m, d_ff), BF16),
        scratch_shapes=[pltpu.VMEM((tm, d), BF16)],
        compiler_params=_cparams("parallel", "arbitrary"),
        name="ffn_up",
    )(x, mod, norm_w, w_up, w_up, conv_w, conv_w, conv_b3, conv_b3)


def _proj_res_kernel(*refs, n_in, row):
    a_refs = refs[:n_in]
    w_refs = refs[n_in:2 * n_in]
    y_ref, mod_ref, o_ref = refs[2 * n_in:]
    acc = _dot(a_refs[0][...], w_refs[0][...])
    for a_ref, w_ref in zip(a_refs[1:], w_refs[1:]):
        acc += _dot(a_ref[...], w_ref[...])
    o_ref[...] = y_ref[...] + mod_ref[row:row + 1, :] * acc


def _proj_res(acts, w, w_idx, y, mod, mod_row0, rows_per_cond, *, row):
    m, d = y.shape
    tm, tn = ROW_TILE, COL_TILE
    n_in = len(acts)
    cond_of = lambda i: mod_row0 + (i * tm) // rows_per_cond
    in_specs = [pl.BlockSpec((tm, a.shape[1]), lambda i, j: (i, 0)) for a in acts]
    k_blk = acts[0].shape[1]
    assert all(a.shape[1] == k_blk for a in acts)
    for k in range(n_in):
        in_specs.append(pl.BlockSpec((None, k_blk, tn), lambda i, j, k=k: (w_idx, k, j)))
    in_specs += [
        pl.BlockSpec((tm, tn), lambda i, j: (i, j)),
        pl.BlockSpec((None, 6, tn), lambda i, j: (cond_of(i), 0, j)),
    ]
    return pl.pallas_call(
        functools.partial(_proj_res_kernel, n_in=n_in, row=row),
        grid=(m // tm, d // tn),
        in_specs=in_specs,
        out_specs=pl.BlockSpec((tm, tn), lambda i, j: (i, j)),
        out_shape=jax.ShapeDtypeStruct((m, d), F32),
        compiler_params=_cparams("parallel", "arbitrary"),
        name="proj_res",
    )(*acts, *([w] * n_in), y, mod)


def _rope_half_roll(x, cos2, sin2):
    return x * cos2 + pltpu.roll(x, x.shape[1] // 2, axis=1) * sin2


def _attn_kernel(*refs, n, n_ctx, rope, emit_kv, tq):
    q_ref, k_ref, v_ref, qw_ref, kw_ref = refs[:5]
    pos = 5
    if rope:
        cos_ref, sin_ref, ck_ref, cv_ref = refs[pos:pos + 4]
        pos += 4
    o_ref = refs[pos]
    pos += 1
    if emit_kv:
        nk_ref, nv_ref = refs[pos:pos + 2]
        pos += 2
    kall, vall = refs[pos:pos + 2]

    kn = _rms(k_ref[...].astype(F32), kw_ref[...])
    vv = v_ref[...]
    if emit_kv:
        nk_ref[...] = kn
        nv_ref[...] = vv.astype(F32)
    if rope:
        kn = _rope_half_roll(kn, cos_ref[...], sin_ref[...])
        kall[0:n_ctx, :] = ck_ref[...].astype(BF16)
        vall[0:n_ctx, :] = cv_ref[...].astype(BF16)
    kall[n_ctx:n_ctx + n, :] = kn.astype(BF16)
    vall[n_ctx:n_ctx + n, :] = vv.astype(BF16)

    scale = A_HEAD_DIM ** -0.5

    def chunk(c, carry):
        r0 = pl.multiple_of(c * tq, tq)
        rows = pl.ds(r0, tq)
        heads = [slice(g * A_HEAD_DIM, (g + 1) * A_HEAD_DIM) for g in range(A_GROUP)]
        qs = []
        for cols in heads:
            qh = _rms(q_ref[rows, cols].astype(F32), qw_ref[...])
            if rope:
                qh = _rope_half_roll(qh, cos_ref[rows, :], sin_ref[rows, :])
            qs.append((qh * scale).astype(BF16))
        ss = [_dot_nt(qh, kall[...]) for qh in qs]
        ps = [jnp.exp(s - jnp.max(s, axis=-1, keepdims=True)) for s in ss]
        ls = [jnp.sum(p, axis=-1, keepdims=True) for p in ps]
        os = [_dot(p.astype(BF16), vall[...]) / l for p, l in zip(ps, ls)]
        for cols, o in zip(heads, os):
            o_ref[rows, cols] = o.astype(o_ref.dtype)
        return carry

    lax.fori_loop(0, n // tq, chunk, 0)


def _attention(proj, n_seq, n, q_norm_w, k_norm_w, e, rope_tabs, cache_k, cache_v, emit_kv):
    rope = rope_tabs is not None
    n_ctx = cache_k.shape[2] if rope else 0
    hd = A_HEAD_DIM
    qcols = A_GROUP * hd
    in_specs = [
        pl.BlockSpec((n, qcols), lambda b, kv: (b, kv)),
        pl.BlockSpec((n, hd), lambda b, kv: (b, A_Q // hd + kv)),
        pl.BlockSpec((n, hd), lambda b, kv: (b, (A_Q + A_KV) // hd + kv)),
        pl.BlockSpec((None, 1, hd), lambda b, kv: (e, 0, 0)),
        pl.BlockSpec((None, 1, hd), lambda b, kv: (e, 0, 0)),
    ]
    args = [proj, proj, proj, q_norm_w, k_norm_w]
    if rope:
        cos2, sin2 = rope_tabs
        in_specs += [
            pl.BlockSpec((n, hd), lambda b, kv: (0, 0)),
            pl.BlockSpec((n, hd), lambda b, kv: (0, 0)),
            pl.BlockSpec((None, None, n_ctx, hd), lambda b, kv: (b, e, 0, kv)),
            pl.BlockSpec((None, None, n_ctx, hd), lambda b, kv: (b, e, 0, kv)),
        ]
        ck = cache_k.reshape(cache_k.shape[0], cache_k.shape[1], n_ctx, A_KV)
        cv = cache_v.reshape(cache_v.shape[0], cache_v.shape[1], n_ctx, A_KV)
        args += [cos2, sin2, ck, cv]
    out_specs = [pl.BlockSpec((n, qcols), lambda b, kv: (b, kv))]
    out_shape = [jax.ShapeDtypeStruct((n_seq * n, A_Q), BF16)]
    if emit_kv:
        out_specs += [pl.BlockSpec((n, hd), lambda b, kv: (b, kv))] * 2
        out_shape += [jax.ShapeDtypeStruct((n_seq * n, A_KV), F32)] * 2
    return pl.pallas_call(
        functools.partial(_attn_kernel, n=n, n_ctx=n_ctx, rope=rope, emit_kv=emit_kv,
                          tq=min(n, 256)),
        grid=(n_seq, A_KV_HEADS),
        in_specs=in_specs,
        out_specs=out_specs,
        out_shape=out_shape,
        scratch_shapes=[pltpu.VMEM((n_ctx + n, hd), BF16), pltpu.VMEM((n_ctx + n, hd), BF16)],
        compiler_params=_cparams("parallel", "arbitrary"),
        name="attention",
    )(*args)


def _gla_levels(c, rev):
    row = lax.broadcasted_iota(jnp.int32, (c, c), 0)
    col = lax.broadcasted_iota(jnp.int32, (c, c), 1)
    shift = HGRN_BASE.bit_length() - 1
    x = (row >> shift) ^ (col >> shift)
    lvl = jnp.zeros((c, c), jnp.int32)
    for l in range(1, (c // HGRN_BASE).bit_length()):
        lvl = jnp.where(x >= (1 << (l - 1)), l, lvl)
    causal = (col >= row) if rev else (col <= row)
    return jnp.where(causal, lvl, -1)


def _gla_blocks(chains):
    c, dk = chains[0]["q"].shape
    each = lambda fn: [fn(ch) for ch in chains]

    def split3(ch):
        hi = ch["lf"].astype(BF16)
        r1 = ch["lf"] - hi.astype(F32)
        mid = r1.astype(BF16)
        ch["parts"] = (hi, mid, (r1 - mid.astype(F32)).astype(BF16))

    def cumulate(ch):
        hi, mid, lo = ch["parts"]
        ch["b"] = _dot(ch["tri"], hi) + _dot(ch["tri"], mid) + _dot(ch["tri"], lo)
        ch["tot"] = ch["b"][0:1, :] if ch["rev"] else ch["b"][c - 1:c, :]

    def inter(ch):
        ch["o"] = _dot_nt((ch["q"] * jnp.exp(ch["b"])).astype(BF16), ch["st"].astype(BF16))
        ch["khat"] = (ch["k"] * jnp.exp(ch["tot"] - ch["b"])).astype(BF16)
        ch["vb"] = ch["v"].astype(BF16)

    def level0(ch):
        b3 = ch["b"].reshape(c // HGRN_BASE, HGRN_BASE, dk)
        mid_row = HGRN_BASE // 2 if ch["rev"] else HGRN_BASE // 2 - 1
        a = jnp.clip(b3 - b3[:, mid_row:mid_row + 1, :], -EXP_CLAMP, EXP_CLAMP).reshape(c, dk)
        p = _dot_nt((ch["q"] * jnp.exp(a)).astype(BF16), (ch["k"] * jnp.exp(-a)).astype(BF16))
        ch["scores"] = jnp.where(ch["lvl"] == 0, p, 0.0)

    def upper_level(ch, h, level):
        b3 = ch["b"].reshape(c // (2 * h), 2 * h, dk)
        ref_row = h if ch["rev"] else h - 1
        e = jnp.exp(-jnp.abs(b3 - b3[:, ref_row:ref_row + 1, :])).reshape(c, dk)
        p = _dot_nt((ch["q"] * e).astype(BF16), (ch["k"] * e).astype(BF16))
        ch["scores"] = jnp.where(ch["lvl"] == level, p, ch["scores"])

    def combine(ch):
        o = ch["o"] + _dot(ch["scores"].astype(BF16), ch["vb"])
        st_new = ch["st"] * jnp.exp(ch["tot"]) + _dot_tn(ch["vb"], ch["khat"])
        return o, st_new

    each(split3)
    each(cumulate)
    each(inter)
    each(level0)
    h, level = HGRN_BASE, 1
    while h < c:
        each(functools.partial(upper_level, h=h, level=level))
        h, level = 2 * h, level + 1
    return each(combine)


def _hgrn_kernel(*refs, n, has_state, emit_state):
    q_ref, i_ref, zf_ref, zb_ref, g_ref, lb_ref, nw_ref = refs[:7]
    pos = 7
    if has_state:
        s0f_ref, s0b_ref = refs[pos:pos + 2]
        pos += 2
    o_ref = refs[pos]
    pos += 1
    if emit_state:
        sf_ref, sb_ref = refs[pos:pos + 2]
        pos += 2
    of_scr, ob_scr, st_scr = refs[pos:pos + 3]

    c = MIX_BLOCK
    nb = n // c
    qscale = B_DK ** -0.5
    lvl_f = _gla_levels(c, False)
    lvl_b = _gla_levels(c, True)
    tri_f = jnp.where(lvl_f >= 0, 1.0, 0.0).astype(BF16)
    tri_b = jnp.where(lvl_b >= 0, 1.0, 0.0).astype(BF16)
    heads = [slice(hh * B_DK, (hh + 1) * B_DK) for hh in range(HGRN_HEADS_PER_STEP)]

    def chain(z_ref, rows, hh, rev):
        cols = heads[hh]
        lb = lb_ref[hh]
        f = lb + (1.0 - lb) * _sigmoid(z_ref[rows, cols].astype(F32))
        return dict(q=q_ref[rows, cols].astype(F32) * qscale, k=1.0 - f,
                    v=i_ref[rows, cols].astype(F32), lf=jnp.log(f),
                    st=st_scr[2 * hh + int(rev)], rev=rev,
                    lvl=lvl_b if rev else lvl_f, tri=tri_b if rev else tri_f)

    def step(j, carry):
        rows_f = pl.ds(pl.multiple_of(j * c, c), c)
        rows_b = pl.ds(pl.multiple_of((nb - 1 - j) * c, c), c)
        chains = []
        for hh in range(len(heads)):
            chains += [chain(zf_ref, rows_f, hh, False), chain(zb_ref, rows_b, hh, True)]
        outs = _gla_blocks(chains)
        for hh, cols in enumerate(heads):
            (o_f, st_f), (o_b, st_b) = outs[2 * hh], outs[2 * hh + 1]
            of_scr[rows_f, cols] = o_f
            ob_scr[rows_b, cols] = o_b
            st_scr[2 * hh] = st_f
            st_scr[2 * hh + 1] = st_b
        return carry

    def finish(j, carry):
        rows = pl.ds(pl.multiple_of(j * c, c), c)
        for cols in heads:
            y = _rms(of_scr[rows, cols] + ob_scr[rows, cols], nw_ref[...])
            g = g_ref[rows, cols].astype(F32)
            o_ref[rows, cols] = (y * (g * _sigmoid(g))).astype(o_ref.dtype)
        return carry

    for hh in range(len(heads)):
        if has_state:
            st_scr[2 * hh] = s0f_ref[hh].T
            st_scr[2 * hh + 1] = s0b_ref[hh].T
        else:
            st_scr[2 * hh] = jnp.zeros((B_DV, B_DK), F32)
            st_scr[2 * hh + 1] = jnp.zeros((B_DV, B_DK), F32)
    lax.fori_loop(0, nb, step, 0)
    lax.fori_loop(0, nb, finish, 0)
    if emit_state:
        for hh in range(len(heads)):
            sf_ref[hh] = st_scr[2 * hh].T
            sb_ref[hh] = st_scr[2 * hh + 1].T


def _hgrn(proj, n_seq, n, lb, o_norm_w, e, state_f, state_b, emit_state):
    has_state = state_f is not None
    hp = HGRN_HEADS_PER_STEP
    d = B_DK * hp
    base = (A_Q + 2 * A_KV) // d
    col = lambda k: (lambda b, h: (b, base + k * (B_HEADS // hp) + h))
    in_specs = [pl.BlockSpec((n, d), col(k)) for k in range(5)]
    in_specs += [
        pl.BlockSpec((hp, 1, B_DK), lambda b, h: (h, 0, 0)),
        pl.BlockSpec((None, 1, B_DV), lambda b, h: (e, 0, 0)),
    ]
    args = [proj] * 5 + [lb, o_norm_w]
    if has_state:
        st_spec = pl.BlockSpec((None, None, hp, B_DK, B_DV), lambda b, h: (b, e, h, 0, 0))
        in_specs += [st_spec, st_spec]
        args += [state_f, state_b]
    out_specs = [pl.BlockSpec((n, d), lambda b, h: (b, h))]
    out_shape = [jax.ShapeDtypeStruct((n_seq * n, B_V), BF16)]
    if emit_state:
        so = pl.BlockSpec((None, hp, B_DK, B_DV), lambda b, h: (b, h, 0, 0))
        out_specs += [so, so]
        out_shape += [jax.ShapeDtypeStruct((n_seq, B_HEADS, B_DK, B_DV), F32)] * 2
    return pl.pallas_call(
        functools.partial(_hgrn_kernel, n=n, has_state=has_state, emit_state=emit_state),
        grid=(n_seq, B_HEADS // hp),
        in_specs=in_specs,
        out_specs=out_specs,
        out_shape=out_shape,
        scratch_shapes=[pltpu.VMEM((n, d), F32), pltpu.VMEM((n, d), F32),
                        pltpu.VMEM((2 * hp, B_DV, B_DK), F32)],
        compiler_params=_cparams("parallel", "arbitrary"),
        name="hgrn2",
    )(*args)


def _rope_split(x, cos, sin):
    half = x.shape[1] // 2
    x1, x2 = x[:, :half], x[:, half:]
    return jnp.concatenate([x1 * cos - x2 * sin, x1 * sin + x2 * cos], axis=1)


def _ret_kernel(*refs, n, rope, has_state, emit_state):
    q_ref, k_ref, v_ref, g_ref, lgf_ref, lgb_ref, nw_ref = refs[:7]
    pos = 7
    if rope:
        cos_ref, sin_ref = refs[pos:pos + 2]
        pos += 2
    if has_state:
        s0f_ref, s0b_ref = refs[pos:pos + 2]
        pos += 2
    o_ref = refs[pos]
    pos += 1
    if emit_state:
        sf_ref, sb_ref = refs[pos:pos + 2]
        pos += 2
    qs_scr, ks_scr, of_scr, ob_scr, stf_scr, stb_scr, dm_scr, qd_scr, kd_scr = refs[pos:pos + 9]

    c = min(RET_BLOCK, n)
    nb = n // c
    skip_inter = (not has_state) and nb == 1
    kscale = C_DK ** -0.5
    rowi = lax.broadcasted_iota(jnp.int32, (c, c), 0)
    coli = lax.broadcasted_iota(jnp.int32, (c, c), 1)
    rowq = lax.broadcasted_iota(jnp.int32, (c, C_DK), 0).astype(F32)

    def prep(j, carry):
        rows = pl.ds(pl.multiple_of(j * MIX_BLOCK, MIX_BLOCK), MIX_BLOCK)
        q = q_ref[rows, :].astype(F32)
        k = k_ref[rows, :].astype(F32)
        if rope:
            q = _rope_split(q, cos_ref[rows, :], sin_ref[rows, :])
            k = _rope_split(k, cos_ref[rows, :], sin_ref[rows, :])
        qs_scr[rows, :] = q
        ks_scr[rows, :] = k * kscale
        return carry

    @pl.when(pl.program_id(1) == 0)
    def _():
        for d, (lg_ref, rev) in enumerate(((lgf_ref, False), (lgb_ref, True))):
            lg = lg_ref[...]
            dist = (coli - rowi) if rev else (rowi - coli)
            dm_scr[d] = jnp.where(
                dist >= 0, jnp.exp(lg[:, :c] * jnp.maximum(dist, 0).astype(F32)), 0.0)
            lgq = lg[:, :C_DK]
            qd_scr[d] = jnp.exp(lgq * ((c - rowq) if rev else (rowq + 1.0)))
            kd_scr[d] = jnp.exp(lgq * (rowq if rev else (c - 1.0 - rowq)))

    def block(rows, d, lg_ref, st_scr):
        q = qs_scr[rows, :]
        k = ks_scr[rows, :]
        vb = v_ref[rows, :]
        s = _dot_nt(q.astype(BF16), k.astype(BF16)) * dm_scr[d]
        o = _dot(s.astype(BF16), vb)
        u = _dot_tn((k * kd_scr[d]).astype(BF16), vb)
        if skip_inter:
            st_scr[...] = u
        else:
            st = st_scr[...]
            o = o + _dot((q * qd_scr[d]).astype(BF16), st.astype(BF16))
            st_scr[...] = st * jnp.exp(lg_ref[...] * float(c)) + u
        return o

    def step(j, carry):
        rows_f = pl.ds(pl.multiple_of(j * c, c), c)
        rows_b = pl.ds(pl.multiple_of((nb - 1 - j) * c, c), c)
        of_scr[rows_f, :] = block(rows_f, 0, lgf_ref, stf_scr)
        ob_scr[rows_b, :] = block(rows_b, 1, lgb_ref, stb_scr)
        return carry

    def finish(j, carry):
        rows = pl.ds(pl.multiple_of(j * MIX_BLOCK, MIX_BLOCK), MIX_BLOCK)
        y = _rms(of_scr[rows, :] + ob_scr[rows, :], nw_ref[...])
        g = g_ref[rows, :].astype(F32)
        o_ref[rows, :] = (y * (g * _sigmoid(g))).astype(o_ref.dtype)
        return carry

    lax.fori_loop(0, n // MIX_BLOCK, prep, 0)
    if has_state:
        stf_scr[...] = s0f_ref[...]
        stb_scr[...] = s0b_ref[...]
    elif not skip_inter:
        stf_scr[...] = jnp.zeros((C_DK, C_DV), F32)
        stb_scr[...] = jnp.zeros((C_DK, C_DV), F32)
    lax.fori_loop(0, nb, step, 0)
    lax.fori_loop(0, n // MIX_BLOCK, finish, 0)
    if emit_state:
        sf_ref[...] = stf_scr[...]
        sb_ref[...] = stb_scr[...]


def _retention(proj, n_seq, n, lg_f, lg_b, o_norm_w, o_idx, rope_tabs, state_f, state_b,
               emit_state):
    rope = rope_tabs is not None
    has_state = state_f is not None
    nq = C_QK // C_DK
    c = min(RET_BLOCK, n)
    in_specs = [
        pl.BlockSpec((n, C_DK), lambda h, b: (b, h)),
        pl.BlockSpec((n, C_DK), lambda h, b: (b, nq + h)),
        pl.BlockSpec((n, C_DV), lambda h, b: (b, 2 * C_QK // C_DV + h)),
        pl.BlockSpec((n, C_DV), lambda h, b: (b, (2 * C_QK + C_V) // C_DV + h)),
        pl.BlockSpec((None, 1, C_DV), lambda h, b: (h, 0, 0)),
        pl.BlockSpec((None, 1, C_DV), lambda h, b: (h, 0, 0)),
        pl.BlockSpec((None, 1, C_DV), lambda h, b: (o_idx, 0, 0)),
    ]
    args = [proj] * 4 + [lg_f, lg_b, o_norm_w]
    if rope:
        in_specs += [pl.BlockSpec((n, C_DK // 2), lambda h, b: (0, 0))] * 2
        args += list(rope_tabs)
    if has_state:
        st_spec = pl.BlockSpec((None, None, None, C_DK, C_DV), lambda h, b: (b, o_idx, h, 0, 0))
        in_specs += [st_spec, st_spec]
        args += [state_f, state_b]
    out_specs = [pl.BlockSpec((n, C_DV), lambda h, b: (b, h))]
    out_shape = [jax.ShapeDtypeStruct((n_seq * n, C_V), BF16)]
    if emit_state:
        so = pl.BlockSpec((None, None, C_DK, C_DV), lambda h, b: (b, h, 0, 0))
        out_specs += [so, so]
        out_shape += [jax.ShapeDtypeStruct((n_seq, C_HEADS, C_DK, C_DV), F32)] * 2
    return pl.pallas_call(
        functools.partial(_ret_kernel, n=n, rope=rope, has_state=has_state,
                          emit_state=emit_state),
        grid=(C_HEADS, n_seq),
        in_specs=in_specs,
        out_specs=out_specs,
        out_shape=out_shape,
        scratch_shapes=[pltpu.VMEM((n, C_DK), F32), pltpu.VMEM((n, C_DK), F32),
                        pltpu.VMEM((n, C_DV), F32), pltpu.VMEM((n, C_DV), F32),
                        pltpu.VMEM((C_DK, C_DV), F32), pltpu.VMEM((C_DK, C_DV), F32),
                        pltpu.VMEM((2, c, c), F32), pltpu.VMEM((2, c, C_DK), F32),
                        pltpu.VMEM((2, c, C_DK), F32)],
        compiler_params=_cparams("arbitrary", "arbitrary"),
        name="retention",
    )(*args)


def _rope_tables(n_tokens, head_dim):
    rows = n_tokens // GRID_W
    row = jnp.repeat(jnp.arange(rows, dtype=F32), GRID_W)
    col = jnp.tile(jnp.arange(GRID_W, dtype=F32), rows)
    quarter = head_dim // 4
    inv_freq = jnp.power(ROPE_BASE, -jnp.arange(quarter, dtype=F32) / quarter)
    ang = jnp.concatenate([row[:, None] * inv_freq, col[:, None] * inv_freq], axis=-1)
    return jnp.cos(ang), jnp.sin(ang)


def kernel(x_prompt, x_sample, cache_attn_k, cache_attn_v, state_hgrn_fwd, state_hgrn_bwd,
           state_ret_fwd, state_ret_bwd, c, c_ctx, w_mod, b_mod, norm_mix_w, norm_ffn_w,
           w_in_even, w_out_even, attn_q_norm_w, attn_k_norm_w, hgrn_lb, hgrn_o_norm_w,
           w_in_odd, w_out_odd, ret_decay_fwd, ret_decay_bwd, ret_o_norm_w,
           w_up, conv_w, conv_b, w_down):
    depth, d_model = norm_mix_w.shape
    bp, np_, _ = x_prompt.shape
    bs, ns, _ = x_sample.shape

    lb_all = jnp.cumsum(jax.nn.softmax(hgrn_lb.astype(F32), axis=0), axis=0)
    lg_f = jnp.broadcast_to(jax.nn.log_sigmoid(ret_decay_fwd.astype(F32))[:, :, None, None],
                            ret_decay_fwd.shape + (1, C_DV))
    lg_b = jnp.broadcast_to(jax.nn.log_sigmoid(ret_decay_bwd.astype(F32))[:, :, None, None],
                            ret_decay_bwd.shape + (1, C_DV))
    cos_a, sin_a = _rope_tables(ns, A_HEAD_DIM)
    rope_a = (jnp.concatenate([cos_a, cos_a], axis=1), jnp.concatenate([-sin_a, sin_a], axis=1))
    rope_c = _rope_tables(ns, C_DK)
    nmw = norm_mix_w.reshape(depth, 1, d_model)
    nfw = norm_ffn_w.reshape(depth, 1, d_model)
    qnw = attn_q_norm_w.reshape(-1, 1, A_HEAD_DIM)
    knw = attn_k_norm_w.reshape(-1, 1, A_HEAD_DIM)
    hnw = hgrn_o_norm_w.reshape(-1, 1, B_DV)
    rnw = ret_o_norm_w.reshape(-1, 1, C_DV)
    w_in_even, w_out_even, w_in_odd, w_out_odd, w_up, w_down = (
        w.astype(BF16) for w in (w_in_even, w_out_even, w_in_odd, w_out_odd, w_up, w_down))

    n_cond = 1 + bs
    pad = (-n_cond) % 8
    cond = jnp.concatenate([c_ctx[None, :], c, jnp.zeros((pad, d_model), F32)], axis=0)
    mod_all = _modulation(cond, w_mod, b_mod).reshape(depth, n_cond + pad, 6, d_model)

    groups = (
        dict(x=x_prompt.reshape(bp * np_, d_model), n_seq=bp, n=np_, row0=0,
             rows_per_cond=bp * np_, latent=False),
        dict(x=x_sample.reshape(bs * ns, d_model), n_seq=bs, n=ns, row0=1,
             rows_per_cond=ns, latent=True),
    )
    results = []
    for grp in groups:
        y, n_seq, n = grp["x"], grp["n_seq"], grp["n"]
        row0, rpc, latent = grp["row0"], grp["rows_per_cond"], grp["latent"]
        new = dict(k=[], v=[], hf=[], hb=[], rf=[], rb=[])
        for l in range(depth):
            mod = mod_all[l]
            if l % 2 == 0:
                e = l // 2
                proj = _norm_proj(y, mod, row0, rpc, nmw, l, w_in_even, e, row=0, out_dtype=BF16)
                att = _attention(proj, n_seq, n, qnw, knw, e, rope_a if latent else None,
                                 cache_attn_k, cache_attn_v, emit_kv=not latent)
                lb = lb_all[e].reshape(B_HEADS, 1, B_DK)
                hg = _hgrn(proj, n_seq, n, lb, hnw, e,
                           state_hgrn_fwd if latent else None,
                           state_hgrn_bwd if latent else None, emit_state=not latent)
                if not latent:
                    new["k"].append(att[1].reshape(n_seq, n, A_KV_HEADS, A_HEAD_DIM))
                    new["v"].append(att[2].reshape(n_seq, n, A_KV_HEADS, A_HEAD_DIM))
                    new["hf"].append(hg[1])
                    new["hb"].append(hg[2])
                y = _proj_res([att[0], hg[0]], w_out_even, e, y, mod, row0, rpc, row=2)
            else:
                o = l // 2
                proj = _norm_proj(y, mod, row0, rpc, nmw, l, w_in_odd, o, row=0, out_dtype=BF16)
                rt = _retention(proj, n_seq, n, lg_f[o], lg_b[o], rnw, o,
                                rope_c if latent else None,
                                state_ret_fwd if latent else None,
                                state_ret_bwd if latent else None, emit_state=not latent)
                if not latent:
                    new["rf"].append(rt[1])
                    new["rb"].append(rt[2])
                y = _proj_res([rt[0]], w_out_odd, o, y, mod, row0, rpc, row=2)
            act = _ffn_up(y, mod, row0, rpc, nfw, l, w_up, conv_w, conv_b, n)
            y = _proj_res([act], w_down, l, y, mod, row0, rpc, row=5)
        results.append((y.reshape(n_seq, n, d_model), new))

    (y_p, new), (y_s, _) = results
    stack = lambda xs: jnp.stack(xs, axis=1)
    return (y_p, y_s, stack(new["k"]), stack(new["v"]), stack(new["hf"]), stack(new["hb"]),
            stack(new["rf"]), stack(new["rb"]))
```

```python
import functools
from typing import NamedTuple

import jax
import jax.numpy as jnp
from jax import lax
from jax.experimental import pallas as pl
from jax.experimental.pallas import tpu as pltpu

F32 = jnp.float32
BF16 = jnp.bfloat16

GRID_W = 64
A_HEADS = 8
A_KV_HEADS = 2
A_HEAD_DIM = 128
ROPE_BASE = 10000.0
B_HEADS = 8
B_DK = 128
B_DV = 128
C_HEADS = 8
C_DK = 256
C_DV = 512
EPS = 1e-6

A_Q = A_HEADS * A_HEAD_DIM
A_KV = A_KV_HEADS * A_HEAD_DIM
A_GROUP = A_HEADS // A_KV_HEADS
B_QK = B_HEADS * B_DK
B_V = B_HEADS * B_DV
C_QK = C_HEADS * C_DK
C_V = C_HEADS * C_DV

VMEM_LIMIT_BYTES = 56 * 1024 * 1024
ROW_TILE = 1024
COL_TILE = 512
MIX_BLOCK = 128
RET_BLOCK = 256
NORM_ROWS = 16
HGRN_BASE = 16
HGRN_HEADS_PER_STEP = 4
EXP_CLAMP = 80.0


def _cparams(*sem):
    return pltpu.CompilerParams(dimension_semantics=sem, vmem_limit_bytes=VMEM_LIMIT_BYTES)


def _dot(a, b):
    return jnp.dot(a, b, preferred_element_type=F32)


def _dot_nt(a, b):
    return lax.dot_general(a, b, (((1,), (1,)), ((), ())), preferred_element_type=F32)


def _dot_tn(a, b):
    return lax.dot_general(a, b, (((0,), (0,)), ((), ())), preferred_element_type=F32)


def _sigmoid(x):
    return 1.0 / (1.0 + jnp.exp(-x))


def _rms(x, w):
    return x * lax.rsqrt(jnp.mean(x * x, axis=-1, keepdims=True) + EPS) * w


class _Weight(NamedTuple):
    array: jax.Array
    lead: int = 0
    row_blk: int = 0
    col0: int = 0

    @property
    def needs_cast(self):
        return self.array.dtype != BF16

    def spec(self, k_rows, tn):
        first = self.col0 // tn
        assert first * tn == self.col0
        return pl.BlockSpec((None, k_rows, tn), lambda i, j: (self.lead, self.row_blk, first + j))


def _col_tile(cast, wide=False):
    if cast:
        return COL_TILE // 2
    return 2 * COL_TILE if wide else COL_TILE


def _cast_copy_spec(k_rows, tn, nj):
    return pl.BlockSpec((k_rows, tn), lambda i, j: (0, jnp.where(i == 0, j, nj)))


def _cast_copy_shape(k_rows, tn, nj):
    return jax.ShapeDtypeStruct((k_rows, (nj + 1) * tn), BF16)


def _operand(w_ref, copy_ref):
    if copy_ref is None:
        return w_ref[...]
    wb = w_ref[...].astype(BF16)
    copy_ref[...] = wb
    return wb


def _mod_kernel(c_ref, w_ref, b_ref, o_ref):
    c = c_ref[...]
    s = (c * _sigmoid(c)).astype(BF16)
    o_ref[...] = _dot(s, w_ref[...].astype(BF16)) + b_ref[...]


def _modulation(cond, w_mod, b_mod):
    n_layers, d, n_out = w_mod.shape
    r = cond.shape[0]
    tn = 1024
    return pl.pallas_call(
        _mod_kernel,
        grid=(n_layers, n_out // tn),
        in_specs=[
            pl.BlockSpec((r, d), lambda l, j: (0, 0)),
            pl.BlockSpec((None, d, tn), lambda l, j: (l, 0, j)),
            pl.BlockSpec((None, 1, tn), lambda l, j: (l, 0, j)),
        ],
        out_specs=pl.BlockSpec((None, r, tn), lambda l, j: (l, 0, j)),
        out_shape=jax.ShapeDtypeStruct((n_layers, r, n_out), F32),
        compiler_params=_cparams("arbitrary", "arbitrary"),
        name="modulation",
    )(cond, w_mod, b_mod.reshape(n_layers, 1, n_out))


def _norm_mod(x_ref, mod_ref, nw_ref, h_scr, row):
    gain = nw_ref[...] * (1.0 + mod_ref[row + 1:row + 2, :])
    shift = mod_ref[row:row + 1, :]

    def chunk(c, carry):
        rows = pl.ds(pl.multiple_of(c * NORM_ROWS, NORM_ROWS), NORM_ROWS)
        x = x_ref[rows, :]
        r = lax.rsqrt(jnp.mean(x * x, axis=-1, keepdims=True) + EPS)
        h_scr[rows, :] = (x * r * gain + shift).astype(BF16)
        return carry

    lax.fori_loop(0, x_ref.shape[0] // NORM_ROWS, chunk, 0, unroll=4)


def _norm_proj_kernel(*refs, row, cast):
    x_ref, mod_ref, nw_ref, w_ref, o_ref = refs[:5]
    copy_ref = refs[5] if cast else None
    h_scr = refs[-1]

    @pl.when(pl.program_id(1) == 0)
    def _():
        _norm_mod(x_ref, mod_ref, nw_ref, h_scr, row)

    o_ref[...] = _dot(h_scr[...], _operand(w_ref, copy_ref)).astype(o_ref.dtype)


def _norm_proj(x, mod, mod_row0, rows_per_cond, norm_w, layer, w, n_out, *, row):
    m, d = x.shape
    tm = ROW_TILE
    cast = w.needs_cast
    tn = _col_tile(cast, wide=n_out % (2 * COL_TILE) == 0)
    nj = n_out // tn
    cond_of = lambda i: mod_row0 + (i * tm) // rows_per_cond
    out_specs = [pl.BlockSpec((tm, tn), lambda i, j: (i, j))]
    out_shape = [jax.ShapeDtypeStruct((m, n_out), BF16)]
    if cast:
        out_specs.append(_cast_copy_spec(d, tn, nj))
        out_shape.append(_cast_copy_shape(d, tn, nj))
    outs = pl.pallas_call(
        functools.partial(_norm_proj_kernel, row=row, cast=cast),
        grid=(m // tm, nj),
        in_specs=[
            pl.BlockSpec((tm, d), lambda i, j: (i, 0)),
            pl.BlockSpec((None, 6, d), lambda i, j: (cond_of(i), 0, 0)),
            pl.BlockSpec((None, 1, d), lambda i, j: (layer, 0, 0)),
            w.spec(d, tn),
        ],
        out_specs=out_specs,
        out_shape=out_shape,
        scratch_shapes=[pltpu.VMEM((tm, d), BF16)],
        compiler_params=_cparams("arbitrary", "arbitrary"),
        name="norm_proj",
    )(x, mod, norm_w, w.array)
    return outs[0], (_Weight(outs[1][None]) if cast else w)


def _conv3(u, cw, cb, first, last):
    t = u.shape[0]
    left = jnp.where(first, 0.0, pltpu.roll(u, 1, axis=0))
    right = jnp.where(last, 0.0, pltpu.roll(u, t - 1, axis=0))
    return left * cw[0:1, :] + u * cw[1:2, :] + right * cw[2:3, :] + cb


def _ffn_up_kernel(*refs, seq_len, cast):
    x_ref, mod_ref, nw_ref, wa_ref, wv_ref, cwa_ref, cwv_ref, cba_ref, cbv_ref, o_ref = refs[:10]
    copy_a, copy_v = refs[10:12] if cast else (None, None)
    h_scr = refs[-1]

    @pl.when(pl.program_id(1) == 0)
    def _():
        _norm_mod(x_ref, mod_ref, nw_ref, h_scr, 3)

    h = h_scr[...]
    tm = h.shape[0]
    pos = lax.broadcasted_iota(jnp.int32, (tm, 1), 0) % seq_len
    first = pos == 0
    last = pos == seq_len - 1
    a = _conv3(_dot(h, _operand(wa_ref, copy_a)), cwa_ref[...], cba_ref[...], first, last)
    v = _conv3(_dot(h, _operand(wv_ref, copy_v)), cwv_ref[...], cbv_ref[...], first, last)
    o_ref[...] = (a * _sigmoid(a) * v).astype(o_ref.dtype)


def _ffn_up(x, mod, mod_row0, rows_per_cond, norm_w, layer, w_gate, w_val, conv_w, conv_b,
            seq_len):
    m, d = x.shape
    d_ff = conv_w.shape[-1] // 2
    cast = w_gate.needs_cast
    assert w_val.needs_cast == cast
    tm, tn = ROW_TILE, _col_tile(cast)
    nj = d_ff // tn
    cond_of = lambda i: mod_row0 + (i * tm) // rows_per_cond
    conv_b3 = conv_b.reshape(conv_b.shape[0], 1, 2 * d_ff)
    out_specs = [pl.BlockSpec((tm, tn), lambda i, j: (i, j))]
    out_shape = [jax.ShapeDtypeStruct((m, d_ff), BF16)]
    if cast:
        out_specs += [_cast_copy_spec(d, tn, nj)] * 2
        out_shape += [_cast_copy_shape(d, tn, nj)] * 2
    outs = pl.pallas_call(
        functools.partial(_ffn_up_kernel, seq_len=seq_len, cast=cast),
        grid=(m // tm, nj),
        in_specs=[
            pl.BlockSpec((tm, d), lambda i, j: (i, 0)),
            pl.BlockSpec((None, 6, d), lambda i, j: (cond_of(i), 0, 0)),
            pl.BlockSpec((None, 1, d), lambda i, j: (layer, 0, 0)),
            w_gate.spec(d, tn),
            w_val.spec(d, tn),
            pl.BlockSpec((None, 3, tn), lambda i, j: (layer, 0, j)),
            pl.BlockSpec((None, 3, tn), lambda i, j: (layer, 0, nj + j)),
            pl.BlockSpec((None, 1, tn), lambda i, j: (layer, 0, j)),
            pl.BlockSpec((None, 1, tn), lambda i, j: (layer, 0, nj + j)),
        ],
        out_specs=out_specs,
        out_shape=out_shape,
        scratch_shapes=[pltpu.VMEM((tm, d), BF16)],
        compiler_params=_cparams("arbitrary", "arbitrary"),
        name="ffn_up",
    )(x, mod, norm_w, w_gate.array, w_val.array, conv_w, conv_w, conv_b3, conv_b3)
    if cast:
        w_gate, w_val = _Weight(outs[1][None]), _Weight(outs[2][None])
    return outs[0], w_gate, w_val


def _proj_res_kernel(*refs, n_in, row, cast):
    a_refs = refs[:n_in]
    w_refs = refs[n_in:2 * n_in]
    y_ref, mod_ref, o_ref = refs[2 * n_in:2 * n_in + 3]
    copy_refs = refs[2 * n_in + 3:] if cast else (None,) * n_in
    acc = _dot(a_refs[0][...], _operand(w_refs[0], copy_refs[0]))
    for a_ref, w_ref, copy_ref in zip(a_refs[1:], w_refs[1:], copy_refs[1:]):
        acc += _dot(a_ref[...], _operand(w_ref, copy_ref))
    o_ref[...] = y_ref[...] + mod_ref[row:row + 1, :] * acc


def _proj_res(acts, ws, y, mod, mod_row0, rows_per_cond, *, row):
    m, d = y.shape
    n_in = len(acts)
    k_blk = acts[0].shape[1]
    assert all(a.shape[1] == k_blk for a in acts)
    cast = ws[0].needs_cast
    assert all(w.needs_cast == cast for w in ws)
    tm, tn = ROW_TILE, _col_tile(cast)
    nj = d // tn
    cond_of = lambda i: mod_row0 + (i * tm) // rows_per_cond
    in_specs = [pl.BlockSpec((tm, k_blk), lambda i, j: (i, 0)) for _ in acts]
    in_specs += [w.spec(k_blk, tn) for w in ws]
    in_specs += [
        pl.BlockSpec((tm, tn), lambda i, j: (i, j)),
        pl.BlockSpec((None, 6, tn), lambda i, j: (cond_of(i), 0, j)),
    ]
    out_specs = [pl.BlockSpec((tm, tn), lambda i, j: (i, j))]
    out_shape = [jax.ShapeDtypeStruct((m, d), F32)]
    if cast:
        out_specs += [_cast_copy_spec(k_blk, tn, nj)] * n_in
        out_shape += [_cast_copy_shape(k_blk, tn, nj)] * n_in
    outs = pl.pallas_call(
        functools.partial(_proj_res_kernel, n_in=n_in, row=row, cast=cast),
        grid=(m // tm, nj),
        in_specs=in_specs,
        out_specs=out_specs,
        out_shape=out_shape,
        compiler_params=_cparams("arbitrary", "arbitrary"),
        name="proj_res",
    )(*acts, *[w.array for w in ws], y, mod)
    if cast:
        ws = [_Weight(o[None]) for o in outs[1:]]
    return outs[0], ws


def _rope_half_roll(x, cos2, sin2):
    return x * cos2 + pltpu.roll(x, x.shape[1] // 2, axis=1) * sin2


def _attn_kernel(*refs, n, n_ctx, rope, emit_kv, tq):
    q_ref, k_ref, v_ref, qw_ref, kw_ref = refs[:5]
    pos = 5
    if rope:
        cos_ref, sin_ref, ck_ref, cv_ref = refs[pos:pos + 4]
        pos += 4
    o_ref = refs[pos]
    pos += 1
    if emit_kv:
        nk_ref, nv_ref = refs[pos:pos + 2]
        pos += 2
    kall, vall = refs[pos:pos + 2]

    kn = _rms(k_ref[...].astype(F32), kw_ref[...])
    vv = v_ref[...]
    if emit_kv:
        nk_ref[...] = kn
        nv_ref[...] = vv.astype(F32)
    if rope:
        kn = _rope_half_roll(kn, cos_ref[...], sin_ref[...])
        kall[0:n_ctx, :] = ck_ref[...].astype(BF16)
        vall[0:n_ctx, :] = cv_ref[...].astype(BF16)
    kall[n_ctx:n_ctx + n, :] = kn.astype(BF16)
    vall[n_ctx:n_ctx + n, :] = vv.astype(BF16)

    scale = A_HEAD_DIM ** -0.5

    def chunk(c, carry):
        r0 = pl.multiple_of(c * tq, tq)
        rows = pl.ds(r0, tq)
        heads = [slice(g * A_HEAD_DIM, (g + 1) * A_HEAD_DIM) for g in range(A_GROUP)]
        qs = []
        for cols in heads:
            qh = _rms(q_ref[rows, cols].astype(F32), qw_ref[...])
            if rope:
                qh = _rope_half_roll(qh, cos_ref[rows, :], sin_ref[rows, :])
            qs.append((qh * scale).astype(BF16))
        ss = [_dot_nt(qh, kall[...]) for qh in qs]
        ps = [jnp.exp(s - jnp.max(s, axis=-1, keepdims=True)) for s in ss]
        ls = [jnp.sum(p, axis=-1, keepdims=True) for p in ps]
        os = [_dot(p.astype(BF16), vall[...]) / l for p, l in zip(ps, ls)]
        for cols, o in zip(heads, os):
            o_ref[rows, cols] = o.astype(o_ref.dtype)
        return carry

    lax.fori_loop(0, n // tq, chunk, 0)


def _attention(proj, n_seq, n, q_norm_w, k_norm_w, e, rope_tabs, cache_k, cache_v, emit_kv):
    rope = rope_tabs is not None
    n_ctx = cache_k.shape[2] if rope else 0
    hd = A_HEAD_DIM
    qcols = A_GROUP * hd
    in_specs = [
        pl.BlockSpec((n, qcols), lambda b, kv: (b, kv)),
        pl.BlockSpec((n, hd), lambda b, kv: (b, A_Q // hd + kv)),
        pl.BlockSpec((n, hd), lambda b, kv: (b, (A_Q + A_KV) // hd + kv)),
        pl.BlockSpec((None, 1, hd), lambda b, kv: (e, 0, 0)),
        pl.BlockSpec((None, 1, hd), lambda b, kv: (e, 0, 0)),
    ]
    args = [proj, proj, proj, q_norm_w, k_norm_w]
    if rope:
        cos2, sin2 = rope_tabs
        in_specs += [
            pl.BlockSpec((n, hd), lambda b, kv: (0, 0)),
            pl.BlockSpec((n, hd), lambda b, kv: (0, 0)),
            pl.BlockSpec((None, None, n_ctx, hd), lambda b, kv: (b, e, 0, kv)),
            pl.BlockSpec((None, None, n_ctx, hd), lambda b, kv: (b, e, 0, kv)),
        ]
        ck = cache_k.reshape(cache_k.shape[0], cache_k.shape[1], n_ctx, A_KV)
        cv = cache_v.reshape(cache_v.shape[0], cache_v.shape[1], n_ctx, A_KV)
        args += [cos2, sin2, ck, cv]
    out_specs = [pl.BlockSpec((n, qcols), lambda b, kv: (b, kv))]
    out_shape = [jax.ShapeDtypeStruct((n_seq * n, A_Q), BF16)]
    if emit_kv:
        out_specs += [pl.BlockSpec((n, hd), lambda b, kv: (b, kv))] * 2
        out_shape += [jax.ShapeDtypeStruct((n_seq * n, A_KV), F32)] * 2
    return pl.pallas_call(
        functools.partial(_attn_kernel, n=n, n_ctx=n_ctx, rope=rope, emit_kv=emit_kv,
                          tq=min(n, 256)),
        grid=(n_seq, A_KV_HEADS),
        in_specs=in_specs,
        out_specs=out_specs,
        out_shape=out_shape,
        scratch_shapes=[pltpu.VMEM((n_ctx + n, hd), BF16), pltpu.VMEM((n_ctx + n, hd), BF16)],
        compiler_params=_cparams("parallel", "arbitrary"),
        name="attention",
    )(*args)


def _gla_levels(c, rev):
    row = lax.broadcasted_iota(jnp.int32, (c, c), 0)
    col = lax.broadcasted_iota(jnp.int32, (c, c), 1)
    shift = HGRN_BASE.bit_length() - 1
    x = (row >> shift) ^ (col >> shift)
    lvl = jnp.zeros((c, c), jnp.int32)
    for l in range(1, (c // HGRN_BASE).bit_length()):
        lvl = jnp.where(x >= (1 << (l - 1)), l, lvl)
    causal = (col >= row) if rev else (col <= row)
    return jnp.where(causal, lvl, -1)


def _gla_blocks(chains):
    c, dk = chains[0]["q"].shape
    each = lambda fn: [fn(ch) for ch in chains]

    def split3(ch):
        hi = ch["lf"].astype(BF16)
        r1 = ch["lf"] - hi.astype(F32)
        mid = r1.astype(BF16)
        ch["parts"] = (hi, mid, (r1 - mid.astype(F32)).astype(BF16))

    def cumulate(ch):
        hi, mid, lo = ch["parts"]
        ch["b"] = _dot(ch["tri"], hi) + _dot(ch["tri"], mid) + _dot(ch["tri"], lo)
        ch["tot"] = ch["b"][0:1, :] if ch["rev"] else ch["b"][c - 1:c, :]

    def inter(ch):
        ch["o"] = _dot_nt((ch["q"] * jnp.exp(ch["b"])).astype(BF16), ch["st"].astype(BF16))
        ch["khat"] = (ch["k"] * jnp.exp(ch["tot"] - ch["b"])).astype(BF16)
        ch["vb"] = ch["v"].astype(BF16)

    def level0(ch):
        b3 = ch["b"].reshape(c // HGRN_BASE, HGRN_BASE, dk)
        mid_row = HGRN_BASE // 2 if ch["rev"] else HGRN_BASE // 2 - 1
        a = jnp.clip(b3 - b3[:, mid_row:mid_row + 1, :], -EXP_CLAMP, EXP_CLAMP).reshape(c, dk)
        p = _dot_nt((ch["q"] * jnp.exp(a)).astype(BF16), (ch["k"] * jnp.exp(-a)).astype(BF16))
        ch["scores"] = jnp.where(ch["lvl"] == 0, p, 0.0)

    def upper_level(ch, h, level):
        b3 = ch["b"].reshape(c // (2 * h), 2 * h, dk)
        ref_row = h if ch["rev"] else h - 1
        e = jnp.exp(-jnp.abs(b3 - b3[:, ref_row:ref_row + 1, :])).reshape(c, dk)
        p = _dot_nt((ch["q"] * e).astype(BF16), (ch["k"] * e).astype(BF16))
        ch["scores"] = jnp.where(ch["lvl"] == level, p, ch["scores"])

    def combine(ch):
        o = ch["o"] + _dot(ch["scores"].astype(BF16), ch["vb"])
        st_new = ch["st"] * jnp.exp(ch["tot"]) + _dot_tn(ch["vb"], ch["khat"])
        return o, st_new

    each(split3)
    each(cumulate)
    each(inter)
    each(level0)
    h, level = HGRN_BASE, 1
    while h < c:
        each(functools.partial(upper_level, h=h, level=level))
        h, level = 2 * h, level + 1
    return each(combine)


def _hgrn_kernel(*refs, n, has_state, emit_state):
    q_ref, i_ref, zf_ref, zb_ref, g_ref, lb_ref, nw_ref = refs[:7]
    pos = 7
    if has_state:
        s0f_ref, s0b_ref = refs[pos:pos + 2]
        pos += 2
    o_ref = refs[pos]
    pos += 1
    if emit_state:
        sf_ref, sb_ref = refs[pos:pos + 2]
        pos += 2
    of_scr, ob_scr, st_scr = refs[pos:pos + 3]

    c = MIX_BLOCK
    nb = n // c
    qscale = B_DK ** -0.5
    lvl_f = _gla_levels(c, False)
    lvl_b = _gla_levels(c, True)
    tri_f = jnp.where(lvl_f >= 0, 1.0, 0.0).astype(BF16)
    tri_b = jnp.where(lvl_b >= 0, 1.0, 0.0).astype(BF16)
    heads = [slice(hh * B_DK, (hh + 1) * B_DK) for hh in range(HGRN_HEADS_PER_STEP)]

    def chain(z_ref, rows, hh, rev):
        cols = heads[hh]
        lb = lb_ref[hh]
        f = lb + (1.0 - lb) * _sigmoid(z_ref[rows, cols].astype(F32))
        return dict(q=q_ref[rows, cols].astype(F32) * qscale, k=1.0 - f,
                    v=i_ref[rows, cols].astype(F32), lf=jnp.log(f),
                    st=st_scr[2 * hh + int(rev)], rev=rev,
                    lvl=lvl_b if rev else lvl_f, tri=tri_b if rev else tri_f)

    def step(j, carry):
        rows_f = pl.ds(pl.multiple_of(j * c, c), c)
        rows_b = pl.ds(pl.multiple_of((nb - 1 - j) * c, c), c)
        chains = []
        for hh in range(len(heads)):
            chains += [chain(zf_ref, rows_f, hh, False), chain(zb_ref, rows_b, hh, True)]
        outs = _gla_blocks(chains)
        for hh, cols in enumerate(heads):
            (o_f, st_f), (o_b, st_b) = outs[2 * hh], outs[2 * hh + 1]
            of_scr[rows_f, cols] = o_f
            ob_scr[rows_b, cols] = o_b
            st_scr[2 * hh] = st_f
            st_scr[2 * hh + 1] = st_b
        return carry

    def finish(j, carry):
        rows = pl.ds(pl.multiple_of(j * c, c), c)
        for cols in heads:
            y = _rms(of_scr[rows, cols] + ob_scr[rows, cols], nw_ref[...])
            g = g_ref[rows, cols].astype(F32)
            o_ref[rows, cols] = (y * (g * _sigmoid(g))).astype(o_ref.dtype)
        return carry

    for hh in range(len(heads)):
        if has_state:
            st_scr[2 * hh] = s0f_ref[hh].T
            st_scr[2 * hh + 1] = s0b_ref[hh].T
        else:
            st_scr[2 * hh] = jnp.zeros((B_DV, B_DK), F32)
            st_scr[2 * hh + 1] = jnp.zeros((B_DV, B_DK), F32)
    lax.fori_loop(0, nb, step, 0)
    lax.fori_loop(0, nb, finish, 0)
    if emit_state:
        for hh in range(len(heads)):
            sf_ref[hh] = st_scr[2 * hh].T
            sb_ref[hh] = st_scr[2 * hh + 1].T


def _hgrn(proj, n_seq, n, lb, o_norm_w, e, state_f, state_b, emit_state):
    has_state = state_f is not None
    hp = HGRN_HEADS_PER_STEP
    d = B_DK * hp
    base = (A_Q + 2 * A_KV) // d
    col = lambda k: (lambda b, h: (b, base + k * (B_HEADS // hp) + h))
    in_specs = [pl.BlockSpec((n, d), col(k)) for k in range(5)]
    in_specs += [
        pl.BlockSpec((hp, 1, B_DK), lambda b, h: (h, 0, 0)),
        pl.BlockSpec((None, 1, B_DV), lambda b, h: (e, 0, 0)),
    ]
    args = [proj] * 5 + [lb, o_norm_w]
    if has_state:
        st_spec = pl.BlockSpec((None, None, hp, B_DK, B_DV), lambda b, h: (b, e, h, 0, 0))
        in_specs += [st_spec, st_spec]
        args += [state_f, state_b]
    out_specs = [pl.BlockSpec((n, d), lambda b, h: (b, h))]
    out_shape = [jax.ShapeDtypeStruct((n_seq * n, B_V), BF16)]
    if emit_state:
        so = pl.BlockSpec((None, hp, B_DK, B_DV), lambda b, h: (b, h, 0, 0))
        out_specs += [so, so]
        out_shape += [jax.ShapeDtypeStruct((n_seq, B_HEADS, B_DK, B_DV), F32)] * 2
    return pl.pallas_call(
        functools.partial(_hgrn_kernel, n=n, has_state=has_state, emit_state=emit_state),
        grid=(n_seq, B_HEADS // hp),
        in_specs=in_specs,
        out_specs=out_specs,
        out_shape=out_shape,
        scratch_shapes=[pltpu.VMEM((n, d), F32), pltpu.VMEM((n, d), F32),
                        pltpu.VMEM((2 * hp, B_DV, B_DK), F32)],
        compiler_params=_cparams("parallel", "arbitrary"),
        name="hgrn2",
    )(*args)


def _rope_split(x, cos, sin):
    half = x.shape[1] // 2
    x1, x2 = x[:, :half], x[:, half:]
    return jnp.concatenate([x1 * cos - x2 * sin, x1 * sin + x2 * cos], axis=1)


def _ret_kernel(*refs, n, rope, has_state, emit_state):
    q_ref, k_ref, v_ref, g_ref, lgf_ref, lgb_ref, nw_ref = refs[:7]
    pos = 7
    if rope:
        cos_ref, sin_ref = refs[pos:pos + 2]
        pos += 2
    if has_state:
        s0f_ref, s0b_ref = refs[pos:pos + 2]
        pos += 2
    o_ref = refs[pos]
    pos += 1
    if emit_state:
        sf_ref, sb_ref = refs[pos:pos + 2]
        pos += 2
    qs_scr, ks_scr, of_scr, ob_scr, stf_scr, stb_scr, dm_scr, qd_scr, kd_scr = refs[pos:pos + 9]

    c = min(RET_BLOCK, n)
    nb = n // c
    skip_inter = (not has_state) and nb == 1
    kscale = C_DK ** -0.5
    rowi = lax.broadcasted_iota(jnp.int32, (c, c), 0)
    coli = lax.broadcasted_iota(jnp.int32, (c, c), 1)
    rowq = lax.broadcasted_iota(jnp.int32, (c, C_DK), 0).astype(F32)

    def prep(j, carry):
        rows = pl.ds(pl.multiple_of(j * MIX_BLOCK, MIX_BLOCK), MIX_BLOCK)
        q = q_ref[rows, :].astype(F32)
        k = k_ref[rows, :].astype(F32)
        if rope:
            q = _rope_split(q, cos_ref[rows, :], sin_ref[rows, :])
            k = _rope_split(k, cos_ref[rows, :], sin_ref[rows, :])
        qs_scr[rows, :] = q
        ks_scr[rows, :] = k * kscale
        return carry

    @pl.when(pl.program_id(1) == 0)
    def _():
        lgs = (lgf_ref[...], lgb_ref[...])
        dist = (rowi - coli).astype(F32)
        dm_scr[...] = (
            jnp.where(dist >= 0.0, jnp.exp(lgs[0][:, :c] * jnp.maximum(dist, 0.0)), 0.0)
            + jnp.where(dist <= 0.0, jnp.exp(lgs[1][:, :c] * jnp.maximum(-dist, 0.0)), 0.0))
        for d, rev in enumerate((False, True)):
            lgq = lgs[d][:, :C_DK]
            qd_scr[d] = jnp.exp(lgq * ((c - rowq) if rev else (rowq + 1.0)))
            kd_scr[d] = jnp.exp(lgq * (rowq if rev else (c - 1.0 - rowq)))

    def sweep_block(rows, d, lg_ref, st_scr, o_scr):
        vb = v_ref[rows, :]
        u = _dot_tn((ks_scr[rows, :] * kd_scr[d]).astype(BF16), vb)
        if skip_inter:
            st_scr[...] = u
        else:
            st = st_scr[...]
            o_scr[rows, :] = _dot((qs_scr[rows, :] * qd_scr[d]).astype(BF16), st.astype(BF16))
            st_scr[...] = st * jnp.exp(lg_ref[...] * float(c)) + u

    def sweep(j, carry):
        rows_f = pl.ds(pl.multiple_of(j * c, c), c)
        rows_b = pl.ds(pl.multiple_of((nb - 1 - j) * c, c), c)
        sweep_block(rows_f, 0, lgf_ref, stf_scr, of_scr)
        sweep_block(rows_b, 1, lgb_ref, stb_scr, ob_scr)
        return carry

    def finish(j, carry):
        keys = pl.ds(pl.multiple_of(j * c, c), c)
        kb = ks_scr[keys, :].astype(BF16)
        vb = v_ref[keys, :]
        pieces = [(pl.ds(pl.multiple_of(j * c + p * MIX_BLOCK, MIX_BLOCK), MIX_BLOCK),
                   slice(p * MIX_BLOCK, (p + 1) * MIX_BLOCK)) for p in range(c // MIX_BLOCK)]
        ss = [_dot_nt(qs_scr[rows, :].astype(BF16), kb) * dm_scr[within, :]
              for rows, within in pieces]
        os = [_dot(s.astype(BF16), vb) for s in ss]
        if not skip_inter:
            os = [o + of_scr[rows, :] + ob_scr[rows, :] for o, (rows, _) in zip(os, pieces)]
        ys = [_rms(o, nw_ref[...]) for o in os]
        for y, (rows, _) in zip(ys, pieces):
            g = g_ref[rows, :].astype(F32)
            o_ref[rows, :] = (y * (g * _sigmoid(g))).astype(o_ref.dtype)
        return carry

    lax.fori_loop(0, n // MIX_BLOCK, prep, 0)
    if has_state:
        stf_scr[...] = s0f_ref[...]
        stb_scr[...] = s0b_ref[...]
    elif not skip_inter:
        stf_scr[...] = jnp.zeros((C_DK, C_DV), F32)
        stb_scr[...] = jnp.zeros((C_DK, C_DV), F32)
    if emit_state or not skip_inter:
        lax.fori_loop(0, nb, sweep, 0)
    lax.fori_loop(0, nb, finish, 0, unroll=2)
    if emit_state:
        sf_ref[...] = stf_scr[...]
        sb_ref[...] = stb_scr[...]


def _retention(proj, n_seq, n, lg_f, lg_b, o_norm_w, o_idx, rope_tabs, state_f, state_b,
               emit_state):
    rope = rope_tabs is not None
    has_state = state_f is not None
    nq = C_QK // C_DK
    c = min(RET_BLOCK, n)
    in_specs = [
        pl.BlockSpec((n, C_DK), lambda h, b: (b, h)),
        pl.BlockSpec((n, C_DK), lambda h, b: (b, nq + h)),
        pl.BlockSpec((n, C_DV), lambda h, b: (b, 2 * C_QK // C_DV + h)),
        pl.BlockSpec((n, C_DV), lambda h, b: (b, (2 * C_QK + C_V) // C_DV + h)),
        pl.BlockSpec((None, 1, C_DV), lambda h, b: (h, 0, 0)),
        pl.BlockSpec((None, 1, C_DV), lambda h, b: (h, 0, 0)),
        pl.BlockSpec((None, 1, C_DV), lambda h, b: (o_idx, 0, 0)),
    ]
    args = [proj] * 4 + [lg_f, lg_b, o_norm_w]
    if rope:
        in_specs += [pl.BlockSpec((n, C_DK // 2), lambda h, b: (0, 0))] * 2
        args += list(rope_tabs)
    if has_state:
        st_spec = pl.BlockSpec((None, None, None, C_DK, C_DV), lambda h, b: (b, o_idx, h, 0, 0))
        in_specs += [st_spec, st_spec]
        args += [state_f, state_b]
    out_specs = [pl.BlockSpec((n, C_DV), lambda h, b: (b, h))]
    out_shape = [jax.ShapeDtypeStruct((n_seq * n, C_V), BF16)]
    if emit_state:
        so = pl.BlockSpec((None, None, C_DK, C_DV), lambda h, b: (b, h, 0, 0))
        out_specs += [so, so]
        out_shape += [jax.ShapeDtypeStruct((n_seq, C_HEADS, C_DK, C_DV), F32)] * 2
    return pl.pallas_call(
        functools.partial(_ret_kernel, n=n, rope=rope, has_state=has_state,
                          emit_state=emit_state),
        grid=(C_HEADS, n_seq),
        in_specs=in_specs,
        out_specs=out_specs,
        out_shape=out_shape,
        scratch_shapes=[pltpu.VMEM((n, C_DK), F32), pltpu.VMEM((n, C_DK), F32),
                        pltpu.VMEM((n, C_DV), F32), pltpu.VMEM((n, C_DV), F32),
                        pltpu.VMEM((C_DK, C_DV), F32), pltpu.VMEM((C_DK, C_DV), F32),
                        pltpu.VMEM((c, c), F32), pltpu.VMEM((2, c, C_DK), F32),
                        pltpu.VMEM((2, c, C_DK), F32)],
        compiler_params=_cparams("arbitrary", "arbitrary"),
        name="retention",
    )(*args)


def _rope_tables(n_tokens, head_dim):
    rows = n_tokens // GRID_W
    row = jnp.repeat(jnp.arange(rows, dtype=F32), GRID_W)
    col = jnp.tile(jnp.arange(GRID_W, dtype=F32), rows)
    quarter = head_dim // 4
    inv_freq = jnp.power(ROPE_BASE, -jnp.arange(quarter, dtype=F32) / quarter)
    ang = jnp.concatenate([row[:, None] * inv_freq, col[:, None] * inv_freq], axis=-1)
    return jnp.cos(ang), jnp.sin(ang)


def kernel(x_prompt, x_sample, cache_attn_k, cache_attn_v, state_hgrn_fwd, state_hgrn_bwd,
           state_ret_fwd, state_ret_bwd, c, c_ctx, w_mod, b_mod, norm_mix_w, norm_ffn_w,
           w_in_even, w_out_even, attn_q_norm_w, attn_k_norm_w, hgrn_lb, hgrn_o_norm_w,
           w_in_odd, w_out_odd, ret_decay_fwd, ret_decay_bwd, ret_o_norm_w,
           w_up, conv_w, conv_b, w_down):
    depth, d_model = norm_mix_w.shape
    bp, np_, _ = x_prompt.shape
    bs, ns, _ = x_sample.shape

    lb_all = jnp.cumsum(jax.nn.softmax(hgrn_lb.astype(F32), axis=0), axis=0)
    lg_f = jnp.broadcast_to(jax.nn.log_sigmoid(ret_decay_fwd.astype(F32))[:, :, None, None],
                            ret_decay_fwd.shape + (1, C_DV))
    lg_b = jnp.broadcast_to(jax.nn.log_sigmoid(ret_decay_bwd.astype(F32))[:, :, None, None],
                            ret_decay_bwd.shape + (1, C_DV))
    cos_a, sin_a = _rope_tables(ns, A_HEAD_DIM)
    rope_a = (jnp.concatenate([cos_a, cos_a], axis=1), jnp.concatenate([-sin_a, sin_a], axis=1))
    rope_c = _rope_tables(ns, C_DK)
    nmw = norm_mix_w.reshape(depth, 1, d_model)
    nfw = norm_ffn_w.reshape(depth, 1, d_model)
    qnw = attn_q_norm_w.reshape(-1, 1, A_HEAD_DIM)
    knw = attn_k_norm_w.reshape(-1, 1, A_HEAD_DIM)
    hnw = hgrn_o_norm_w.reshape(-1, 1, B_DV)
    rnw = ret_o_norm_w.reshape(-1, 1, C_DV)
    n_cond = 1 + bs
    pad = (-n_cond) % 8
    cond = jnp.concatenate([c_ctx[None, :], c, jnp.zeros((pad, d_model), F32)], axis=0)
    mod_all = _modulation(cond, w_mod, b_mod).reshape(depth, n_cond + pad, 6, d_model)

    groups = (
        dict(x=x_prompt.reshape(bp * np_, d_model), n_seq=bp, n=np_, row0=0,
             rows_per_cond=bp * np_, latent=False),
        dict(x=x_sample.reshape(bs * ns, d_model), n_seq=bs, n=ns, row0=1,
             rows_per_cond=ns, latent=True),
    )
    weights = {}
    for l in range(depth):
        if l % 2 == 0:
            weights["in", l] = _Weight(w_in_even, l // 2)
            weights["out", l] = [_Weight(w_out_even, l // 2, k) for k in range(2)]
        else:
            weights["in", l] = _Weight(w_in_odd, l // 2)
            weights["out", l] = [_Weight(w_out_odd, l // 2)]
        weights["gate", l] = _Weight(w_up, l)
        weights["val", l] = _Weight(w_up, l, 0, w_down.shape[1])
        weights["down", l] = [_Weight(w_down, l)]

    results = []
    for grp in groups:
        y, n_seq, n = grp["x"], grp["n_seq"], grp["n"]
        row0, rpc, latent = grp["row0"], grp["rows_per_cond"], grp["latent"]
        new = dict(k=[], v=[], hf=[], hb=[], rf=[], rb=[])
        for l in range(depth):
            mod = mod_all[l]
            if l % 2 == 0:
                e = l // 2
                proj, weights["in", l] = _norm_proj(y, mod, row0, rpc, nmw, l, weights["in", l],
                                                    w_in_even.shape[-1], row=0)
                att = _attention(proj, n_seq, n, qnw, knw, e, rope_a if latent else None,
                                 cache_attn_k, cache_attn_v, emit_kv=not latent)
                lb = lb_all[e].reshape(B_HEADS, 1, B_DK)
                hg = _hgrn(proj, n_seq, n, lb, hnw, e,
                           state_hgrn_fwd if latent else None,
                           state_hgrn_bwd if latent else None, emit_state=not latent)
                if not latent:
                    new["k"].append(att[1].reshape(n_seq, n, A_KV_HEADS, A_HEAD_DIM))
                    new["v"].append(att[2].reshape(n_seq, n, A_KV_HEADS, A_HEAD_DIM))
                    new["hf"].append(hg[1])
                    new["hb"].append(hg[2])
                mixed = [att[0], hg[0]]
            else:
                o = l // 2
                proj, weights["in", l] = _norm_proj(y, mod, row0, rpc, nmw, l, weights["in", l],
                                                    w_in_odd.shape[-1], row=0)
                rt = _retention(proj, n_seq, n, lg_f[o], lg_b[o], rnw, o,
                                rope_c if latent else None,
                                state_ret_fwd if latent else None,
                                state_ret_bwd if latent else None, emit_state=not latent)
                if not latent:
                    new["rf"].append(rt[1])
                    new["rb"].append(rt[2])
                mixed = [rt[0]]
            y, weights["out", l] = _proj_res(mixed, weights["out", l], y, mod, row0, rpc, row=2)
            act, weights["gate", l], weights["val", l] = _ffn_up(
                y, mod, row0, rpc, nfw, l, weights["gate", l], weights["val", l], conv_w, conv_b, n)
            y, weights["down", l] = _proj_res([act], weights["down", l], y, mod, row0, rpc, row=5)
        results.append((y.reshape(n_seq, n, d_model), new))

    (y_p, new), (y_s, _) = results
    stack = lambda xs: jnp.stack(xs, axis=1)
    return (y_p, y_s, stack(new["k"]), stack(new["v"]), stack(new["hf"]), stack(new["hb"]),
            stack(new["rf"]), stack(new["rb"]))
```

```python
import functools

import jax
import jax.numpy as jnp
from jax import lax
from jax.experimental import pallas as pl
from jax.experimental.pallas import tpu as pltpu

F32 = jnp.float32
BF16 = jnp.bfloat16

GRID_W = 64
A_HEADS = 8
A_KV_HEADS = 2
A_HEAD_DIM = 128
ROPE_BASE = 10000.0
B_HEADS = 8
B_DK = 128
B_DV = 128
C_HEADS = 8
C_DK = 256
C_DV = 512
EPS = 1e-6

A_Q = A_HEADS * A_HEAD_DIM
A_KV = A_KV_HEADS * A_HEAD_DIM
A_GROUP = A_HEADS // A_KV_HEADS
B_QK = B_HEADS * B_DK
B_V = B_HEADS * B_DV
C_QK = C_HEADS * C_DK
C_V = C_HEADS * C_DV

VMEM_LIMIT_BYTES = 56 * 1024 * 1024
ROW_TILE = 1024
COL_TILE = 512
MIX_BLOCK = 128
RET_BLOCK = 256
NORM_ROWS = 16
HGRN_BASE = 16
HGRN_HEADS_PER_STEP = 4
EXP_CLAMP = 80.0


def _cparams(*sem):
    return pltpu.CompilerParams(dimension_semantics=sem, vmem_limit_bytes=VMEM_LIMIT_BYTES)


def _dot(a, b):
    return jnp.dot(a, b, preferred_element_type=F32)


def _dot_nt(a, b):
    return lax.dot_general(a, b, (((1,), (1,)), ((), ())), preferred_element_type=F32)


def _dot_tn(a, b):
    return lax.dot_general(a, b, (((0,), (0,)), ((), ())), preferred_element_type=F32)


def _sigmoid(x):
    return 1.0 / (1.0 + jnp.exp(-x))


def _rms(x, w):
    return x * lax.rsqrt(jnp.mean(x * x, axis=-1, keepdims=True) + EPS) * w


def _mod_kernel(c_ref, w_ref, b_ref, o_ref):
    c = c_ref[...]
    s = (c * _sigmoid(c)).astype(BF16)
    o_ref[...] = _dot(s, w_ref[...].astype(BF16)) + b_ref[...]


def _modulation(cond, w_mod, b_mod):
    n_layers, d, n_out = w_mod.shape
    r = cond.shape[0]
    tn = 1024
    return pl.pallas_call(
        _mod_kernel,
        grid=(n_layers, n_out // tn),
        in_specs=[
            pl.BlockSpec((r, d), lambda l, j: (0, 0)),
            pl.BlockSpec((None, d, tn), lambda l, j: (l, 0, j)),
            pl.BlockSpec((None, 1, tn), lambda l, j: (l, 0, j)),
        ],
        out_specs=pl.BlockSpec((None, r, tn), lambda l, j: (l, 0, j)),
        out_shape=jax.ShapeDtypeStruct((n_layers, r, n_out), F32),
        compiler_params=_cparams("arbitrary", "arbitrary"),
        name="modulation",
    )(cond, w_mod, b_mod.reshape(n_layers, 1, n_out))


def _norm_mod(x_ref, mod_ref, nw_ref, h_scr, row):
    gain = nw_ref[...] * (1.0 + mod_ref[row + 1:row + 2, :])
    shift = mod_ref[row:row + 1, :]

    def chunk(c, carry):
        rows = pl.ds(pl.multiple_of(c * NORM_ROWS, NORM_ROWS), NORM_ROWS)
        x = x_ref[rows, :]
        r = lax.rsqrt(jnp.mean(x * x, axis=-1, keepdims=True) + EPS)
        h_scr[rows, :] = (x * r * gain + shift).astype(BF16)
        return carry

    lax.fori_loop(0, x_ref.shape[0] // NORM_ROWS, chunk, 0, unroll=4)


def _norm_proj_kernel(x_ref, mod_ref, nw_ref, w_ref, o_ref, h_scr, *, row):
    @pl.when(pl.program_id(1) == 0)
    def _():
        _norm_mod(x_ref, mod_ref, nw_ref, h_scr, row)

    o_ref[...] = _dot(h_scr[...], w_ref[...]).astype(o_ref.dtype)


def _norm_proj(x, mod, mod_row0, rows_per_cond, norm_w, layer, w, w_idx, *, row, out_dtype):
    m, d = x.shape
    n_out = w.shape[-1]
    tm = ROW_TILE
    tn = 2 * COL_TILE if n_out % (2 * COL_TILE) == 0 else COL_TILE
    cond_of = lambda i: mod_row0 + (i * tm) // rows_per_cond
    return pl.pallas_call(
        functools.partial(_norm_proj_kernel, row=row),
        grid=(m // tm, n_out // tn),
        in_specs=[
            pl.BlockSpec((tm, d), lambda i, j: (i, 0)),
            pl.BlockSpec((None, 6, d), lambda i, j: (cond_of(i), 0, 0)),
            pl.BlockSpec((None, 1, d), lambda i, j: (layer, 0, 0)),
            pl.BlockSpec((None, d, tn), lambda i, j: (w_idx, 0, j)),
        ],
        out_specs=pl.BlockSpec((tm, tn), lambda i, j: (i, j)),
        out_shape=jax.ShapeDtypeStruct((m, n_out), out_dtype),
        scratch_shapes=[pltpu.VMEM((tm, d), BF16)],
        compiler_params=_cparams("parallel", "arbitrary"),
        name="norm_proj",
    )(x, mod, norm_w, w)


def _conv3(u, cw, cb, first, last):
    t = u.shape[0]
    left = jnp.where(first, 0.0, pltpu.roll(u, 1, axis=0))
    right = jnp.where(last, 0.0, pltpu.roll(u, t - 1, axis=0))
    return left * cw[0:1, :] + u * cw[1:2, :] + right * cw[2:3, :] + cb


def _ffn_up_kernel(x_ref, mod_ref, nw_ref, wa_ref, wv_ref, cwa_ref, cwv_ref, cba_ref, cbv_ref,
                   o_ref, h_scr, *, seq_len):
    @pl.when(pl.program_id(1) == 0)
    def _():
        _norm_mod(x_ref, mod_ref, nw_ref, h_scr, 3)

    h = h_scr[...]
    tm = h.shape[0]
    pos = lax.broadcasted_iota(jnp.int32, (tm, 1), 0) % seq_len
    first = pos == 0
    last = pos == seq_len - 1
    a = _conv3(_dot(h, wa_ref[...]), cwa_ref[...], cba_ref[...], first, last)
    v = _conv3(_dot(h, wv_ref[...]), cwv_ref[...], cbv_ref[...], first, last)
    o_ref[...] = (a * _sigmoid(a) * v).astype(o_ref.dtype)


def _ffn_up(x, mod, mod_row0, rows_per_cond, norm_w, layer, w_up, conv_w, conv_b, seq_len):
    m, d = x.shape
    d_ff = w_up.shape[-1] // 2
    tm, tn = ROW_TILE, COL_TILE
    nj = d_ff // tn
    cond_of = lambda i: mod_row0 + (i * tm) // rows_per_cond
    conv_b3 = conv_b.reshape(conv_b.shape[0], 1, 2 * d_ff)
    return pl.pallas_call(
        functools.partial(_ffn_up_kernel, seq_len=seq_len),
        grid=(m // tm, nj),
        in_specs=[
            pl.BlockSpec((tm, d), lambda i, j: (i, 0)),
            pl.BlockSpec((None, 6, d), lambda i, j: (cond_of(i), 0, 0)),
            pl.BlockSpec((None, 1, d), lambda i, j: (layer, 0, 0)),
            pl.BlockSpec((None, d, tn), lambda i, j: (layer, 0, j)),
            pl.BlockSpec((None, d, tn), lambda i, j: (layer, 0, nj + j)),
            pl.BlockSpec((None, 3, tn), lambda i, j: (layer, 0, j)),
            pl.BlockSpec((None, 3, tn), lambda i, j: (layer, 0, nj + j)),
            pl.BlockSpec((None, 1, tn), lambda i, j: (layer, 0, j)),
            pl.BlockSpec((None, 1, tn), lambda i, j: (layer, 0, nj + j)),
        ],
        out_specs=pl.BlockSpec((tm, tn), lambda i, j: (i, j)),
        out_shape=jax.ShapeDtypeStruct((m, d_ff), BF16),
        scratch_shapes=[pltpu.VMEM((tm, d), BF16)],
        compiler_params=_cparams("parallel", "arbitrary"),
        name="ffn_up",
    )(x, mod, norm_w, w_up, w_up, conv_w, conv_w, conv_b3, conv_b3)


def _proj_res_kernel(*refs, n_in, row):
    a_refs = refs[:n_in]
    w_refs = refs[n_in:2 * n_in]
    y_ref, mod_ref, o_ref = refs[2 * n_in:]
    acc = _dot(a_refs[0][...], w_refs[0][...])
    for a_ref, w_ref in zip(a_refs[1:], w_refs[1:]):
        acc += _dot(a_ref[...], w_ref[...])
    o_ref[...] = y_ref[...] + mod_ref[row:row + 1, :] * acc


def _proj_res(acts, w, w_idx, y, mod, mod_row0, rows_per_cond, *, row):
    m, d = y.shape
    tm, tn = ROW_TILE, COL_TILE
    n_in = len(acts)
    cond_of = lambda i: mod_row0 + (i * tm) // rows_per_cond
    in_specs = [pl.BlockSpec((tm, a.shape[1]), lambda i, j: (i, 0)) for a in acts]
    k_blk = acts[0].shape[1]
    assert all(a.shape[1] == k_blk for a in acts)
    for k in range(n_in):
        in_specs.append(pl.BlockSpec((None, k_blk, tn), lambda i, j, k=k: (w_idx, k, j)))
    in_specs += [
        pl.BlockSpec((tm, tn), lambda i, j: (i, j)),
        pl.BlockSpec((None, 6, tn), lambda i, j: (cond_of(i), 0, j)),
    ]
    return pl.pallas_call(
        functools.partial(_proj_res_kernel, n_in=n_in, row=row),
        grid=(m // tm, d // tn),
        in_specs=in_specs,
        out_specs=pl.BlockSpec((tm, tn), lambda i, j: (i, j)),
        out_shape=jax.ShapeDtypeStruct((m, d), F32),
        compiler_params=_cparams("parallel", "arbitrary"),
        name="proj_res",
    )(*acts, *([w] * n_in), y, mod)


def _rope_half_roll(x, cos2, sin2):
    return x * cos2 + pltpu.roll(x, x.shape[1] // 2, axis=1) * sin2


def _attn_kernel(*refs, n, n_ctx, rope, emit_kv, tq):
    q_ref, k_ref, v_ref, qw_ref, kw_ref = refs[:5]
    pos = 5
    if rope:
        cos_ref, sin_ref, ck_ref, cv_ref = refs[pos:pos + 4]
        pos += 4
    o_ref = refs[pos]
    pos += 1
    if emit_kv:
        nk_ref, nv_ref = refs[pos:pos + 2]
        pos += 2
    kall, vall = refs[pos:pos + 2]

    kn = _rms(k_ref[...].astype(F32), kw_ref[...])
    vv = v_ref[...]
    if emit_kv:
        nk_ref[...] = kn
        nv_ref[...] = vv.astype(F32)
    if rope:
        kn = _rope_half_roll(kn, cos_ref[...], sin_ref[...])
        kall[0:n_ctx, :] = ck_ref[...].astype(BF16)
        vall[0:n_ctx, :] = cv_ref[...].astype(BF16)
    kall[n_ctx:n_ctx + n, :] = kn.astype(BF16)
    vall[n_ctx:n_ctx + n, :] = vv.astype(BF16)

    scale = A_HEAD_DIM ** -0.5

    def chunk(c, carry):
        r0 = pl.multiple_of(c * tq, tq)
        rows = pl.ds(r0, tq)
        heads = [slice(g * A_HEAD_DIM, (g + 1) * A_HEAD_DIM) for g in range(A_GROUP)]
        qs = []
        for cols in heads:
            qh = _rms(q_ref[rows, cols].astype(F32), qw_ref[...])
            if rope:
                qh = _rope_half_roll(qh, cos_ref[rows, :], sin_ref[rows, :])
            qs.append((qh * scale).astype(BF16))
        ss = [_dot_nt(qh, kall[...]) for qh in qs]
        ps = [jnp.exp(s - jnp.max(s, axis=-1, keepdims=True)) for s in ss]
        ls = [jnp.sum(p, axis=-1, keepdims=True) for p in ps]
        os = [_dot(p.astype(BF16), vall[...]) / l for p, l in zip(ps, ls)]
        for cols, o in zip(heads, os):
            o_ref[rows, cols] = o.astype(o_ref.dtype)
        return carry

    lax.fori_loop(0, n // tq, chunk, 0, unroll=True)


def _attention(proj, n_seq, n, q_norm_w, k_norm_w, e, rope_tabs, cache_k, cache_v, emit_kv):
    rope = rope_tabs is not None
    n_ctx = cache_k.shape[2] if rope else 0
    hd = A_HEAD_DIM
    qcols = A_GROUP * hd
    in_specs = [
        pl.BlockSpec((n, qcols), lambda b, kv: (b, kv)),
        pl.BlockSpec((n, hd), lambda b, kv: (b, A_Q // hd + kv)),
        pl.BlockSpec((n, hd), lambda b, kv: (b, (A_Q + A_KV) // hd + kv)),
        pl.BlockSpec((None, 1, hd), lambda b, kv: (e, 0, 0)),
        pl.BlockSpec((None, 1, hd), lambda b, kv: (e, 0, 0)),
    ]
    args = [proj, proj, proj, q_norm_w, k_norm_w]
    if rope:
        cos2, sin2 = rope_tabs
        in_specs += [
            pl.BlockSpec((n, hd), lambda b, kv: (0, 0)),
            pl.BlockSpec((n, hd), lambda b, kv: (0, 0)),
            pl.BlockSpec((None, None, n_ctx, hd), lambda b, kv: (b, e, 0, kv)),
            pl.BlockSpec((None, None, n_ctx, hd), lambda b, kv: (b, e, 0, kv)),
        ]
        ck = cache_k.reshape(cache_k.shape[0], cache_k.shape[1], n_ctx, A_KV)
        cv = cache_v.reshape(cache_v.shape[0], cache_v.shape[1], n_ctx, A_KV)
        args += [cos2, sin2, ck, cv]
    out_specs = [pl.BlockSpec((n, qcols), lambda b, kv: (b, kv))]
    out_shape = [jax.ShapeDtypeStruct((n_seq * n, A_Q), BF16)]
    if emit_kv:
        out_specs += [pl.BlockSpec((n, hd), lambda b, kv: (b, kv))] * 2
        out_shape += [jax.ShapeDtypeStruct((n_seq * n, A_KV), F32)] * 2
    return pl.pallas_call(
        functools.partial(_attn_kernel, n=n, n_ctx=n_ctx, rope=rope, emit_kv=emit_kv,
                          tq=min(n, 256)),
        grid=(n_seq, A_KV_HEADS),
        in_specs=in_specs,
        out_specs=out_specs,
        out_shape=out_shape,
        scratch_shapes=[pltpu.VMEM((n_ctx + n, hd), BF16), pltpu.VMEM((n_ctx + n, hd), BF16)],
        compiler_params=_cparams("parallel", "arbitrary"),
        name="attention",
    )(*args)


def _gla_levels(c, rev):
    row = lax.broadcasted_iota(jnp.int32, (c, c), 0)
    col = lax.broadcasted_iota(jnp.int32, (c, c), 1)
    shift = HGRN_BASE.bit_length() - 1
    x = (row >> shift) ^ (col >> shift)
    lvl = jnp.zeros((c, c), jnp.int32)
    for l in range(1, (c // HGRN_BASE).bit_length()):
        lvl = jnp.where(x >= (1 << (l - 1)), l, lvl)
    causal = (col >= row) if rev else (col <= row)
    return jnp.where(causal, lvl, -1)


def _gla_blocks(chains):
    c, dk = chains[0]["q"].shape
    each = lambda fn: [fn(ch) for ch in chains]

    def split3(ch):
        hi = ch["lf"].astype(BF16)
        r1 = ch["lf"] - hi.astype(F32)
        mid = r1.astype(BF16)
        ch["parts"] = (hi, mid, (r1 - mid.astype(F32)).astype(BF16))

    def cumulate(ch):
        hi, mid, lo = ch["parts"]
        ch["b"] = _dot(ch["tri"], hi) + _dot(ch["tri"], mid) + _dot(ch["tri"], lo)
        ch["tot"] = ch["b"][0:1, :] if ch["rev"] else ch["b"][c - 1:c, :]

    def inter(ch):
        ch["o"] = _dot_nt((ch["q"] * jnp.exp(ch["b"])).astype(BF16), ch["st"].astype(BF16))
        ch["khat"] = (ch["k"] * jnp.exp(ch["tot"] - ch["b"])).astype(BF16)
        ch["vb"] = ch["v"].astype(BF16)

    def level0(ch):
        b3 = ch["b"].reshape(c // HGRN_BASE, HGRN_BASE, dk)
        mid_row = HGRN_BASE // 2 if ch["rev"] else HGRN_BASE // 2 - 1
        a = jnp.clip(b3 - b3[:, mid_row:mid_row + 1, :], -EXP_CLAMP, EXP_CLAMP).reshape(c, dk)
        p = _dot_nt((ch["q"] * jnp.exp(a)).astype(BF16), (ch["k"] * jnp.exp(-a)).astype(BF16))
        ch["scores"] = jnp.where(ch["lvl"] == 0, p, 0.0)

    def upper_level(ch, h, level):
        b3 = ch["b"].reshape(c // (2 * h), 2 * h, dk)
        ref_row = h if ch["rev"] else h - 1
        e = jnp.exp(-jnp.abs(b3 - b3[:, ref_row:ref_row + 1, :])).reshape(c, dk)
        p = _dot_nt((ch["q"] * e).astype(BF16), (ch["k"] * e).astype(BF16))
        ch["scores"] = jnp.where(ch["lvl"] == level, p, ch["scores"])

    def combine(ch):
        o = ch["o"] + _dot(ch["scores"].astype(BF16), ch["vb"])
        st_new = ch["st"] * jnp.exp(ch["tot"]) + _dot_tn(ch["vb"], ch["khat"])
        return o, st_new

    each(split3)
    each(cumulate)
    each(inter)
    each(level0)
    h, level = HGRN_BASE, 1
    while h < c:
        each(functools.partial(upper_level, h=h, level=level))
        h, level = 2 * h, level + 1
    return each(combine)


def _hgrn_kernel(*refs, n, has_state, emit_state):
    q_ref, i_ref, zf_ref, zb_ref, g_ref, lb_ref, nw_ref = refs[:7]
    pos = 7
    if has_state:
        s0f_ref, s0b_ref = refs[pos:pos + 2]
        pos += 2
    o_ref = refs[pos]
    pos += 1
    if emit_state:
        sf_ref, sb_ref = refs[pos:pos + 2]
        pos += 2
    of_scr, ob_scr, st_scr = refs[pos:pos + 3]

    c = MIX_BLOCK
    nb = n // c
    qscale = B_DK ** -0.5
    lvl_f = _gla_levels(c, False)
    lvl_b = _gla_levels(c, True)
    tri_f = jnp.where(lvl_f >= 0, 1.0, 0.0).astype(BF16)
    tri_b = jnp.where(lvl_b >= 0, 1.0, 0.0).astype(BF16)
    heads = [slice(hh * B_DK, (hh + 1) * B_DK) for hh in range(HGRN_HEADS_PER_STEP)]

    def chain(z_ref, rows, hh, rev):
        cols = heads[hh]
        lb = lb_ref[hh]
        f = lb + (1.0 - lb) * _sigmoid(z_ref[rows, cols].astype(F32))
        return dict(q=q_ref[rows, cols].astype(F32) * qscale, k=1.0 - f,
                    v=i_ref[rows, cols].astype(F32), lf=jnp.log(f),
                    st=st_scr[2 * hh + int(rev)], rev=rev,
                    lvl=lvl_b if rev else lvl_f, tri=tri_b if rev else tri_f)

    def step(j, carry):
        rows_f = pl.ds(pl.multiple_of(j * c, c), c)
        rows_b = pl.ds(pl.multiple_of((nb - 1 - j) * c, c), c)
        chains = []
        for hh in range(len(heads)):
            chains += [chain(zf_ref, rows_f, hh, False), chain(zb_ref, rows_b, hh, True)]
        outs = _gla_blocks(chains)
        for hh, cols in enumerate(heads):
            (o_f, st_f), (o_b, st_b) = outs[2 * hh], outs[2 * hh + 1]
            of_scr[rows_f, cols] = o_f
            ob_scr[rows_b, cols] = o_b
            st_scr[2 * hh] = st_f
            st_scr[2 * hh + 1] = st_b
        return carry

    def finish(j, carry):
        rows = pl.ds(pl.multiple_of(j * c, c), c)
        for cols in heads:
            y = _rms(of_scr[rows, cols] + ob_scr[rows, cols], nw_ref[...])
            g = g_ref[rows, cols].astype(F32)
            o_ref[rows, cols] = (y * (g * _sigmoid(g))).astype(o_ref.dtype)
        return carry

    for hh in range(len(heads)):
        if has_state:
            st_scr[2 * hh] = s0f_ref[hh].T
            st_scr[2 * hh + 1] = s0b_ref[hh].T
        else:
            st_scr[2 * hh] = jnp.zeros((B_DV, B_DK), F32)
            st_scr[2 * hh + 1] = jnp.zeros((B_DV, B_DK), F32)
    lax.fori_loop(0, nb, step, 0, unroll=2)
    lax.fori_loop(0, nb, finish, 0)
    if emit_state:
        for hh in range(len(heads)):
            sf_ref[hh] = st_scr[2 * hh].T
            sb_ref[hh] = st_scr[2 * hh + 1].T


def _hgrn(proj, n_seq, n, lb, o_norm_w, e, state_f, state_b, emit_state):
    has_state = state_f is not None
    hp = HGRN_HEADS_PER_STEP
    d = B_DK * hp
    base = (A_Q + 2 * A_KV) // d
    col = lambda k: (lambda b, h: (b, base + k * (B_HEADS // hp) + h))
    in_specs = [pl.BlockSpec((n, d), col(k)) for k in range(5)]
    in_specs += [
        pl.BlockSpec((hp, 1, B_DK), lambda b, h: (h, 0, 0)),
        pl.BlockSpec((None, 1, B_DV), lambda b, h: (e, 0, 0)),
    ]
    args = [proj] * 5 + [lb, o_norm_w]
    if has_state:
        st_spec = pl.BlockSpec((None, None, hp, B_DK, B_DV), lambda b, h: (b, e, h, 0, 0))
        in_specs += [st_spec, st_spec]
        args += [state_f, state_b]
    out_specs = [pl.BlockSpec((n, d), lambda b, h: (b, h))]
    out_shape = [jax.ShapeDtypeStruct((n_seq * n, B_V), BF16)]
    if emit_state:
        so = pl.BlockSpec((None, hp, B_DK, B_DV), lambda b, h: (b, h, 0, 0))
        out_specs += [so, so]
        out_shape += [jax.ShapeDtypeStruct((n_seq, B_HEADS, B_DK, B_DV), F32)] * 2
    return pl.pallas_call(
        functools.partial(_hgrn_kernel, n=n, has_state=has_state, emit_state=emit_state),
        grid=(n_seq, B_HEADS // hp),
        in_specs=in_specs,
        out_specs=out_specs,
        out_shape=out_shape,
        scratch_shapes=[pltpu.VMEM((n, d), F32), pltpu.VMEM((n, d), F32),
                        pltpu.VMEM((2 * hp, B_DV, B_DK), F32)],
        compiler_params=_cparams("parallel", "arbitrary"),
        name="hgrn2",
    )(*args)


def _rope_split(x, cos, sin):
    half = x.shape[1] // 2
    x1, x2 = x[:, :half], x[:, half:]
    return jnp.concatenate([x1 * cos - x2 * sin, x1 * sin + x2 * cos], axis=1)


def _ret_kernel(*refs, n, rope, has_state, emit_state):
    q_ref, k_ref, v_ref, g_ref, lgf_ref, lgb_ref, nw_ref = refs[:7]
    pos = 7
    if rope:
        cos_ref, sin_ref = refs[pos:pos + 2]
        pos += 2
    if has_state:
        s0f_ref, s0b_ref = refs[pos:pos + 2]
        pos += 2
    o_ref = refs[pos]
    pos += 1
    if emit_state:
        sf_ref, sb_ref = refs[pos:pos + 2]
        pos += 2
    qs_scr, ks_scr, of_scr, ob_scr, stf_scr, stb_scr, dm_scr, qd_scr, kd_scr = refs[pos:pos + 9]

    c = min(RET_BLOCK, n)
    nb = n // c
    skip_inter = (not has_state) and nb == 1
    kscale = C_DK ** -0.5
    rowi = lax.broadcasted_iota(jnp.int32, (c, c), 0)
    coli = lax.broadcasted_iota(jnp.int32, (c, c), 1)
    rowq = lax.broadcasted_iota(jnp.int32, (c, C_DK), 0).astype(F32)

    def prep(j, carry):
        rows = pl.ds(pl.multiple_of(j * MIX_BLOCK, MIX_BLOCK), MIX_BLOCK)
        q = q_ref[rows, :].astype(F32)
        k = k_ref[rows, :].astype(F32)
        if rope:
            q = _rope_split(q, cos_ref[rows, :], sin_ref[rows, :])
            k = _rope_split(k, cos_ref[rows, :], sin_ref[rows, :])
        qs_scr[rows, :] = q
        ks_scr[rows, :] = k * kscale
        return carry

    @pl.when(pl.program_id(1) == 0)
    def _():
        lgs = (lgf_ref[...], lgb_ref[...])
        dist = (rowi - coli).astype(F32)
        dm_scr[...] = (
            jnp.where(dist >= 0.0, jnp.exp(lgs[0][:, :c] * jnp.maximum(dist, 0.0)), 0.0)
            + jnp.where(dist <= 0.0, jnp.exp(lgs[1][:, :c] * jnp.maximum(-dist, 0.0)), 0.0))
        for d, rev in enumerate((False, True)):
            lgq = lgs[d][:, :C_DK]
            qd_scr[d] = jnp.exp(lgq * ((c - rowq) if rev else (rowq + 1.0)))
            kd_scr[d] = jnp.exp(lgq * (rowq if rev else (c - 1.0 - rowq)))

    def sweep_block(rows, d, lg_ref, st_scr, o_scr):
        vb = v_ref[rows, :]
        u = _dot_tn((ks_scr[rows, :] * kd_scr[d]).astype(BF16), vb)
        if skip_inter:
            st_scr[...] = u
        else:
            st = st_scr[...]
            o_scr[rows, :] = _dot((qs_scr[rows, :] * qd_scr[d]).astype(BF16), st.astype(BF16))
            st_scr[...] = st * jnp.exp(lg_ref[...] * float(c)) + u

    def sweep(j, carry):
        rows_f = pl.ds(pl.multiple_of(j * c, c), c)
        rows_b = pl.ds(pl.multiple_of((nb - 1 - j) * c, c), c)
        sweep_block(rows_f, 0, lgf_ref, stf_scr, of_scr)
        sweep_block(rows_b, 1, lgb_ref, stb_scr, ob_scr)
        return carry

    def finish(j, carry):
        keys = pl.ds(pl.multiple_of(j * c, c), c)
        kb = ks_scr[keys, :].astype(BF16)
        vb = v_ref[keys, :]
        pieces = [(pl.ds(pl.multiple_of(j * c + p * MIX_BLOCK, MIX_BLOCK), MIX_BLOCK),
                   slice(p * MIX_BLOCK, (p + 1) * MIX_BLOCK)) for p in range(c // MIX_BLOCK)]
        ss = [_dot_nt(qs_scr[rows, :].astype(BF16), kb) * dm_scr[within, :]
              for rows, within in pieces]
        os = [_dot(s.astype(BF16), vb) for s in ss]
        if not skip_inter:
            os = [o + of_scr[rows, :] + ob_scr[rows, :] for o, (rows, _) in zip(os, pieces)]
        ys = [_rms(o, nw_ref[...]) for o in os]
        for y, (rows, _) in zip(ys, pieces):
            g = g_ref[rows, :].astype(F32)
            o_ref[rows, :] = (y * (g * _sigmoid(g))).astype(o_ref.dtype)
        return carry

    lax.fori_loop(0, n // MIX_BLOCK, prep, 0)
    if has_state:
        stf_scr[...] = s0f_ref[...]
        stb_scr[...] = s0b_ref[...]
    elif not skip_inter:
        stf_scr[...] = jnp.zeros((C_DK, C_DV), F32)
        stb_scr[...] = jnp.zeros((C_DK, C_DV), F32)
    if emit_state or not skip_inter:
        lax.fori_loop(0, nb, sweep, 0)
    lax.fori_loop(0, nb, finish, 0, unroll=True)
    if emit_state:
        sf_ref[...] = stf_scr[...]
        sb_ref[...] = stb_scr[...]


def _retention(proj, n_seq, n, lg_f, lg_b, o_norm_w, o_idx, rope_tabs, state_f, state_b,
               emit_state):
    rope = rope_tabs is not None
    has_state = state_f is not None
    nq = C_QK // C_DK
    c = min(RET_BLOCK, n)
    in_specs = [
        pl.BlockSpec((n, C_DK), lambda h, b: (b, h)),
        pl.BlockSpec((n, C_DK), lambda h, b: (b, nq + h)),
        pl.BlockSpec((n, C_DV), lambda h, b: (b, 2 * C_QK // C_DV + h)),
        pl.BlockSpec((n, C_DV), lambda h, b: (b, (2 * C_QK + C_V) // C_DV + h)),
        pl.BlockSpec((None, 1, C_DV), lambda h, b: (h, 0, 0)),
        pl.BlockSpec((None, 1, C_DV), lambda h, b: (h, 0, 0)),
        pl.BlockSpec((None, 1, C_DV), lambda h, b: (o_idx, 0, 0)),
    ]
    args = [proj] * 4 + [lg_f, lg_b, o_norm_w]
    if rope:
        in_specs += [pl.BlockSpec((n, C_DK // 2), lambda h, b: (0, 0))] * 2
        args += list(rope_tabs)
    if has_state:
        st_spec = pl.BlockSpec((None, None, None, C_DK, C_DV), lambda h, b: (b, o_idx, h, 0, 0))
        in_specs += [st_spec, st_spec]
        args += [state_f, state_b]
    out_specs = [pl.BlockSpec((n, C_DV), lambda h, b: (b, h))]
    out_shape = [jax.ShapeDtypeStruct((n_seq * n, C_V), BF16)]
    if emit_state:
        so = pl.BlockSpec((None, None, C_DK, C_DV), lambda h, b: (b, h, 0, 0))
        out_specs += [so, so]
        out_shape += [jax.ShapeDtypeStruct((n_seq, C_HEADS, C_DK, C_DV), F32)] * 2
    return pl.pallas_call(
        functools.partial(_ret_kernel, n=n, rope=rope, has_state=has_state,
                          emit_state=emit_state),
        grid=(C_HEADS, n_seq),
        in_specs=in_specs,
        out_specs=out_specs,
        out_shape=out_shape,
        scratch_shapes=[pltpu.VMEM((n, C_DK), F32), pltpu.VMEM((n, C_DK), F32),
                        pltpu.VMEM((n, C_DV), F32), pltpu.VMEM((n, C_DV), F32),
                        pltpu.VMEM((C_DK, C_DV), F32), pltpu.VMEM((C_DK, C_DV), F32),
                        pltpu.VMEM((c, c), F32), pltpu.VMEM((2, c, C_DK), F32),
                        pltpu.VMEM((2, c, C_DK), F32)],
        compiler_params=_cparams("arbitrary", "arbitrary"),
        name="retention",
    )(*args)


def _rope_tables(n_tokens, head_dim):
    rows = n_tokens // GRID_W
    row = jnp.repeat(jnp.arange(rows, dtype=F32), GRID_W)
    col = jnp.tile(jnp.arange(GRID_W, dtype=F32), rows)
    quarter = head_dim // 4
    inv_freq = jnp.power(ROPE_BASE, -jnp.arange(quarter, dtype=F32) / quarter)
    ang = jnp.concatenate([row[:, None] * inv_freq, col[:, None] * inv_freq], axis=-1)
    return jnp.cos(ang), jnp.sin(ang)


def kernel(x_prompt, x_sample, cache_attn_k, cache_attn_v, state_hgrn_fwd, state_hgrn_bwd,
           state_ret_fwd, state_ret_bwd, c, c_ctx, w_mod, b_mod, norm_mix_w, norm_ffn_w,
           w_in_even, w_out_even, attn_q_norm_w, attn_k_norm_w, hgrn_lb, hgrn_o_norm_w,
           w_in_odd, w_out_odd, ret_decay_fwd, ret_decay_bwd, ret_o_norm_w,
           w_up, conv_w, conv_b, w_down):
    depth, d_model = norm_mix_w.shape
    bp, np_, _ = x_prompt.shape
    bs, ns, _ = x_sample.shape

    lb_all = jnp.cumsum(jax.nn.softmax(hgrn_lb.astype(F32), axis=0), axis=0)
    lg_f = jnp.broadcast_to(jax.nn.log_sigmoid(ret_decay_fwd.astype(F32))[:, :, None, None],
                            ret_decay_fwd.shape + (1, C_DV))
    lg_b = jnp.broadcast_to(jax.nn.log_sigmoid(ret_decay_bwd.astype(F32))[:, :, None, None],
                            ret_decay_bwd.shape + (1, C_DV))
    cos_a, sin_a = _rope_tables(ns, A_HEAD_DIM)
    rope_a = (jnp.concatenate([cos_a, cos_a], axis=1), jnp.concatenate([-sin_a, sin_a], axis=1))
    rope_c = _rope_tables(ns, C_DK)
    nmw = norm_mix_w.reshape(depth, 1, d_model)
    nfw = norm_ffn_w.reshape(depth, 1, d_model)
    qnw = attn_q_norm_w.reshape(-1, 1, A_HEAD_DIM)
    knw = attn_k_norm_w.reshape(-1, 1, A_HEAD_DIM)
    hnw = hgrn_o_norm_w.reshape(-1, 1, B_DV)
    rnw = ret_o_norm_w.reshape(-1, 1, C_DV)
    w_in_even, w_out_even, w_in_odd, w_out_odd, w_up, w_down = (
        w.astype(BF16) for w in (w_in_even, w_out_even, w_in_odd, w_out_odd, w_up, w_down))
    n_cond = 1 + bs
    pad = (-n_cond) % 8
    cond = jnp.concatenate([c_ctx[None, :], c, jnp.zeros((pad, d_model), F32)], axis=0)
    mod_all = _modulation(cond, w_mod, b_mod).reshape(depth, n_cond + pad, 6, d_model)

    groups = (
        dict(x=x_prompt.reshape(bp * np_, d_model), n_seq=bp, n=np_, row0=0,
             rows_per_cond=bp * np_, latent=False),
        dict(x=x_sample.reshape(bs * ns, d_model), n_seq=bs, n=ns, row0=1,
             rows_per_cond=ns, latent=True),
    )
    results = []
    for grp in groups:
        y, n_seq, n = grp["x"], grp["n_seq"], grp["n"]
        row0, rpc, latent = grp["row0"], grp["rows_per_cond"], grp["latent"]
        new = dict(k=[], v=[], hf=[], hb=[], rf=[], rb=[])
        for l in range(depth):
            mod = mod_all[l]
            if l % 2 == 0:
                e = l // 2
                proj = _norm_proj(y, mod, row0, rpc, nmw, l, w_in_even, e, row=0, out_dtype=BF16)
                att = _attention(proj, n_seq, n, qnw, knw, e, rope_a if latent else None,
                                 cache_attn_k, cache_attn_v, emit_kv=not latent)
                lb = lb_all[e].reshape(B_HEADS, 1, B_DK)
                hg = _hgrn(proj, n_seq, n, lb, hnw, e,
                           state_hgrn_fwd if latent else None,
                           state_hgrn_bwd if latent else None, emit_state=not latent)
                if not latent:
                    new["k"].append(att[1].reshape(n_seq, n, A_KV_HEADS, A_HEAD_DIM))
                    new["v"].append(att[2].reshape(n_seq, n, A_KV_HEADS, A_HEAD_DIM))
                    new["hf"].append(hg[1])
                    new["hb"].append(hg[2])
                y = _proj_res([att[0], hg[0]], w_out_even, e, y, mod, row0, rpc, row=2)
            else:
                o = l // 2
                proj = _norm_proj(y, mod, row0, rpc, nmw, l, w_in_odd, o, row=0, out_dtype=BF16)
                rt = _retention(proj, n_seq, n, lg_f[o], lg_b[o], rnw, o,
                                rope_c if latent else None,
                                state_ret_fwd if latent else None,
                                state_ret_bwd if latent else None, emit_state=not latent)
                if not latent:
                    new["rf"].append(rt[1])
                    new["rb"].append(rt[2])
                y = _proj_res([rt[0]], w_out_odd, o, y, mod, row0, rpc, row=2)
            act = _ffn_up(y, mod, row0, rpc, nfw, l, w_up, conv_w, conv_b, n)
            y = _proj_res([act], w_down, l, y, mod, row0, rpc, row=5)
        results.append((y.reshape(n_seq, n, d_model), new))

    (y_p, new), (y_s, _) = results
    stack = lambda xs: jnp.stack(xs, axis=1)
    return (y_p, y_s, stack(new["k"]), stack(new["v"]), stack(new["hf"]), stack(new["hb"]),
            stack(new["rf"]), stack(new["rb"]))
```

```python
import functools
from typing import NamedTuple

import jax
import jax.numpy as jnp
from jax import lax
from jax.experimental import pallas as pl
from jax.experimental.pallas import tpu as pltpu

F32 = jnp.float32
BF16 = jnp.bfloat16

GRID_W = 64
A_HEADS = 8
A_KV_HEADS = 2
A_HEAD_DIM = 128
ROPE_BASE = 10000.0
B_HEADS = 8
B_DK = 128
B_DV = 128
C_HEADS = 8
C_DK = 256
C_DV = 512
EPS = 1e-6

A_Q = A_HEADS * A_HEAD_DIM
A_KV = A_KV_HEADS * A_HEAD_DIM
A_GROUP = A_HEADS // A_KV_HEADS
B_QK = B_HEADS * B_DK
B_V = B_HEADS * B_DV
C_QK = C_HEADS * C_DK
C_V = C_HEADS * C_DV

VMEM_LIMIT_BYTES = 56 * 1024 * 1024
ROW_TILE = 1024
COL_TILE = 512
MIX_BLOCK = 128
RET_BLOCK = 256
NORM_ROWS = 16
HGRN_BASE = 16
HGRN_HEADS_PER_STEP = 4
EXP_CLAMP = 80.0


def _cparams(*sem):
    return pltpu.CompilerParams(dimension_semantics=sem, vmem_limit_bytes=VMEM_LIMIT_BYTES)


def _dot(a, b):
    return jnp.dot(a, b, preferred_element_type=F32)


def _dot_nt(a, b):
    return lax.dot_general(a, b, (((1,), (1,)), ((), ())), preferred_element_type=F32)


def _dot_tn(a, b):
    return lax.dot_general(a, b, (((0,), (0,)), ((), ())), preferred_element_type=F32)


def _sigmoid(x):
    return 1.0 / (1.0 + jnp.exp(-x))


def _rms(x, w):
    return x * lax.rsqrt(jnp.mean(x * x, axis=-1, keepdims=True) + EPS) * w


class _CastJob(NamedTuple):
    src: jax.Array
    lead: int


def _cast_job_specs(jobs, grid):
    n_steps = grid[0] * grid[1]
    step = lambda i, j: i * grid[1] + j
    in_specs, out_specs, out_shape = [], [], []
    for job in jobs:
        _, r, c = job.src.shape
        n_blocks = max(nb for nb in range(1, n_steps + 1)
                       if n_steps % nb == 0 and r % nb == 0 and (r // nb) % 16 == 0)
        rows, rep = r // n_blocks, n_steps // n_blocks
        in_specs.append(pl.BlockSpec((None, rows, c),
                                     lambda i, j, job=job, rep=rep: (job.lead, step(i, j) // rep, 0)))
        out_specs.append(pl.BlockSpec((rows, c), lambda i, j, rep=rep: (step(i, j) // rep, 0)))
        out_shape.append(jax.ShapeDtypeStruct((r, c), BF16))
    return in_specs, out_specs, out_shape


def _run_cast_jobs(src_refs, dst_refs):
    for src_ref, dst_ref in zip(src_refs, dst_refs):
        dst_ref[...] = src_ref[...].astype(BF16)


def _mod_kernel(c_ref, w_ref, b_ref, o_ref):
    c = c_ref[...]
    s = (c * _sigmoid(c)).astype(BF16)
    o_ref[...] = _dot(s, w_ref[...].astype(BF16)) + b_ref[...]


def _modulation(cond, w_mod, b_mod):
    n_layers, d, n_out = w_mod.shape
    r = cond.shape[0]
    tn = 1024
    return pl.pallas_call(
        _mod_kernel,
        grid=(n_layers, n_out // tn),
        in_specs=[
            pl.BlockSpec((r, d), lambda l, j: (0, 0)),
            pl.BlockSpec((None, d, tn), lambda l, j: (l, 0, j)),
            pl.BlockSpec((None, 1, tn), lambda l, j: (l, 0, j)),
        ],
        out_specs=pl.BlockSpec((None, r, tn), lambda l, j: (l, 0, j)),
        out_shape=jax.ShapeDtypeStruct((n_layers, r, n_out), F32),
        compiler_params=_cparams("arbitrary", "arbitrary"),
        name="modulation",
    )(cond, w_mod, b_mod.reshape(n_layers, 1, n_out))


def _norm_mod(x_ref, mod_ref, nw_ref, h_scr, row):
    gain = nw_ref[...] * (1.0 + mod_ref[row + 1:row + 2, :])
    shift = mod_ref[row:row + 1, :]

    def chunk(c, carry):
        rows = pl.ds(pl.multiple_of(c * NORM_ROWS, NORM_ROWS), NORM_ROWS)
        x = x_ref[rows, :]
        r = lax.rsqrt(jnp.mean(x * x, axis=-1, keepdims=True) + EPS)
        h_scr[rows, :] = (x * r * gain + shift).astype(BF16)
        return carry

    lax.fori_loop(0, x_ref.shape[0] // NORM_ROWS, chunk, 0, unroll=4)


def _norm_proj_kernel(x_ref, mod_ref, nw_ref, w_ref, o_ref, h_scr, *, row):
    @pl.when(pl.program_id(1) == 0)
    def _():
        _norm_mod(x_ref, mod_ref, nw_ref, h_scr, row)

    o_ref[...] = _dot(h_scr[...], w_ref[...]).astype(o_ref.dtype)


def _norm_proj(x, mod, mod_row0, rows_per_cond, norm_w, layer, w, w_idx, *, row, out_dtype):
    m, d = x.shape
    n_out = w.shape[-1]
    tm = ROW_TILE
    tn = 2 * COL_TILE if n_out % (2 * COL_TILE) == 0 else COL_TILE
    cond_of = lambda i: mod_row0 + (i * tm) // rows_per_cond
    return pl.pallas_call(
        functools.partial(_norm_proj_kernel, row=row),
        grid=(m // tm, n_out // tn),
        in_specs=[
            pl.BlockSpec((tm, d), lambda i, j: (i, 0)),
            pl.BlockSpec((None, 6, d), lambda i, j: (cond_of(i), 0, 0)),
            pl.BlockSpec((None, 1, d), lambda i, j: (layer, 0, 0)),
            pl.BlockSpec((None, d, tn), lambda i, j: (w_idx, 0, j)),
        ],
        out_specs=pl.BlockSpec((tm, tn), lambda i, j: (i, j)),
        out_shape=jax.ShapeDtypeStruct((m, n_out), out_dtype),
        scratch_shapes=[pltpu.VMEM((tm, d), BF16)],
        compiler_params=_cparams("parallel", "arbitrary"),
        name="norm_proj",
    )(x, mod, norm_w, w)


def _conv3(u, cw, cb, first, last):
    t = u.shape[0]
    left = jnp.where(first, 0.0, pltpu.roll(u, 1, axis=0))
    right = jnp.where(last, 0.0, pltpu.roll(u, t - 1, axis=0))
    return left * cw[0:1, :] + u * cw[1:2, :] + right * cw[2:3, :] + cb


def _ffn_up_kernel(x_ref, mod_ref, nw_ref, wa_ref, wv_ref, cwa_ref, cwv_ref, cba_ref, cbv_ref,
                   o_ref, h_scr, *, seq_len):
    @pl.when(pl.program_id(1) == 0)
    def _():
        _norm_mod(x_ref, mod_ref, nw_ref, h_scr, 3)

    h = h_scr[...]
    tm = h.shape[0]
    pos = lax.broadcasted_iota(jnp.int32, (tm, 1), 0) % seq_len
    first = pos == 0
    last = pos == seq_len - 1
    a = _conv3(_dot(h, wa_ref[...]), cwa_ref[...], cba_ref[...], first, last)
    v = _conv3(_dot(h, wv_ref[...]), cwv_ref[...], cbv_ref[...], first, last)
    o_ref[...] = (a * _sigmoid(a) * v).astype(o_ref.dtype)


def _ffn_up(x, mod, mod_row0, rows_per_cond, norm_w, layer, w_up, w_idx, conv_w, conv_b,
            seq_len):
    m, d = x.shape
    d_ff = w_up.shape[-1] // 2
    tm, tn = ROW_TILE, COL_TILE
    nj = d_ff // tn
    cond_of = lambda i: mod_row0 + (i * tm) // rows_per_cond
    conv_b3 = conv_b.reshape(conv_b.shape[0], 1, 2 * d_ff)
    return pl.pallas_call(
        functools.partial(_ffn_up_kernel, seq_len=seq_len),
        grid=(m // tm, nj),
        in_specs=[
            pl.BlockSpec((tm, d), lambda i, j: (i, 0)),
            pl.BlockSpec((None, 6, d), lambda i, j: (cond_of(i), 0, 0)),
            pl.BlockSpec((None, 1, d), lambda i, j: (layer, 0, 0)),
            pl.BlockSpec((None, d, tn), lambda i, j: (w_idx, 0, j)),
            pl.BlockSpec((None, d, tn), lambda i, j: (w_idx, 0, nj + j)),
            pl.BlockSpec((None, 3, tn), lambda i, j: (layer, 0, j)),
            pl.BlockSpec((None, 3, tn), lambda i, j: (layer, 0, nj + j)),
            pl.BlockSpec((None, 1, tn), lambda i, j: (layer, 0, j)),
            pl.BlockSpec((None, 1, tn), lambda i, j: (layer, 0, nj + j)),
        ],
        out_specs=pl.BlockSpec((tm, tn), lambda i, j: (i, j)),
        out_shape=jax.ShapeDtypeStruct((m, d_ff), BF16),
        scratch_shapes=[pltpu.VMEM((tm, d), BF16)],
        compiler_params=_cparams("parallel", "arbitrary"),
        name="ffn_up",
    )(x, mod, norm_w, w_up, w_up, conv_w, conv_w, conv_b3, conv_b3)


def _proj_res_kernel(*refs, n_in, row):
    a_refs = refs[:n_in]
    w_refs = refs[n_in:2 * n_in]
    y_ref, mod_ref, o_ref = refs[2 * n_in:]
    acc = _dot(a_refs[0][...], w_refs[0][...])
    for a_ref, w_ref in zip(a_refs[1:], w_refs[1:]):
        acc += _dot(a_ref[...], w_ref[...])
    o_ref[...] = y_ref[...] + mod_ref[row:row + 1, :] * acc


def _proj_res(acts, w, w_idx, y, mod, mod_row0, rows_per_cond, *, row):
    m, d = y.shape
    tm, tn = ROW_TILE, COL_TILE
    n_in = len(acts)
    cond_of = lambda i: mod_row0 + (i * tm) // rows_per_cond
    in_specs = [pl.BlockSpec((tm, a.shape[1]), lambda i, j: (i, 0)) for a in acts]
    k_blk = acts[0].shape[1]
    assert all(a.shape[1] == k_blk for a in acts)
    for k in range(n_in):
        in_specs.append(pl.BlockSpec((None, k_blk, tn), lambda i, j, k=k: (w_idx, k, j)))
    in_specs += [
        pl.BlockSpec((tm, tn), lambda i, j: (i, j)),
        pl.BlockSpec((None, 6, tn), lambda i, j: (cond_of(i), 0, j)),
    ]
    return pl.pallas_call(
        functools.partial(_proj_res_kernel, n_in=n_in, row=row),
        grid=(m // tm, d // tn),
        in_specs=in_specs,
        out_specs=pl.BlockSpec((tm, tn), lambda i, j: (i, j)),
        out_shape=jax.ShapeDtypeStruct((m, d), F32),
        compiler_params=_cparams("parallel", "arbitrary"),
        name="proj_res",
    )(*acts, *([w] * n_in), y, mod)


def _rope_half_roll(x, cos2, sin2):
    return x * cos2 + pltpu.roll(x, x.shape[1] // 2, axis=1) * sin2


def _attn_kernel(*refs, n, n_ctx, rope, emit_kv, tq, n_jobs):
    q_ref, k_ref, v_ref, qw_ref, kw_ref = refs[:5]
    pos = 5
    if rope:
        cos_ref, sin_ref, ck_ref, cv_ref = refs[pos:pos + 4]
        pos += 4
    job_src = refs[pos:pos + n_jobs]
    pos += n_jobs
    o_ref = refs[pos]
    pos += 1
    if emit_kv:
        nk_ref, nv_ref = refs[pos:pos + 2]
        pos += 2
    job_dst = refs[pos:pos + n_jobs]
    pos += n_jobs
    kall, vall = refs[pos:pos + 2]
    _run_cast_jobs(job_src, job_dst)

    kn = _rms(k_ref[...].astype(F32), kw_ref[...])
    vv = v_ref[...]
    if emit_kv:
        nk_ref[...] = kn
        nv_ref[...] = vv.astype(F32)
    if rope:
        kn = _rope_half_roll(kn, cos_ref[...], sin_ref[...])
        kall[0:n_ctx, :] = ck_ref[...].astype(BF16)
        vall[0:n_ctx, :] = cv_ref[...].astype(BF16)
    kall[n_ctx:n_ctx + n, :] = kn.astype(BF16)
    vall[n_ctx:n_ctx + n, :] = vv.astype(BF16)

    scale = A_HEAD_DIM ** -0.5

    def chunk(c, carry):
        r0 = pl.multiple_of(c * tq, tq)
        rows = pl.ds(r0, tq)
        heads = [slice(g * A_HEAD_DIM, (g + 1) * A_HEAD_DIM) for g in range(A_GROUP)]
        qs = []
        for cols in heads:
            qh = _rms(q_ref[rows, cols].astype(F32), qw_ref[...])
            if rope:
                qh = _rope_half_roll(qh, cos_ref[rows, :], sin_ref[rows, :])
            qs.append((qh * scale).astype(BF16))
        ss = [_dot_nt(qh, kall[...]) for qh in qs]
        ps = [jnp.exp(s - jnp.max(s, axis=-1, keepdims=True)) for s in ss]
        ls = [jnp.sum(p, axis=-1, keepdims=True) for p in ps]
        os = [_dot(p.astype(BF16), vall[...]) / l for p, l in zip(ps, ls)]
        for cols, o in zip(heads, os):
            o_ref[rows, cols] = o.astype(o_ref.dtype)
        return carry

    lax.fori_loop(0, n // tq, chunk, 0, unroll=True)


def _attention(proj, n_seq, n, q_norm_w, k_norm_w, e, rope_tabs, cache_k, cache_v, emit_kv,
               cast_jobs=()):
    rope = rope_tabs is not None
    n_ctx = cache_k.shape[2] if rope else 0
    hd = A_HEAD_DIM
    qcols = A_GROUP * hd
    in_specs = [
        pl.BlockSpec((n, qcols), lambda b, kv: (b, kv)),
        pl.BlockSpec((n, hd), lambda b, kv: (b, A_Q // hd + kv)),
        pl.BlockSpec((n, hd), lambda b, kv: (b, (A_Q + A_KV) // hd + kv)),
        pl.BlockSpec((None, 1, hd), lambda b, kv: (e, 0, 0)),
        pl.BlockSpec((None, 1, hd), lambda b, kv: (e, 0, 0)),
    ]
    args = [proj, proj, proj, q_norm_w, k_norm_w]
    if rope:
        cos2, sin2 = rope_tabs
        in_specs += [
            pl.BlockSpec((n, hd), lambda b, kv: (0, 0)),
            pl.BlockSpec((n, hd), lambda b, kv: (0, 0)),
            pl.BlockSpec((None, None, n_ctx, hd), lambda b, kv: (b, e, 0, kv)),
            pl.BlockSpec((None, None, n_ctx, hd), lambda b, kv: (b, e, 0, kv)),
        ]
        ck = cache_k.reshape(cache_k.shape[0], cache_k.shape[1], n_ctx, A_KV)
        cv = cache_v.reshape(cache_v.shape[0], cache_v.shape[1], n_ctx, A_KV)
        args += [cos2, sin2, ck, cv]
    out_specs = [pl.BlockSpec((n, qcols), lambda b, kv: (b, kv))]
    out_shape = [jax.ShapeDtypeStruct((n_seq * n, A_Q), BF16)]
    if emit_kv:
        out_specs += [pl.BlockSpec((n, hd), lambda b, kv: (b, kv))] * 2
        out_shape += [jax.ShapeDtypeStruct((n_seq * n, A_KV), F32)] * 2
    grid = (n_seq, A_KV_HEADS)
    job_in, job_out, job_shape = _cast_job_specs(cast_jobs, grid)
    in_specs += job_in
    args += [job.src for job in cast_jobs]
    out_specs += job_out
    out_shape += job_shape
    return pl.pallas_call(
        functools.partial(_attn_kernel, n=n, n_ctx=n_ctx, rope=rope, emit_kv=emit_kv,
                          tq=min(n, 256), n_jobs=len(cast_jobs)),
        grid=grid,
        in_specs=in_specs,
        out_specs=out_specs,
        out_shape=out_shape,
        scratch_shapes=[pltpu.VMEM((n_ctx + n, hd), BF16), pltpu.VMEM((n_ctx + n, hd), BF16)],
        compiler_params=_cparams("parallel", "arbitrary"),
        name="attention",
    )(*args)


def _gla_levels(c, rev):
    row = lax.broadcasted_iota(jnp.int32, (c, c), 0)
    col = lax.broadcasted_iota(jnp.int32, (c, c), 1)
    shift = HGRN_BASE.bit_length() - 1
    x = (row >> shift) ^ (col >> shift)
    lvl = jnp.zeros((c, c), jnp.int32)
    for l in range(1, (c // HGRN_BASE).bit_length()):
        lvl = jnp.where(x >= (1 << (l - 1)), l, lvl)
    causal = (col >= row) if rev else (col <= row)
    return jnp.where(causal, lvl, -1)


def _gla_blocks(chains):
    c, dk = chains[0]["q"].shape
    each = lambda fn: [fn(ch) for ch in chains]

    def split3(ch):
        hi = ch["lf"].astype(BF16)
        r1 = ch["lf"] - hi.astype(F32)
        mid = r1.astype(BF16)
        ch["parts"] = (hi, mid, (r1 - mid.astype(F32)).astype(BF16))

    def cumulate(ch):
        hi, mid, lo = ch["parts"]
        ch["b"] = _dot(ch["tri"], hi) + _dot(ch["tri"], mid) + _dot(ch["tri"], lo)
        ch["tot"] = ch["b"][0:1, :] if ch["rev"] else ch["b"][c - 1:c, :]

    def inter(ch):
        ch["o"] = _dot_nt((ch["q"] * jnp.exp(ch["b"])).astype(BF16), ch["st"].astype(BF16))
        ch["khat"] = (ch["k"] * jnp.exp(ch["tot"] - ch["b"])).astype(BF16)
        ch["vb"] = ch["v"].astype(BF16)

    def level0(ch):
        b3 = ch["b"].reshape(c // HGRN_BASE, HGRN_BASE, dk)
        mid_row = HGRN_BASE // 2 if ch["rev"] else HGRN_BASE // 2 - 1
        a = jnp.clip(b3 - b3[:, mid_row:mid_row + 1, :], -EXP_CLAMP, EXP_CLAMP).reshape(c, dk)
        p = _dot_nt((ch["q"] * jnp.exp(a)).astype(BF16), (ch["k"] * jnp.exp(-a)).astype(BF16))
        ch["scores"] = jnp.where(ch["lvl"] == 0, p, 0.0)

    def upper_level(ch, h, level):
        b3 = ch["b"].reshape(c // (2 * h), 2 * h, dk)
        ref_row = h if ch["rev"] else h - 1
        e = jnp.exp(-jnp.abs(b3 - b3[:, ref_row:ref_row + 1, :])).reshape(c, dk)
        p = _dot_nt((ch["q"] * e).astype(BF16), (ch["k"] * e).astype(BF16))
        ch["scores"] = jnp.where(ch["lvl"] == level, p, ch["scores"])

    def combine(ch):
        o = ch["o"] + _dot(ch["scores"].astype(BF16), ch["vb"])
        st_new = ch["st"] * jnp.exp(ch["tot"]) + _dot_tn(ch["vb"], ch["khat"])
        return o, st_new

    each(split3)
    each(cumulate)
    each(inter)
    each(level0)
    h, level = HGRN_BASE, 1
    while h < c:
        each(functools.partial(upper_level, h=h, level=level))
        h, level = 2 * h, level + 1
    return each(combine)


def _hgrn_kernel(*refs, n, has_state, emit_state, n_jobs):
    q_ref, i_ref, zf_ref, zb_ref, g_ref, lb_ref, nw_ref = refs[:7]
    pos = 7
    if has_state:
        s0f_ref, s0b_ref = refs[pos:pos + 2]
        pos += 2
    job_src = refs[pos:pos + n_jobs]
    pos += n_jobs
    o_ref = refs[pos]
    pos += 1
    if emit_state:
        sf_ref, sb_ref = refs[pos:pos + 2]
        pos += 2
    job_dst = refs[pos:pos + n_jobs]
    pos += n_jobs
    of_scr, ob_scr, st_scr = refs[pos:pos + 3]
    _run_cast_jobs(job_src, job_dst)

    c = MIX_BLOCK
    nb = n // c
    qscale = B_DK ** -0.5
    lvl_f = _gla_levels(c, False)
    lvl_b = _gla_levels(c, True)
    tri_f = jnp.where(lvl_f >= 0, 1.0, 0.0).astype(BF16)
    tri_b = jnp.where(lvl_b >= 0, 1.0, 0.0).astype(BF16)
    heads = [slice(hh * B_DK, (hh + 1) * B_DK) for hh in range(HGRN_HEADS_PER_STEP)]

    def chain(z_ref, rows, hh, rev):
        cols = heads[hh]
        lb = lb_ref[hh]
        f = lb + (1.0 - lb) * _sigmoid(z_ref[rows, cols].astype(F32))
        return dict(q=q_ref[rows, cols].astype(F32) * qscale, k=1.0 - f,
                    v=i_ref[rows, cols].astype(F32), lf=jnp.log(f),
                    st=st_scr[2 * hh + int(rev)], rev=rev,
                    lvl=lvl_b if rev else lvl_f, tri=tri_b if rev else tri_f)

    def step(j, carry):
        rows_f = pl.ds(pl.multiple_of(j * c, c), c)
        rows_b = pl.ds(pl.multiple_of((nb - 1 - j) * c, c), c)
        chains = []
        for hh in range(len(heads)):
            chains += [chain(zf_ref, rows_f, hh, False), chain(zb_ref, rows_b, hh, True)]
        outs = _gla_blocks(chains)
        for hh, cols in enumerate(heads):
            (o_f, st_f), (o_b, st_b) = outs[2 * hh], outs[2 * hh + 1]
            of_scr[rows_f, cols] = o_f
            ob_scr[rows_b, cols] = o_b
            st_scr[2 * hh] = st_f
            st_scr[2 * hh + 1] = st_b
        return carry

    def finish(j, carry):
        rows = pl.ds(pl.multiple_of(j * c, c), c)
        for cols in heads:
            y = _rms(of_scr[rows, cols] + ob_scr[rows, cols], nw_ref[...])
            g = g_ref[rows, cols].astype(F32)
            o_ref[rows, cols] = (y * (g * _sigmoid(g))).astype(o_ref.dtype)
        return carry

    for hh in range(len(heads)):
        if has_state:
            st_scr[2 * hh] = s0f_ref[hh].T
            st_scr[2 * hh + 1] = s0b_ref[hh].T
        else:
            st_scr[2 * hh] = jnp.zeros((B_DV, B_DK), F32)
            st_scr[2 * hh + 1] = jnp.zeros((B_DV, B_DK), F32)
    lax.fori_loop(0, nb, step, 0, unroll=2)
    lax.fori_loop(0, nb, finish, 0)
    if emit_state:
        for hh in range(len(heads)):
            sf_ref[hh] = st_scr[2 * hh].T
            sb_ref[hh] = st_scr[2 * hh + 1].T


def _hgrn(proj, n_seq, n, lb, o_norm_w, e, state_f, state_b, emit_state, cast_jobs=()):
    has_state = state_f is not None
    hp = HGRN_HEADS_PER_STEP
    d = B_DK * hp
    base = (A_Q + 2 * A_KV) // d
    col = lambda k: (lambda b, h: (b, base + k * (B_HEADS // hp) + h))
    in_specs = [pl.BlockSpec((n, d), col(k)) for k in range(5)]
    in_specs += [
        pl.BlockSpec((hp, 1, B_DK), lambda b, h: (h, 0, 0)),
        pl.BlockSpec((None, 1, B_DV), lambda b, h: (e, 0, 0)),
    ]
    args = [proj] * 5 + [lb, o_norm_w]
    if has_state:
        st_spec = pl.BlockSpec((None, None, hp, B_DK, B_DV), lambda b, h: (b, e, h, 0, 0))
        in_specs += [st_spec, st_spec]
        args += [state_f, state_b]
    out_specs = [pl.BlockSpec((n, d), lambda b, h: (b, h))]
    out_shape = [jax.ShapeDtypeStruct((n_seq * n, B_V), BF16)]
    if emit_state:
        so = pl.BlockSpec((None, hp, B_DK, B_DV), lambda b, h: (b, h, 0, 0))
        out_specs += [so, so]
        out_shape += [jax.ShapeDtypeStruct((n_seq, B_HEADS, B_DK, B_DV), F32)] * 2
    grid = (n_seq, B_HEADS // hp)
    job_in, job_out, job_shape = _cast_job_specs(cast_jobs, grid)
    in_specs += job_in
    args += [job.src for job in cast_jobs]
    out_specs += job_out
    out_shape += job_shape
    return pl.pallas_call(
        functools.partial(_hgrn_kernel, n=n, has_state=has_state, emit_state=emit_state,
                          n_jobs=len(cast_jobs)),
        grid=grid,
        in_specs=in_specs,
        out_specs=out_specs,
        out_shape=out_shape,
        scratch_shapes=[pltpu.VMEM((n, d), F32), pltpu.VMEM((n, d), F32),
                        pltpu.VMEM((2 * hp, B_DV, B_DK), F32)],
        compiler_params=_cparams("parallel", "arbitrary"),
        name="hgrn2",
    )(*args)


def _rope_split(x, cos, sin):
    half = x.shape[1] // 2
    x1, x2 = x[:, :half], x[:, half:]
    return jnp.concatenate([x1 * cos - x2 * sin, x1 * sin + x2 * cos], axis=1)


def _ret_kernel(*refs, n, rope, has_state, emit_state, n_jobs):
    q_ref, k_ref, v_ref, g_ref, lgf_ref, lgb_ref, nw_ref = refs[:7]
    pos = 7
    if rope:
        cos_ref, sin_ref = refs[pos:pos + 2]
        pos += 2
    if has_state:
        s0f_ref, s0b_ref = refs[pos:pos + 2]
        pos += 2
    job_src = refs[pos:pos + n_jobs]
    pos += n_jobs
    o_ref = refs[pos]
    pos += 1
    if emit_state:
        sf_ref, sb_ref = refs[pos:pos + 2]
        pos += 2
    job_dst = refs[pos:pos + n_jobs]
    pos += n_jobs
    _run_cast_jobs(job_src, job_dst)
    qs_scr, ks_scr, of_scr, ob_scr, stf_scr, stb_scr, dm_scr, qd_scr, kd_scr = refs[pos:pos + 9]

    c = min(RET_BLOCK, n)
    nb = n // c
    skip_inter = (not has_state) and nb == 1
    kscale = C_DK ** -0.5
    rowi = lax.broadcasted_iota(jnp.int32, (c, c), 0)
    coli = lax.broadcasted_iota(jnp.int32, (c, c), 1)
    rowq = lax.broadcasted_iota(jnp.int32, (c, C_DK), 0).astype(F32)

    def prep(j, carry):
        rows = pl.ds(pl.multiple_of(j * MIX_BLOCK, MIX_BLOCK), MIX_BLOCK)
        q = q_ref[rows, :].astype(F32)
        k = k_ref[rows, :].astype(F32)
        if rope:
            q = _rope_split(q, cos_ref[rows, :], sin_ref[rows, :])
            k = _rope_split(k, cos_ref[rows, :], sin_ref[rows, :])
        qs_scr[rows, :] = q
        ks_scr[rows, :] = k * kscale
        return carry

    @pl.when(pl.program_id(1) == 0)
    def _():
        lgs = (lgf_ref[...], lgb_ref[...])
        dist = (rowi - coli).astype(F32)
        dm_scr[...] = (
            jnp.where(dist >= 0.0, jnp.exp(lgs[0][:, :c] * jnp.maximum(dist, 0.0)), 0.0)
            + jnp.where(dist <= 0.0, jnp.exp(lgs[1][:, :c] * jnp.maximum(-dist, 0.0)), 0.0))
        for d, rev in enumerate((False, True)):
            lgq = lgs[d][:, :C_DK]
            qd_scr[d] = jnp.exp(lgq * ((c - rowq) if rev else (rowq + 1.0)))
            kd_scr[d] = jnp.exp(lgq * (rowq if rev else (c - 1.0 - rowq)))

    def sweep_block(rows, d, lg_ref, st_scr, o_scr):
        vb = v_ref[rows, :]
        u = _dot_tn((ks_scr[rows, :] * kd_scr[d]).astype(BF16), vb)
        if skip_inter:
            st_scr[...] = u
        else:
            st = st_scr[...]
            o_scr[rows, :] = _dot((qs_scr[rows, :] * qd_scr[d]).astype(BF16), st.astype(BF16))
            st_scr[...] = st * jnp.exp(lg_ref[...] * float(c)) + u

    def sweep(j, carry):
        rows_f = pl.ds(pl.multiple_of(j * c, c), c)
        rows_b = pl.ds(pl.multiple_of((nb - 1 - j) * c, c), c)
        sweep_block(rows_f, 0, lgf_ref, stf_scr, of_scr)
        sweep_block(rows_b, 1, lgb_ref, stb_scr, ob_scr)
        return carry

    def finish(j, carry):
        keys = pl.ds(pl.multiple_of(j * c, c), c)
        kb = ks_scr[keys, :].astype(BF16)
        vb = v_ref[keys, :]
        pieces = [(pl.ds(pl.multiple_of(j * c + p * MIX_BLOCK, MIX_BLOCK), MIX_BLOCK),
                   slice(p * MIX_BLOCK, (p + 1) * MIX_BLOCK)) for p in range(c // MIX_BLOCK)]
        ss = [_dot_nt(qs_scr[rows, :].astype(BF16), kb) * dm_scr[within, :]
              for rows, within in pieces]
        os = [_dot(s.astype(BF16), vb) for s in ss]
        if not skip_inter:
            os = [o + of_scr[rows, :] + ob_scr[rows, :] for o, (rows, _) in zip(os, pieces)]
        ys = [_rms(o, nw_ref[...]) for o in os]
        for y, (rows, _) in zip(ys, pieces):
            g = g_ref[rows, :].astype(F32)
            o_ref[rows, :] = (y * (g * _sigmoid(g))).astype(o_ref.dtype)
        return carry

    lax.fori_loop(0, n // MIX_BLOCK, prep, 0)
    if has_state:
        stf_scr[...] = s0f_ref[...]
        stb_scr[...] = s0b_ref[...]
    elif not skip_inter:
        stf_scr[...] = jnp.zeros((C_DK, C_DV), F32)
        stb_scr[...] = jnp.zeros((C_DK, C_DV), F32)
    if emit_state or not skip_inter:
        lax.fori_loop(0, nb, sweep, 0, unroll=True)
    lax.fori_loop(0, nb, finish, 0, unroll=True)
    if emit_state:
        sf_ref[...] = stf_scr[...]
        sb_ref[...] = stb_scr[...]


def _retention(proj, n_seq, n, lg_f, lg_b, o_norm_w, o_idx, rope_tabs, state_f, state_b,
               emit_state, cast_jobs=()):
    rope = rope_tabs is not None
    has_state = state_f is not None
    nq = C_QK // C_DK
    c = min(RET_BLOCK, n)
    in_specs = [
        pl.BlockSpec((n, C_DK), lambda h, b: (b, h)),
        pl.BlockSpec((n, C_DK), lambda h, b: (b, nq + h)),
        pl.BlockSpec((n, C_DV), lambda h, b: (b, 2 * C_QK // C_DV + h)),
        pl.BlockSpec((n, C_DV), lambda h, b: (b, (2 * C_QK + C_V) // C_DV + h)),
        pl.BlockSpec((None, 1, C_DV), lambda h, b: (h, 0, 0)),
        pl.BlockSpec((None, 1, C_DV), lambda h, b: (h, 0, 0)),
        pl.BlockSpec((None, 1, C_DV), lambda h, b: (o_idx, 0, 0)),
    ]
    args = [proj] * 4 + [lg_f, lg_b, o_norm_w]
    if rope:
        in_specs += [pl.BlockSpec((n, C_DK // 2), lambda h, b: (0, 0))] * 2
        args += list(rope_tabs)
    if has_state:
        st_spec = pl.BlockSpec((None, None, None, C_DK, C_DV), lambda h, b: (b, o_idx, h, 0, 0))
        in_specs += [st_spec, st_spec]
        args += [state_f, state_b]
    out_specs = [pl.BlockSpec((n, C_DV), lambda h, b: (b, h))]
    out_shape = [jax.ShapeDtypeStruct((n_seq * n, C_V), BF16)]
    if emit_state:
        so = pl.BlockSpec((None, None, C_DK, C_DV), lambda h, b: (b, h, 0, 0))
        out_specs += [so, so]
        out_shape += [jax.ShapeDtypeStruct((n_seq, C_HEADS, C_DK, C_DV), F32)] * 2
    grid = (C_HEADS, n_seq)
    job_in, job_out, job_shape = _cast_job_specs(cast_jobs, grid)
    in_specs += job_in
    args += [job.src for job in cast_jobs]
    out_specs += job_out
    out_shape += job_shape
    return pl.pallas_call(
        functools.partial(_ret_kernel, n=n, rope=rope, has_state=has_state,
                          emit_state=emit_state, n_jobs=len(cast_jobs)),
        grid=grid,
        in_specs=in_specs,
        out_specs=out_specs,
        out_shape=out_shape,
        scratch_shapes=[pltpu.VMEM((n, C_DK), F32), pltpu.VMEM((n, C_DK), F32),
                        pltpu.VMEM((n, C_DV), F32), pltpu.VMEM((n, C_DV), F32),
                        pltpu.VMEM((C_DK, C_DV), F32), pltpu.VMEM((C_DK, C_DV), F32),
                        pltpu.VMEM((c, c), F32), pltpu.VMEM((2, c, C_DK), F32),
                        pltpu.VMEM((2, c, C_DK), F32)],
        compiler_params=_cparams("arbitrary", "arbitrary"),
        name="retention",
    )(*args)


def _rope_tables(n_tokens, head_dim):
    rows = n_tokens // GRID_W
    row = jnp.repeat(jnp.arange(rows, dtype=F32), GRID_W)
    col = jnp.tile(jnp.arange(GRID_W, dtype=F32), rows)
    quarter = head_dim // 4
    inv_freq = jnp.power(ROPE_BASE, -jnp.arange(quarter, dtype=F32) / quarter)
    ang = jnp.concatenate([row[:, None] * inv_freq, col[:, None] * inv_freq], axis=-1)
    return jnp.cos(ang), jnp.sin(ang)


def kernel(x_prompt, x_sample, cache_attn_k, cache_attn_v, state_hgrn_fwd, state_hgrn_bwd,
           state_ret_fwd, state_ret_bwd, c, c_ctx, w_mod, b_mod, norm_mix_w, norm_ffn_w,
           w_in_even, w_out_even, attn_q_norm_w, attn_k_norm_w, hgrn_lb, hgrn_o_norm_w,
           w_in_odd, w_out_odd, ret_decay_fwd, ret_decay_bwd, ret_o_norm_w,
           w_up, conv_w, conv_b, w_down):
    depth, d_model = norm_mix_w.shape
    bp, np_, _ = x_prompt.shape
    bs, ns, _ = x_sample.shape

    lb_all = jnp.cumsum(jax.nn.softmax(hgrn_lb.astype(F32), axis=0), axis=0)
    lg_f = jnp.broadcast_to(jax.nn.log_sigmoid(ret_decay_fwd.astype(F32))[:, :, None, None],
                            ret_decay_fwd.shape + (1, C_DV))
    lg_b = jnp.broadcast_to(jax.nn.log_sigmoid(ret_decay_bwd.astype(F32))[:, :, None, None],
                            ret_decay_bwd.shape + (1, C_DV))
    cos_a, sin_a = _rope_tables(ns, A_HEAD_DIM)
    rope_a = (jnp.concatenate([cos_a, cos_a], axis=1), jnp.concatenate([-sin_a, sin_a], axis=1))
    rope_c = _rope_tables(ns, C_DK)
    nmw = norm_mix_w.reshape(depth, 1, d_model)
    nfw = norm_ffn_w.reshape(depth, 1, d_model)
    qnw = attn_q_norm_w.reshape(-1, 1, A_HEAD_DIM)
    knw = attn_k_norm_w.reshape(-1, 1, A_HEAD_DIM)
    hnw = hgrn_o_norm_w.reshape(-1, 1, B_DV)
    rnw = ret_o_norm_w.reshape(-1, 1, C_DV)
    w_in_even, w_in_odd = w_in_even.astype(BF16), w_in_odd.astype(BF16)
    w_out, w_gate_val, w_down_b = {}, {}, {}
    n_cond = 1 + bs
    pad = (-n_cond) % 8
    cond = jnp.concatenate([c_ctx[None, :], c, jnp.zeros((pad, d_model), F32)], axis=0)
    mod_all = _modulation(cond, w_mod, b_mod).reshape(depth, n_cond + pad, 6, d_model)

    groups = (
        dict(x=x_prompt.reshape(bp * np_, d_model), n_seq=bp, n=np_, row0=0,
             rows_per_cond=bp * np_, latent=False),
        dict(x=x_sample.reshape(bs * ns, d_model), n_seq=bs, n=ns, row0=1,
             rows_per_cond=ns, latent=True),
    )
    results = []
    for grp in groups:
        y, n_seq, n = grp["x"], grp["n_seq"], grp["n"]
        row0, rpc, latent = grp["row0"], grp["rows_per_cond"], grp["latent"]
        new = dict(k=[], v=[], hf=[], hb=[], rf=[], rb=[])
        for l in range(depth):
            mod = mod_all[l]
            if l % 2 == 0:
                e = l // 2
                proj = _norm_proj(y, mod, row0, rpc, nmw, l, w_in_even, e, row=0, out_dtype=BF16)
                first = (l not in w_out)
                att = _attention(proj, n_seq, n, qnw, knw, e, rope_a if latent else None,
                                 cache_attn_k, cache_attn_v, emit_kv=not latent,
                                 cast_jobs=[_CastJob(w_out_even, e)] if first else ())
                lb = lb_all[e].reshape(B_HEADS, 1, B_DK)
                hg = _hgrn(proj, n_seq, n, lb, hnw, e,
                           state_hgrn_fwd if latent else None,
                           state_hgrn_bwd if latent else None, emit_state=not latent,
                           cast_jobs=[_CastJob(w_up, l), _CastJob(w_down, l)] if first else ())
                if first:
                    w_out[l], w_gate_val[l], w_down_b[l] = att[-1][None], hg[-2][None], hg[-1][None]
                if not latent:
                    new["k"].append(att[1].reshape(n_seq, n, A_KV_HEADS, A_HEAD_DIM))
                    new["v"].append(att[2].reshape(n_seq, n, A_KV_HEADS, A_HEAD_DIM))
                    new["hf"].append(hg[1])
                    new["hb"].append(hg[2])
                y = _proj_res([att[0], hg[0]], w_out[l], 0, y, mod, row0, rpc, row=2)
            else:
                o = l // 2
                proj = _norm_proj(y, mod, row0, rpc, nmw, l, w_in_odd, o, row=0, out_dtype=BF16)
                first = (l not in w_out)
                jobs = [_CastJob(w_out_odd, o), _CastJob(w_up, l), _CastJob(w_down, l)]
                rt = _retention(proj, n_seq, n, lg_f[o], lg_b[o], rnw, o,
                                rope_c if latent else None,
                                state_ret_fwd if latent else None,
                                state_ret_bwd if latent else None, emit_state=not latent,
                                cast_jobs=jobs if first else ())
                if first:
                    w_out[l], w_gate_val[l], w_down_b[l] = (w[None] for w in rt[-3:])
                if not latent:
                    new["rf"].append(rt[1])
                    new["rb"].append(rt[2])
                y = _proj_res([rt[0]], w_out[l], 0, y, mod, row0, rpc, row=2)
            act = _ffn_up(y, mod, row0, rpc, nfw, l, w_gate_val[l], 0, conv_w, conv_b, n)
            y = _proj_res([act], w_down_b[l], 0, y, mod, row0, rpc, row=5)
        results.append((y.reshape(n_seq, n, d_model), new))

    (y_p, new), (y_s, _) = results
    stack = lambda xs: jnp.stack(xs, axis=1)
    return (y_p, y_s, stack(new["k"]), stack(new["v"]), stack(new["hf"]), stack(new["hb"]),
            stack(new["rf"]), stack(new["rb"]))
```

```python
import functools
from typing import NamedTuple

import jax
import jax.numpy as jnp
from jax import lax
from jax.experimental import pallas as pl
from jax.experimental.pallas import tpu as pltpu

F32 = jnp.float32
BF16 = jnp.bfloat16

GRID_W = 64
A_HEADS = 8
A_KV_HEADS = 2
A_HEAD_DIM = 128
ROPE_BASE = 10000.0
B_HEADS = 8
B_DK = 128
B_DV = 128
C_HEADS = 8
C_DK = 256
C_DV = 512
EPS = 1e-6

A_Q = A_HEADS * A_HEAD_DIM
A_KV = A_KV_HEADS * A_HEAD_DIM
A_GROUP = A_HEADS // A_KV_HEADS
B_QK = B_HEADS * B_DK
B_V = B_HEADS * B_DV
C_QK = C_HEADS * C_DK
C_V = C_HEADS * C_DV

VMEM_LIMIT_BYTES = 56 * 1024 * 1024
ROW_TILE = 1024
COL_TILE = 512
MIX_BLOCK = 128
RET_BLOCK = 256
NORM_ROWS = 16
HGRN_BASE = 16
HGRN_HEADS_PER_STEP = 4
EXP_CLAMP = 80.0


def _cparams(*sem):
    return pltpu.CompilerParams(dimension_semantics=sem, vmem_limit_bytes=VMEM_LIMIT_BYTES)


def _dot(a, b):
    return jnp.dot(a, b, preferred_element_type=F32)


def _dot_nt(a, b):
    return lax.dot_general(a, b, (((1,), (1,)), ((), ())), preferred_element_type=F32)


def _dot_tn(a, b):
    return lax.dot_general(a, b, (((0,), (0,)), ((), ())), preferred_element_type=F32)


def _sigmoid(x):
    return 1.0 / (1.0 + jnp.exp(-x))


def _rms(x, w):
    return x * lax.rsqrt(jnp.mean(x * x, axis=-1, keepdims=True) + EPS) * w


class _CastJob(NamedTuple):
    src: jax.Array
    lead: int


def _cast_job_specs(jobs, grid):
    n_steps = grid[0] * grid[1]
    in_specs, out_specs, out_shape = [], [], []
    for job in jobs:
        _, r, c = job.src.shape
        n_blocks = max(nb for nb in range(1, n_steps + 1)
                       if r % nb == 0 and (r // nb) % 16 == 0)
        rows, rep = r // n_blocks, n_steps // n_blocks
        blk = lambda i, j, rep=rep, last=n_blocks - 1: jnp.minimum((i * grid[1] + j) // rep, last)
        in_specs.append(pl.BlockSpec((None, rows, c),
                                     lambda i, j, job=job, blk=blk: (job.lead, blk(i, j), 0)))
        out_specs.append(pl.BlockSpec((rows, c), lambda i, j, blk=blk: (blk(i, j), 0)))
        out_shape.append(jax.ShapeDtypeStruct((r, c), BF16))
    return in_specs, out_specs, out_shape


def _run_cast_jobs(src_refs, dst_refs):
    for src_ref, dst_ref in zip(src_refs, dst_refs):
        dst_ref[...] = src_ref[...].astype(BF16)


def _mod_kernel(c_ref, w_ref, b_ref, o_ref):
    c = c_ref[...]
    s = (c * _sigmoid(c)).astype(BF16)
    o_ref[...] = _dot(s, w_ref[...].astype(BF16)) + b_ref[...]


def _modulation(cond, w_mod, b_mod):
    n_layers, d, n_out = w_mod.shape
    r = cond.shape[0]
    tn = 1024
    return pl.pallas_call(
        _mod_kernel,
        grid=(n_layers, n_out // tn),
        in_specs=[
            pl.BlockSpec((r, d), lambda l, j: (0, 0)),
            pl.BlockSpec((None, d, tn), lambda l, j: (l, 0, j)),
            pl.BlockSpec((None, 1, tn), lambda l, j: (l, 0, j)),
        ],
        out_specs=pl.BlockSpec((None, r, tn), lambda l, j: (l, 0, j)),
        out_shape=jax.ShapeDtypeStruct((n_layers, r, n_out), F32),
        compiler_params=_cparams("arbitrary", "arbitrary"),
        name="modulation",
    )(cond, w_mod, b_mod.reshape(n_layers, 1, n_out))


def _norm_mod(x_ref, mod_ref, nw_ref, h_scr, row):
    gain = nw_ref[...] * (1.0 + mod_ref[row + 1:row + 2, :])
    shift = mod_ref[row:row + 1, :]

    def chunk(c, carry):
        rows = pl.ds(pl.multiple_of(c * NORM_ROWS, NORM_ROWS), NORM_ROWS)
        x = x_ref[rows, :]
        r = lax.rsqrt(jnp.mean(x * x, axis=-1, keepdims=True) + EPS)
        h_scr[rows, :] = (x * r * gain + shift).astype(BF16)
        return carry

    lax.fori_loop(0, x_ref.shape[0] // NORM_ROWS, chunk, 0, unroll=4)


def _norm_proj_kernel(*refs, row, n_jobs):
    x_ref, mod_ref, nw_ref, w_ref = refs[:4]
    job_src = refs[4:4 + n_jobs]
    o_ref = refs[4 + n_jobs]
    job_dst = refs[5 + n_jobs:5 + 2 * n_jobs]
    h_scr = refs[-1]
    _run_cast_jobs(job_src, job_dst)

    @pl.when(pl.program_id(1) == 0)
    def _():
        _norm_mod(x_ref, mod_ref, nw_ref, h_scr, row)

    o_ref[...] = _dot(h_scr[...], w_ref[...]).astype(o_ref.dtype)


def _norm_proj(x, mod, mod_row0, rows_per_cond, norm_w, layer, w, w_idx, *, row, cast_jobs=()):
    m, d = x.shape
    n_out = w.shape[-1]
    tm = ROW_TILE
    tn = 2 * COL_TILE if n_out % (2 * COL_TILE) == 0 else COL_TILE
    grid = (m // tm, n_out // tn)
    cond_of = lambda i: mod_row0 + (i * tm) // rows_per_cond
    job_in, job_out, job_shape = _cast_job_specs(cast_jobs, grid)
    return pl.pallas_call(
        functools.partial(_norm_proj_kernel, row=row, n_jobs=len(cast_jobs)),
        grid=grid,
        in_specs=[
            pl.BlockSpec((tm, d), lambda i, j: (i, 0)),
            pl.BlockSpec((None, 6, d), lambda i, j: (cond_of(i), 0, 0)),
            pl.BlockSpec((None, 1, d), lambda i, j: (layer, 0, 0)),
            pl.BlockSpec((None, d, tn), lambda i, j: (w_idx, 0, j)),
        ] + job_in,
        out_specs=[pl.BlockSpec((tm, tn), lambda i, j: (i, j))] + job_out,
        out_shape=[jax.ShapeDtypeStruct((m, n_out), BF16)] + job_shape,
        scratch_shapes=[pltpu.VMEM((tm, d), BF16)],
        compiler_params=_cparams("parallel", "arbitrary"),
        name="norm_proj",
    )(x, mod, norm_w, w, *[job.src for job in cast_jobs])


def _conv3(u, cw, cb, first, last):
    t = u.shape[0]
    left = jnp.where(first, 0.0, pltpu.roll(u, 1, axis=0))
    right = jnp.where(last, 0.0, pltpu.roll(u, t - 1, axis=0))
    return left * cw[0:1, :] + u * cw[1:2, :] + right * cw[2:3, :] + cb


def _ffn_up_kernel(x_ref, mod_ref, nw_ref, wa_ref, wv_ref, cwa_ref, cwv_ref, cba_ref, cbv_ref,
                   o_ref, h_scr, *, seq_len):
    @pl.when(pl.program_id(1) == 0)
    def _():
        _norm_mod(x_ref, mod_ref, nw_ref, h_scr, 3)

    h = h_scr[...]
    tm = h.shape[0]
    pos = lax.broadcasted_iota(jnp.int32, (tm, 1), 0) % seq_len
    first = pos == 0
    last = pos == seq_len - 1
    a = _conv3(_dot(h, wa_ref[...]), cwa_ref[...], cba_ref[...], first, last)
    v = _conv3(_dot(h, wv_ref[...]), cwv_ref[...], cbv_ref[...], first, last)
    o_ref[...] = (a * _sigmoid(a) * v).astype(o_ref.dtype)


def _ffn_up(x, mod, mod_row0, rows_per_cond, norm_w, layer, w_up, w_idx, conv_w, conv_b,
            seq_len):
    m, d = x.shape
    d_ff = w_up.shape[-1] // 2
    tm, tn = ROW_TILE, COL_TILE
    nj = d_ff // tn
    cond_of = lambda i: mod_row0 + (i * tm) // rows_per_cond
    conv_b3 = conv_b.reshape(conv_b.shape[0], 1, 2 * d_ff)
    return pl.pallas_call(
        functools.partial(_ffn_up_kernel, seq_len=seq_len),
        grid=(m // tm, nj),
        in_specs=[
            pl.BlockSpec((tm, d), lambda i, j: (i, 0)),
            pl.BlockSpec((None, 6, d), lambda i, j: (cond_of(i), 0, 0)),
            pl.BlockSpec((None, 1, d), lambda i, j: (layer, 0, 0)),
            pl.BlockSpec((None, d, tn), lambda i, j: (w_idx, 0, j)),
            pl.BlockSpec((None, d, tn), lambda i, j: (w_idx, 0, nj + j)),
            pl.BlockSpec((None, 3, tn), lambda i, j: (layer, 0, j)),
            pl.BlockSpec((None, 3, tn), lambda i, j: (layer, 0, nj + j)),
            pl.BlockSpec((None, 1, tn), lambda i, j: (layer, 0, j)),
            pl.BlockSpec((None, 1, tn), lambda i, j: (layer, 0, nj + j)),
        ],
        out_specs=pl.BlockSpec((tm, tn), lambda i, j: (i, j)),
        out_shape=jax.ShapeDtypeStruct((m, d_ff), BF16),
        scratch_shapes=[pltpu.VMEM((tm, d), BF16)],
        compiler_params=_cparams("parallel", "arbitrary"),
        name="ffn_up",
    )(x, mod, norm_w, w_up, w_up, conv_w, conv_w, conv_b3, conv_b3)


def _proj_res_kernel(*refs, n_in, row, n_jobs):
    a_refs = refs[:n_in]
    w_refs = refs[n_in:2 * n_in]
    y_ref, mod_ref = refs[2 * n_in:2 * n_in + 2]
    job_src = refs[2 * n_in + 2:2 * n_in + 2 + n_jobs]
    o_ref = refs[2 * n_in + 2 + n_jobs]
    job_dst = refs[2 * n_in + 3 + n_jobs:]
    _run_cast_jobs(job_src, job_dst)
    acc = _dot(a_refs[0][...], w_refs[0][...])
    for a_ref, w_ref in zip(a_refs[1:], w_refs[1:]):
        acc += _dot(a_ref[...], w_ref[...])
    o_ref[...] = y_ref[...] + mod_ref[row:row + 1, :] * acc


def _proj_res(acts, w, w_idx, y, mod, mod_row0, rows_per_cond, *, row, cast_jobs=()):
    m, d = y.shape
    tm, tn = ROW_TILE, COL_TILE
    n_in = len(acts)
    cond_of = lambda i: mod_row0 + (i * tm) // rows_per_cond
    in_specs = [pl.BlockSpec((tm, a.shape[1]), lambda i, j: (i, 0)) for a in acts]
    k_blk = acts[0].shape[1]
    assert all(a.shape[1] == k_blk for a in acts)
    for k in range(n_in):
        in_specs.append(pl.BlockSpec((None, k_blk, tn), lambda i, j, k=k: (w_idx, k, j)))
    in_specs += [
        pl.BlockSpec((tm, tn), lambda i, j: (i, j)),
        pl.BlockSpec((None, 6, tn), lambda i, j: (cond_of(i), 0, j)),
    ]
    grid = (m // tm, d // tn)
    job_in, job_out, job_shape = _cast_job_specs(cast_jobs, grid)
    return pl.pallas_call(
        functools.partial(_proj_res_kernel, n_in=n_in, row=row, n_jobs=len(cast_jobs)),
        grid=grid,
        in_specs=in_specs + job_in,
        out_specs=[pl.BlockSpec((tm, tn), lambda i, j: (i, j))] + job_out,
        out_shape=[jax.ShapeDtypeStruct((m, d), F32)] + job_shape,
        compiler_params=_cparams("parallel", "arbitrary"),
        name="proj_res",
    )(*acts, *([w] * n_in), y, mod, *[job.src for job in cast_jobs])


def _rope_half_roll(x, cos2, sin2):
    return x * cos2 + pltpu.roll(x, x.shape[1] // 2, axis=1) * sin2


def _attn_kernel(*refs, n, n_ctx, rope, emit_kv, tq, n_jobs):
    q_ref, k_ref, v_ref, qw_ref, kw_ref = refs[:5]
    pos = 5
    if rope:
        cos_ref, sin_ref, ck_ref, cv_ref = refs[pos:pos + 4]
        pos += 4
    job_src = refs[pos:pos + n_jobs]
    pos += n_jobs
    o_ref = refs[pos]
    pos += 1
    if emit_kv:
        nk_ref, nv_ref = refs[pos:pos + 2]
        pos += 2
    job_dst = refs[pos:pos + n_jobs]
    pos += n_jobs
    kall, vall = refs[pos:pos + 2]
    _run_cast_jobs(job_src, job_dst)

    kn = _rms(k_ref[...].astype(F32), kw_ref[...])
    vv = v_ref[...]
    if emit_kv:
        nk_ref[...] = kn
        nv_ref[...] = vv.astype(F32)
    if rope:
        kn = _rope_half_roll(kn, cos_ref[...], sin_ref[...])
        kall[0:n_ctx, :] = ck_ref[...].astype(BF16)
        vall[0:n_ctx, :] = cv_ref[...].astype(BF16)
    kall[n_ctx:n_ctx + n, :] = kn.astype(BF16)
    vall[n_ctx:n_ctx + n, :] = vv.astype(BF16)

    scale = A_HEAD_DIM ** -0.5

    def chunk(c, carry):
        r0 = pl.multiple_of(c * tq, tq)
        rows = pl.ds(r0, tq)
        heads = [slice(g * A_HEAD_DIM, (g + 1) * A_HEAD_DIM) for g in range(A_GROUP)]
        qs = []
        for cols in heads:
            qh = _rms(q_ref[rows, cols].astype(F32), qw_ref[...])
            if rope:
                qh = _rope_half_roll(qh, cos_ref[rows, :], sin_ref[rows, :])
            qs.append((qh * scale).astype(BF16))
        ss = [_dot_nt(qh, kall[...]) for qh in qs]
        ps = [jnp.exp(s - jnp.max(s, axis=-1, keepdims=True)) for s in ss]
        ls = [jnp.sum(p, axis=-1, keepdims=True) for p in ps]
        os = [_dot(p.astype(BF16), vall[...]) / l for p, l in zip(ps, ls)]
        for cols, o in zip(heads, os):
            o_ref[rows, cols] = o.astype(o_ref.dtype)
        return carry

    lax.fori_loop(0, n // tq, chunk, 0, unroll=True)


def _attention(proj, n_seq, n, q_norm_w, k_norm_w, e, rope_tabs, cache_k, cache_v, emit_kv,
               cast_jobs=()):
    rope = rope_tabs is not None
    n_ctx = cache_k.shape[2] if rope else 0
    hd = A_HEAD_DIM
    qcols = A_GROUP * hd
    in_specs = [
        pl.BlockSpec((n, qcols), lambda b, kv: (b, kv)),
        pl.BlockSpec((n, hd), lambda b, kv: (b, A_Q // hd + kv)),
        pl.BlockSpec((n, hd), lambda b, kv: (b, (A_Q + A_KV) // hd + kv)),
        pl.BlockSpec((None, 1, hd), lambda b, kv: (e, 0, 0)),
        pl.BlockSpec((None, 1, hd), lambda b, kv: (e, 0, 0)),
    ]
    args = [proj, proj, proj, q_norm_w, k_norm_w]
    if rope:
        cos2, sin2 = rope_tabs
        in_specs += [
            pl.BlockSpec((n, hd), lambda b, kv: (0, 0)),
            pl.BlockSpec((n, hd), lambda b, kv: (0, 0)),
            pl.BlockSpec((None, None, n_ctx, hd), lambda b, kv: (b, e, 0, kv)),
            pl.BlockSpec((None, None, n_ctx, hd), lambda b, kv: (b, e, 0, kv)),
        ]
        ck = cache_k.reshape(cache_k.shape[0], cache_k.shape[1], n_ctx, A_KV)
        cv = cache_v.reshape(cache_v.shape[0], cache_v.shape[1], n_ctx, A_KV)
        args += [cos2, sin2, ck, cv]
    out_specs = [pl.BlockSpec((n, qcols), lambda b, kv: (b, kv))]
    out_shape = [jax.ShapeDtypeStruct((n_seq * n, A_Q), BF16)]
    if emit_kv:
        out_specs += [pl.BlockSpec((n, hd), lambda b, kv: (b, kv))] * 2
        out_shape += [jax.ShapeDtypeStruct((n_seq * n, A_KV), F32)] * 2
    grid = (n_seq, A_KV_HEADS)
    job_in, job_out, job_shape = _cast_job_specs(cast_jobs, grid)
    in_specs += job_in
    args += [job.src for job in cast_jobs]
    out_specs += job_out
    out_shape += job_shape
    return pl.pallas_call(
        functools.partial(_attn_kernel, n=n, n_ctx=n_ctx, rope=rope, emit_kv=emit_kv,
                          tq=min(n, 256), n_jobs=len(cast_jobs)),
        grid=grid,
        in_specs=in_specs,
        out_specs=out_specs,
        out_shape=out_shape,
        scratch_shapes=[pltpu.VMEM((n_ctx + n, hd), BF16), pltpu.VMEM((n_ctx + n, hd), BF16)],
        compiler_params=_cparams("parallel", "arbitrary"),
        name="attention",
    )(*args)


def _gla_levels(c, rev):
    row = lax.broadcasted_iota(jnp.int32, (c, c), 0)
    col = lax.broadcasted_iota(jnp.int32, (c, c), 1)
    shift = HGRN_BASE.bit_length() - 1
    x = (row >> shift) ^ (col >> shift)
    lvl = jnp.zeros((c, c), jnp.int32)
    for l in range(1, (c // HGRN_BASE).bit_length()):
        lvl = jnp.where(x >= (1 << (l - 1)), l, lvl)
    causal = (col >= row) if rev else (col <= row)
    return jnp.where(causal, lvl, -1)


def _gla_blocks(chains):
    c, dk = chains[0]["q"].shape
    each = lambda fn: [fn(ch) for ch in chains]

    def split3(ch):
        hi = ch["lf"].astype(BF16)
        r1 = ch["lf"] - hi.astype(F32)
        mid = r1.astype(BF16)
        ch["parts"] = (hi, mid, (r1 - mid.astype(F32)).astype(BF16))

    def cumulate(ch):
        hi, mid, lo = ch["parts"]
        ch["b"] = _dot(ch["tri"], hi) + _dot(ch["tri"], mid) + _dot(ch["tri"], lo)
        ch["tot"] = ch["b"][0:1, :] if ch["rev"] else ch["b"][c - 1:c, :]

    def inter(ch):
        ch["o"] = _dot_nt((ch["q"] * jnp.exp(ch["b"])).astype(BF16), ch["st"].astype(BF16))
        ch["khat"] = (ch["k"] * jnp.exp(ch["tot"] - ch["b"])).astype(BF16)
        ch["vb"] = ch["v"].astype(BF16)

    def level0(ch):
        b3 = ch["b"].reshape(c // HGRN_BASE, HGRN_BASE, dk)
        mid_row = HGRN_BASE // 2 if ch["rev"] else HGRN_BASE // 2 - 1
        a = jnp.clip(b3 - b3[:, mid_row:mid_row + 1, :], -EXP_CLAMP, EXP_CLAMP).reshape(c, dk)
        p = _dot_nt((ch["q"] * jnp.exp(a)).astype(BF16), (ch["k"] * jnp.exp(-a)).astype(BF16))
        ch["scores"] = jnp.where(ch["lvl"] == 0, p, 0.0)

    def upper_level(ch, h, level):
        b3 = ch["b"].reshape(c // (2 * h), 2 * h, dk)
        ref_row = h if ch["rev"] else h - 1
        e = jnp.exp(-jnp.abs(b3 - b3[:, ref_row:ref_row + 1, :])).reshape(c, dk)
        p = _dot_nt((ch["q"] * e).astype(BF16), (ch["k"] * e).astype(BF16))
        ch["scores"] = jnp.where(ch["lvl"] == level, p, ch["scores"])

    def combine(ch):
        o = ch["o"] + _dot(ch["scores"].astype(BF16), ch["vb"])
        st_new = ch["st"] * jnp.exp(ch["tot"]) + _dot_tn(ch["vb"], ch["khat"])
        return o, st_new

    each(split3)
    each(cumulate)
    each(inter)
    each(level0)
    h, level = HGRN_BASE, 1
    while h < c:
        each(functools.partial(upper_level, h=h, level=level))
        h, level = 2 * h, level + 1
    return each(combine)


def _hgrn_kernel(*refs, n, has_state, emit_state, n_jobs):
    q_ref, i_ref, zf_ref, zb_ref, g_ref, lb_ref, nw_ref = refs[:7]
    pos = 7
    if has_state:
        s0f_ref, s0b_ref = refs[pos:pos + 2]
        pos += 2
    job_src = refs[pos:pos + n_jobs]
    pos += n_jobs
    o_ref = refs[pos]
    pos += 1
    if emit_state:
        sf_ref, sb_ref = refs[pos:pos + 2]
        pos += 2
    job_dst = refs[pos:pos + n_jobs]
    pos += n_jobs
    of_scr, ob_scr, st_scr = refs[pos:pos + 3]
    _run_cast_jobs(job_src, job_dst)

    c = MIX_BLOCK
    nb = n // c
    qscale = B_DK ** -0.5
    lvl_f = _gla_levels(c, False)
    lvl_b = _gla_levels(c, True)
    tri_f = jnp.where(lvl_f >= 0, 1.0, 0.0).astype(BF16)
    tri_b = jnp.where(lvl_b >= 0, 1.0, 0.0).astype(BF16)
    heads = [slice(hh * B_DK, (hh + 1) * B_DK) for hh in range(HGRN_HEADS_PER_STEP)]

    def chain(z_ref, rows, hh, rev):
        cols = heads[hh]
        lb = lb_ref[hh]
        f = lb + (1.0 - lb) * _sigmoid(z_ref[rows, cols].astype(F32))
        return dict(q=q_ref[rows, cols].astype(F32) * qscale, k=1.0 - f,
                    v=i_ref[rows, cols].astype(F32), lf=jnp.log(f),
                    st=st_scr[2 * hh + int(rev)], rev=rev,
                    lvl=lvl_b if rev else lvl_f, tri=tri_b if rev else tri_f)

    def step(j, carry):
        rows_f = pl.ds(pl.multiple_of(j * c, c), c)
        rows_b = pl.ds(pl.multiple_of((nb - 1 - j) * c, c), c)
        chains = []
        for hh in range(len(heads)):
            chains += [chain(zf_ref, rows_f, hh, False), chain(zb_ref, rows_b, hh, True)]
        outs = _gla_blocks(chains)
        for hh, cols in enumerate(heads):
            (o_f, st_f), (o_b, st_b) = outs[2 * hh], outs[2 * hh + 1]
            of_scr[rows_f, cols] = o_f
            ob_scr[rows_b, cols] = o_b
            st_scr[2 * hh] = st_f
            st_scr[2 * hh + 1] = st_b
        return carry

    def finish(j, carry):
        rows = pl.ds(pl.multiple_of(j * c, c), c)
        for cols in heads:
            y = _rms(of_scr[rows, cols] + ob_scr[rows, cols], nw_ref[...])
            g = g_ref[rows, cols].astype(F32)
            o_ref[rows, cols] = (y * (g * _sigmoid(g))).astype(o_ref.dtype)
        return carry

    for hh in range(len(heads)):
        if has_state:
            st_scr[2 * hh] = s0f_ref[hh].T
            st_scr[2 * hh + 1] = s0b_ref[hh].T
        else:
            st_scr[2 * hh] = jnp.zeros((B_DV, B_DK), F32)
            st_scr[2 * hh + 1] = jnp.zeros((B_DV, B_DK), F32)
    lax.fori_loop(0, nb, step, 0, unroll=2)
    lax.fori_loop(0, nb, finish, 0)
    if emit_state:
        for hh in range(len(heads)):
            sf_ref[hh] = st_scr[2 * hh].T
            sb_ref[hh] = st_scr[2 * hh + 1].T


def _hgrn(proj, n_seq, n, lb, o_norm_w, e, state_f, state_b, emit_state, cast_jobs=()):
    has_state = state_f is not None
    hp = HGRN_HEADS_PER_STEP
    d = B_DK * hp
    base = (A_Q + 2 * A_KV) // d
    col = lambda k: (lambda b, h: (b, base + k * (B_HEADS // hp) + h))
    in_specs = [pl.BlockSpec((n, d), col(k)) for k in range(5)]
    in_specs += [
        pl.BlockSpec((hp, 1, B_DK), lambda b, h: (h, 0, 0)),
        pl.BlockSpec((None, 1, B_DV), lambda b, h: (e, 0, 0)),
    ]
    args = [proj] * 5 + [lb, o_norm_w]
    if has_state:
        st_spec = pl.BlockSpec((None, None, hp, B_DK, B_DV), lambda b, h: (b, e, h, 0, 0))
        in_specs += [st_spec, st_spec]
        args += [state_f, state_b]
    out_specs = [pl.BlockSpec((n, d), lambda b, h: (b, h))]
    out_shape = [jax.ShapeDtypeStruct((n_seq * n, B_V), BF16)]
    if emit_state:
        so = pl.BlockSpec((None, hp, B_DK, B_DV), lambda b, h: (b, h, 0, 0))
        out_specs += [so, so]
        out_shape += [jax.ShapeDtypeStruct((n_seq, B_HEADS, B_DK, B_DV), F32)] * 2
    grid = (n_seq, B_HEADS // hp)
    job_in, job_out, job_shape = _cast_job_specs(cast_jobs, grid)
    in_specs += job_in
    args += [job.src for job in cast_jobs]
    out_specs += job_out
    out_shape += job_shape
    return pl.pallas_call(
        functools.partial(_hgrn_kernel, n=n, has_state=has_state, emit_state=emit_state,
                          n_jobs=len(cast_jobs)),
        grid=grid,
        in_specs=in_specs,
        out_specs=out_specs,
        out_shape=out_shape,
        scratch_shapes=[pltpu.VMEM((n, d), F32), pltpu.VMEM((n, d), F32),
                        pltpu.VMEM((2 * hp, B_DV, B_DK), F32)],
        compiler_params=_cparams("parallel", "arbitrary"),
        name="hgrn2",
    )(*args)


def _rope_split(x, cos, sin):
    half = x.shape[1] // 2
    x1, x2 = x[:, :half], x[:, half:]
    return jnp.concatenate([x1 * cos - x2 * sin, x1 * sin + x2 * cos], axis=1)


def _ret_kernel(*refs, n, rope, has_state, emit_state, n_jobs):
    q_ref, k_ref, v_ref, g_ref, lgf_ref, lgb_ref, nw_ref = refs[:7]
    pos = 7
    if rope:
        cos_ref, sin_ref = refs[pos:pos + 2]
        pos += 2
    if has_state:
        s0f_ref, s0b_ref = refs[pos:pos + 2]
        pos += 2
    job_src = refs[pos:pos + n_jobs]
    pos += n_jobs
    o_ref = refs[pos]
    pos += 1
    if emit_state:
        sf_ref, sb_ref = refs[pos:pos + 2]
        pos += 2
    job_dst = refs[pos:pos + n_jobs]
    pos += n_jobs
    _run_cast_jobs(job_src, job_dst)
    qs_scr, ks_scr, of_scr, ob_scr, stf_scr, stb_scr, dm_scr, qd_scr, kd_scr = refs[pos:pos + 9]

    c = min(RET_BLOCK, n)
    nb = n // c
    skip_inter = (not has_state) and nb == 1
    kscale = C_DK ** -0.5
    rowi = lax.broadcasted_iota(jnp.int32, (c, c), 0)
    coli = lax.broadcasted_iota(jnp.int32, (c, c), 1)
    rowq = lax.broadcasted_iota(jnp.int32, (c, C_DK), 0).astype(F32)

    def prep(j, carry):
        rows = pl.ds(pl.multiple_of(j * MIX_BLOCK, MIX_BLOCK), MIX_BLOCK)
        q = q_ref[rows, :].astype(F32)
        k = k_ref[rows, :].astype(F32)
        if rope:
            q = _rope_split(q, cos_ref[rows, :], sin_ref[rows, :])
            k = _rope_split(k, cos_ref[rows, :], sin_ref[rows, :])
        qs_scr[rows, :] = q
        ks_scr[rows, :] = k * kscale
        return carry

    @pl.when(pl.program_id(1) == 0)
    def _():
        lgs = (lgf_ref[...], lgb_ref[...])
        dist = (rowi - coli).astype(F32)
        dm_scr[...] = (
            jnp.where(dist >= 0.0, jnp.exp(lgs[0][:, :c] * jnp.maximum(dist, 0.0)), 0.0)
            + jnp.where(dist <= 0.0, jnp.exp(lgs[1][:, :c] * jnp.maximum(-dist, 0.0)), 0.0))
        for d, rev in enumerate((False, True)):
            lgq = lgs[d][:, :C_DK]
            qd_scr[d] = jnp.exp(lgq * ((c - rowq) if rev else (rowq + 1.0)))
            kd_scr[d] = jnp.exp(lgq * (rowq if rev else (c - 1.0 - rowq)))

    def sweep_block(rows, d, lg_ref, st_scr, o_scr):
        vb = v_ref[rows, :]
        u = _dot_tn((ks_scr[rows, :] * kd_scr[d]).astype(BF16), vb)
        if skip_inter:
            st_scr[...] = u
        else:
            st = st_scr[...]
            o_scr[rows, :] = _dot((qs_scr[rows, :] * qd_scr[d]).astype(BF16), st.astype(BF16))
            st_scr[...] = st * jnp.exp(lg_ref[...] * float(c)) + u

    def sweep(j, carry):
        rows_f = pl.ds(pl.multiple_of(j * c, c), c)
        rows_b = pl.ds(pl.multiple_of((nb - 1 - j) * c, c), c)
        sweep_block(rows_f, 0, lgf_ref, stf_scr, of_scr)
        sweep_block(rows_b, 1, lgb_ref, stb_scr, ob_scr)
        return carry

    def finish(j, carry):
        keys = pl.ds(pl.multiple_of(j * c, c), c)
        kb = ks_scr[keys, :].astype(BF16)
        vb = v_ref[keys, :]
        pieces = [(pl.ds(pl.multiple_of(j * c + p * MIX_BLOCK, MIX_BLOCK), MIX_BLOCK),
                   slice(p * MIX_BLOCK, (p + 1) * MIX_BLOCK)) for p in range(c // MIX_BLOCK)]
        ss = [_dot_nt(qs_scr[rows, :].astype(BF16), kb) * dm_scr[within, :]
              for rows, within in pieces]
        os = [_dot(s.astype(BF16), vb) for s in ss]
        if not skip_inter:
            os = [o + of_scr[rows, :] + ob_scr[rows, :] for o, (rows, _) in zip(os, pieces)]
        ys = [_rms(o, nw_ref[...]) for o in os]
        for y, (rows, _) in zip(ys, pieces):
            g = g_ref[rows, :].astype(F32)
            o_ref[rows, :] = (y * (g * _sigmoid(g))).astype(o_ref.dtype)
        return carry

    lax.fori_loop(0, n // MIX_BLOCK, prep, 0)
    if has_state:
        stf_scr[...] = s0f_ref[...]
        stb_scr[...] = s0b_ref[...]
    elif not skip_inter:
        stf_scr[...] = jnp.zeros((C_DK, C_DV), F32)
        stb_scr[...] = jnp.zeros((C_DK, C_DV), F32)
    if emit_state or not skip_inter:
        lax.fori_loop(0, nb, sweep, 0, unroll=True)
    lax.fori_loop(0, nb, finish, 0, unroll=True)
    if emit_state:
        sf_ref[...] = stf_scr[...]
        sb_ref[...] = stb_scr[...]


def _retention(proj, n_seq, n, lg_f, lg_b, o_norm_w, o_idx, rope_tabs, state_f, state_b,
               emit_state, cast_jobs=()):
    rope = rope_tabs is not None
    has_state = state_f is not None
    nq = C_QK // C_DK
    c = min(RET_BLOCK, n)
    in_specs = [
        pl.BlockSpec((n, C_DK), lambda h, b: (b, h)),
        pl.BlockSpec((n, C_DK), lambda h, b: (b, nq + h)),
        pl.BlockSpec((n, C_DV), lambda h, b: (b, 2 * C_QK // C_DV + h)),
        pl.BlockSpec((n, C_DV), lambda h, b: (b, (2 * C_QK + C_V) // C_DV + h)),
        pl.BlockSpec((None, 1, C_DV), lambda h, b: (h, 0, 0)),
        pl.BlockSpec((None, 1, C_DV), lambda h, b: (h, 0, 0)),
        pl.BlockSpec((None, 1, C_DV), lambda h, b: (o_idx, 0, 0)),
    ]
    args = [proj] * 4 + [lg_f, lg_b, o_norm_w]
    if rope:
        in_specs += [pl.BlockSpec((n, C_DK // 2), lambda h, b: (0, 0))] * 2
        args += list(rope_tabs)
    if has_state:
        st_spec = pl.BlockSpec((None, None, None, C_DK, C_DV), lambda h, b: (b, o_idx, h, 0, 0))
        in_specs += [st_spec, st_spec]
        args += [state_f, state_b]
    out_specs = [pl.BlockSpec((n, C_DV), lambda h, b: (b, h))]
    out_shape = [jax.ShapeDtypeStruct((n_seq * n, C_V), BF16)]
    if emit_state:
        so = pl.BlockSpec((None, None, C_DK, C_DV), lambda h, b: (b, h, 0, 0))
        out_specs += [so, so]
        out_shape += [jax.ShapeDtypeStruct((n_seq, C_HEADS, C_DK, C_DV), F32)] * 2
    grid = (C_HEADS, n_seq)
    job_in, job_out, job_shape = _cast_job_specs(cast_jobs, grid)
    in_specs += job_in
    args += [job.src for job in cast_jobs]
    out_specs += job_out
    out_shape += job_shape
    return pl.pallas_call(
        functools.partial(_ret_kernel, n=n, rope=rope, has_state=has_state,
                          emit_state=emit_state, n_jobs=len(cast_jobs)),
        grid=grid,
        in_specs=in_specs,
        out_specs=out_specs,
        out_shape=out_shape,
        scratch_shapes=[pltpu.VMEM((n, C_DK), F32), pltpu.VMEM((n, C_DK), F32),
                        pltpu.VMEM((n, C_DV), F32), pltpu.VMEM((n, C_DV), F32),
                        pltpu.VMEM((C_DK, C_DV), F32), pltpu.VMEM((C_DK, C_DV), F32),
                        pltpu.VMEM((c, c), F32), pltpu.VMEM((2, c, C_DK), F32),
                        pltpu.VMEM((2, c, C_DK), F32)],
        compiler_params=_cparams("arbitrary", "arbitrary"),
        name="retention",
    )(*args)


def _rope_tables(n_tokens, head_dim):
    rows = n_tokens // GRID_W
    row = jnp.repeat(jnp.arange(rows, dtype=F32), GRID_W)
    col = jnp.tile(jnp.arange(GRID_W, dtype=F32), rows)
    quarter = head_dim // 4
    inv_freq = jnp.power(ROPE_BASE, -jnp.arange(quarter, dtype=F32) / quarter)
    ang = jnp.concatenate([row[:, None] * inv_freq, col[:, None] * inv_freq], axis=-1)
    return jnp.cos(ang), jnp.sin(ang)


def kernel(x_prompt, x_sample, cache_attn_k, cache_attn_v, state_hgrn_fwd, state_hgrn_bwd,
           state_ret_fwd, state_ret_bwd, c, c_ctx, w_mod, b_mod, norm_mix_w, norm_ffn_w,
           w_in_even, w_out_even, attn_q_norm_w, attn_k_norm_w, hgrn_lb, hgrn_o_norm_w,
           w_in_odd, w_out_odd, ret_decay_fwd, ret_decay_bwd, ret_o_norm_w,
           w_up, conv_w, conv_b, w_down):
    depth, d_model = norm_mix_w.shape
    bp, np_, _ = x_prompt.shape
    bs, ns, _ = x_sample.shape

    lb_all = jnp.cumsum(jax.nn.softmax(hgrn_lb.astype(F32), axis=0), axis=0)
    lg_f = jnp.broadcast_to(jax.nn.log_sigmoid(ret_decay_fwd.astype(F32))[:, :, None, None],
                            ret_decay_fwd.shape + (1, C_DV))
    lg_b = jnp.broadcast_to(jax.nn.log_sigmoid(ret_decay_bwd.astype(F32))[:, :, None, None],
                            ret_decay_bwd.shape + (1, C_DV))
    cos_a, sin_a = _rope_tables(ns, A_HEAD_DIM)
    rope_a = (jnp.concatenate([cos_a, cos_a], axis=1), jnp.concatenate([-sin_a, sin_a], axis=1))
    rope_c = _rope_tables(ns, C_DK)
    nmw = norm_mix_w.reshape(depth, 1, d_model)
    nfw = norm_ffn_w.reshape(depth, 1, d_model)
    qnw = attn_q_norm_w.reshape(-1, 1, A_HEAD_DIM)
    knw = attn_k_norm_w.reshape(-1, 1, A_HEAD_DIM)
    hnw = hgrn_o_norm_w.reshape(-1, 1, B_DV)
    rnw = ret_o_norm_w.reshape(-1, 1, C_DV)
    w_in = {0: w_in_even[0:1].astype(BF16)}
    w_out, w_gate_val, w_down_b = {}, {}, {}
    n_cond = 1 + bs
    pad = (-n_cond) % 8
    cond = jnp.concatenate([c_ctx[None, :], c, jnp.zeros((pad, d_model), F32)], axis=0)
    mod_all = _modulation(cond, w_mod, b_mod).reshape(depth, n_cond + pad, 6, d_model)

    groups = (
        dict(x=x_prompt.reshape(bp * np_, d_model), n_seq=bp, n=np_, row0=0,
             rows_per_cond=bp * np_, latent=False),
        dict(x=x_sample.reshape(bs * ns, d_model), n_seq=bs, n=ns, row0=1,
             rows_per_cond=ns, latent=True),
    )
    def w_in_f32(l):
        return (w_in_even, l // 2) if l % 2 == 0 else (w_in_odd, l // 2)

    results = []
    for grp in groups:
        y, n_seq, n = grp["x"], grp["n_seq"], grp["n"]
        row0, rpc, latent = grp["row0"], grp["rows_per_cond"], grp["latent"]
        new = dict(k=[], v=[], hf=[], hb=[], rf=[], rb=[])
        for l in range(depth):
            mod = mod_all[l]
            first = l not in w_out
            next_in = l + 1 < depth and l + 1 not in w_in
            jobs = [_CastJob(*w_in_f32(l + 1))] if next_in else []
            if l % 2 == 1 and first:
                jobs.append(_CastJob(w_up, l))
            proj, *copies = _norm_proj(y, mod, row0, rpc, nmw, l, w_in[l], 0, row=0, cast_jobs=jobs)
            if next_in:
                w_in[l + 1] = copies.pop(0)[None]
            if copies:
                w_gate_val[l] = copies.pop(0)[None]
            if l % 2 == 0:
                e = l // 2
                att = _attention(proj, n_seq, n, qnw, knw, e, rope_a if latent else None,
                                 cache_attn_k, cache_attn_v, emit_kv=not latent,
                                 cast_jobs=[_CastJob(w_out_even, e)] if first else ())
                lb = lb_all[e].reshape(B_HEADS, 1, B_DK)
                hg = _hgrn(proj, n_seq, n, lb, hnw, e,
                           state_hgrn_fwd if latent else None,
                           state_hgrn_bwd if latent else None, emit_state=not latent,
                           cast_jobs=[_CastJob(w_up, l), _CastJob(w_down, l)] if first else ())
                if first:
                    w_out[l], w_gate_val[l], w_down_b[l] = att[-1][None], hg[-2][None], hg[-1][None]
                if not latent:
                    new["k"].append(att[1].reshape(n_seq, n, A_KV_HEADS, A_HEAD_DIM))
                    new["v"].append(att[2].reshape(n_seq, n, A_KV_HEADS, A_HEAD_DIM))
                    new["hf"].append(hg[1])
                    new["hb"].append(hg[2])
                mixed, out_jobs = [att[0], hg[0]], []
            else:
                o = l // 2
                rt = _retention(proj, n_seq, n, lg_f[o], lg_b[o], rnw, o,
                                rope_c if latent else None,
                                state_ret_fwd if latent else None,
                                state_ret_bwd if latent else None, emit_state=not latent,
                                cast_jobs=[_CastJob(w_out_odd, o)] if first else ())
                if first:
                    w_out[l] = rt[-1][None]
                if not latent:
                    new["rf"].append(rt[1])
                    new["rb"].append(rt[2])
                mixed, out_jobs = [rt[0]], ([_CastJob(w_down, l)] if first else [])
            y, *copies = _proj_res(mixed, w_out[l], 0, y, mod, row0, rpc, row=2, cast_jobs=out_jobs)
            if copies:
                w_down_b[l] = copies[0][None]
            act = _ffn_up(y, mod, row0, rpc, nfw, l, w_gate_val[l], 0, conv_w, conv_b, n)
            y, = _proj_res([act], w_down_b[l], 0, y, mod, row0, rpc, row=5)
        results.append((y.reshape(n_seq, n, d_model), new))

    (y_p, new), (y_s, _) = results
    stack = lambda xs: jnp.stack(xs, axis=1)
    return (y_p, y_s, stack(new["k"]), stack(new["v"]), stack(new["hf"]), stack(new["hb"]),
            stack(new["rf"]), stack(new["rb"]))
```

```python
import functools
from typing import NamedTuple

import jax
import jax.numpy as jnp
from jax import lax
from jax.experimental import pallas as pl
from jax.experimental.pallas import tpu as pltpu

F32 = jnp.float32
BF16 = jnp.bfloat16

GRID_W = 64
A_HEADS = 8
A_KV_HEADS = 2
A_HEAD_DIM = 128
ROPE_BASE = 10000.0
B_HEADS = 8
B_DK = 128
B_DV = 128
C_HEADS = 8
C_DK = 256
C_DV = 512
EPS = 1e-6

A_Q = A_HEADS * A_HEAD_DIM
A_KV = A_KV_HEADS * A_HEAD_DIM
A_GROUP = A_HEADS // A_KV_HEADS
B_QK = B_HEADS * B_DK
B_V = B_HEADS * B_DV
C_QK = C_HEADS * C_DK
C_V = C_HEADS * C_DV

VMEM_LIMIT_BYTES = 56 * 1024 * 1024
ROW_TILE = 1024
COL_TILE = 512
MIX_BLOCK = 128
RET_BLOCK = 256
NORM_ROWS = 16
HGRN_BASE = 16
HGRN_HEADS_PER_STEP = 4
EXP_CLAMP = 80.0


def _cparams(*sem):
    return pltpu.CompilerParams(dimension_semantics=sem, vmem_limit_bytes=VMEM_LIMIT_BYTES)


def _dot(a, b):
    return jnp.dot(a, b, preferred_element_type=F32)


def _dot_nt(a, b):
    return lax.dot_general(a, b, (((1,), (1,)), ((), ())), preferred_element_type=F32)


def _dot_tn(a, b):
    return lax.dot_general(a, b, (((0,), (0,)), ((), ())), preferred_element_type=F32)


def _sigmoid(x):
    return 1.0 / (1.0 + jnp.exp(-x))


def _rms(x, w):
    return x * lax.rsqrt(jnp.mean(x * x, axis=-1, keepdims=True) + EPS) * w


class _CastJob(NamedTuple):
    src: jax.Array
    lead: int


def _cast_job_specs(jobs, grid):
    n_steps = grid[0] * grid[1]
    in_specs, out_specs, out_shape = [], [], []
    for job in jobs:
        _, r, c = job.src.shape
        n_blocks = max(nb for nb in range(1, n_steps + 1)
                       if r % nb == 0 and (r // nb) % 16 == 0)
        rows, rep = r // n_blocks, n_steps // n_blocks
        blk = lambda i, j, rep=rep, last=n_blocks - 1: jnp.minimum((i * grid[1] + j) // rep, last)
        in_specs.append(pl.BlockSpec((None, rows, c),
                                     lambda i, j, job=job, blk=blk: (job.lead, blk(i, j), 0)))
        out_specs.append(pl.BlockSpec((rows, c), lambda i, j, blk=blk: (blk(i, j), 0)))
        out_shape.append(jax.ShapeDtypeStruct((r, c), BF16))
    return in_specs, out_specs, out_shape


def _run_cast_jobs(src_refs, dst_refs):
    for src_ref, dst_ref in zip(src_refs, dst_refs):
        dst_ref[...] = src_ref[...].astype(BF16)


def _mod_kernel(c_ref, w_ref, b_ref, o_ref):
    c = c_ref[...]
    s = (c * _sigmoid(c)).astype(BF16)
    o_ref[...] = _dot(s, w_ref[...].astype(BF16)) + b_ref[...]


def _modulation(cond, w_mod, b_mod):
    n_layers, d, n_out = w_mod.shape
    r = cond.shape[0]
    tn = 1024
    return pl.pallas_call(
        _mod_kernel,
        grid=(n_layers, n_out // tn),
        in_specs=[
            pl.BlockSpec((r, d), lambda l, j: (0, 0)),
            pl.BlockSpec((None, d, tn), lambda l, j: (l, 0, j)),
            pl.BlockSpec((None, 1, tn), lambda l, j: (l, 0, j)),
        ],
        out_specs=pl.BlockSpec((None, r, tn), lambda l, j: (l, 0, j)),
        out_shape=jax.ShapeDtypeStruct((n_layers, r, n_out), F32),
        compiler_params=_cparams("arbitrary", "arbitrary"),
        name="modulation",
    )(cond, w_mod, b_mod.reshape(n_layers, 1, n_out))


def _norm_mod(x_ref, mod_ref, nw_ref, h_scr, row):
    gain = nw_ref[...] * (1.0 + mod_ref[row + 1:row + 2, :])
    shift = mod_ref[row:row + 1, :]

    def chunk(c, carry):
        rows = pl.ds(pl.multiple_of(c * NORM_ROWS, NORM_ROWS), NORM_ROWS)
        x = x_ref[rows, :]
        r = lax.rsqrt(jnp.mean(x * x, axis=-1, keepdims=True) + EPS)
        h_scr[rows, :] = (x * r * gain + shift).astype(BF16)
        return carry

    lax.fori_loop(0, x_ref.shape[0] // NORM_ROWS, chunk, 0, unroll=4)


def _norm_proj_kernel(*refs, row, n_jobs):
    x_ref, mod_ref, nw_ref, w_ref = refs[:4]
    job_src = refs[4:4 + n_jobs]
    o_ref = refs[4 + n_jobs]
    job_dst = refs[5 + n_jobs:5 + 2 * n_jobs]
    h_scr = refs[-1]
    _run_cast_jobs(job_src, job_dst)

    @pl.when(pl.program_id(1) == 0)
    def _():
        _norm_mod(x_ref, mod_ref, nw_ref, h_scr, row)

    o_ref[...] = _dot(h_scr[...], w_ref[...]).astype(o_ref.dtype)


def _norm_proj(x, mod, mod_row0, rows_per_cond, norm_w, layer, w, w_idx, *, row, cast_jobs=()):
    m, d = x.shape
    n_out = w.shape[-1]
    tm = ROW_TILE
    tn = 2 * COL_TILE if n_out % (2 * COL_TILE) == 0 else COL_TILE
    grid = (m // tm, n_out // tn)
    cond_of = lambda i: mod_row0 + (i * tm) // rows_per_cond
    job_in, job_out, job_shape = _cast_job_specs(cast_jobs, grid)
    return pl.pallas_call(
        functools.partial(_norm_proj_kernel, row=row, n_jobs=len(cast_jobs)),
        grid=grid,
        in_specs=[
            pl.BlockSpec((tm, d), lambda i, j: (i, 0)),
            pl.BlockSpec((None, 6, d), lambda i, j: (cond_of(i), 0, 0)),
            pl.BlockSpec((None, 1, d), lambda i, j: (layer, 0, 0)),
            pl.BlockSpec((None, d, tn), lambda i, j: (w_idx, 0, j)),
        ] + job_in,
        out_specs=[pl.BlockSpec((tm, tn), lambda i, j: (i, j))] + job_out,
        out_shape=[jax.ShapeDtypeStruct((m, n_out), BF16)] + job_shape,
        scratch_shapes=[pltpu.VMEM((tm, d), BF16)],
        compiler_params=_cparams("parallel", "arbitrary"),
        name="norm_proj",
    )(x, mod, norm_w, w, *[job.src for job in cast_jobs])


def _conv3(u, cw, cb, first, last):
    t = u.shape[0]
    left = jnp.where(first, 0.0, pltpu.roll(u, 1, axis=0))
    right = jnp.where(last, 0.0, pltpu.roll(u, t - 1, axis=0))
    return left * cw[0:1, :] + u * cw[1:2, :] + right * cw[2:3, :] + cb


def _ffn_up_kernel(x_ref, mod_ref, nw_ref, wa_ref, wv_ref, cwa_ref, cwv_ref, cba_ref, cbv_ref,
                   o_ref, h_scr, *, seq_len):
    @pl.when(pl.program_id(1) == 0)
    def _():
        _norm_mod(x_ref, mod_ref, nw_ref, h_scr, 3)

    h = h_scr[...]
    tm = h.shape[0]
    pos = lax.broadcasted_iota(jnp.int32, (tm, 1), 0) % seq_len
    first = pos == 0
    last = pos == seq_len - 1
    a = _conv3(_dot(h, wa_ref[...]), cwa_ref[...], cba_ref[...], first, last)
    v = _conv3(_dot(h, wv_ref[...]), cwv_ref[...], cbv_ref[...], first, last)
    o_ref[...] = (a * _sigmoid(a) * v).astype(o_ref.dtype)


def _ffn_up(x, mod, mod_row0, rows_per_cond, norm_w, layer, w_up, w_idx, conv_w, conv_b,
            seq_len):
    m, d = x.shape
    d_ff = w_up.shape[-1] // 2
    tm, tn = ROW_TILE, COL_TILE
    nj = d_ff // tn
    cond_of = lambda i: mod_row0 + (i * tm) // rows_per_cond
    conv_b3 = conv_b.reshape(conv_b.shape[0], 1, 2 * d_ff)
    return pl.pallas_call(
        functools.partial(_ffn_up_kernel, seq_len=seq_len),
        grid=(m // tm, nj),
        in_specs=[
            pl.BlockSpec((tm, d), lambda i, j: (i, 0)),
            pl.BlockSpec((None, 6, d), lambda i, j: (cond_of(i), 0, 0)),
            pl.BlockSpec((None, 1, d), lambda i, j: (layer, 0, 0)),
            pl.BlockSpec((None, d, tn), lambda i, j: (w_idx, 0, j)),
            pl.BlockSpec((None, d, tn), lambda i, j: (w_idx, 0, nj + j)),
            pl.BlockSpec((None, 3, tn), lambda i, j: (layer, 0, j)),
            pl.BlockSpec((None, 3, tn), lambda i, j: (layer, 0, nj + j)),
            pl.BlockSpec((None, 1, tn), lambda i, j: (layer, 0, j)),
            pl.BlockSpec((None, 1, tn), lambda i, j: (layer, 0, nj + j)),
        ],
        out_specs=pl.BlockSpec((tm, tn), lambda i, j: (i, j)),
        out_shape=jax.ShapeDtypeStruct((m, d_ff), BF16),
        scratch_shapes=[pltpu.VMEM((tm, d), BF16)],
        compiler_params=_cparams("parallel", "arbitrary"),
        name="ffn_up",
    )(x, mod, norm_w, w_up, w_up, conv_w, conv_w, conv_b3, conv_b3)


def _proj_res_kernel(*refs, n_in, row, n_jobs):
    a_refs = refs[:n_in]
    w_refs = refs[n_in:2 * n_in]
    y_ref, mod_ref = refs[2 * n_in:2 * n_in + 2]
    job_src = refs[2 * n_in + 2:2 * n_in + 2 + n_jobs]
    o_ref = refs[2 * n_in + 2 + n_jobs]
    job_dst = refs[2 * n_in + 3 + n_jobs:]
    _run_cast_jobs(job_src, job_dst)
    acc = _dot(a_refs[0][...], w_refs[0][...])
    for a_ref, w_ref in zip(a_refs[1:], w_refs[1:]):
        acc += _dot(a_ref[...], w_ref[...])
    o_ref[...] = y_ref[...] + mod_ref[row:row + 1, :] * acc


def _proj_res(acts, w, w_idx, y, mod, mod_row0, rows_per_cond, *, row, cast_jobs=()):
    m, d = y.shape
    tm, tn = ROW_TILE, COL_TILE
    n_in = len(acts)
    cond_of = lambda i: mod_row0 + (i * tm) // rows_per_cond
    in_specs = [pl.BlockSpec((tm, a.shape[1]), lambda i, j: (i, 0)) for a in acts]
    k_blk = acts[0].shape[1]
    assert all(a.shape[1] == k_blk for a in acts)
    for k in range(n_in):
        in_specs.append(pl.BlockSpec((None, k_blk, tn), lambda i, j, k=k: (w_idx, k, j)))
    in_specs += [
        pl.BlockSpec((tm, tn), lambda i, j: (i, j)),
        pl.BlockSpec((None, 6, tn), lambda i, j: (cond_of(i), 0, j)),
    ]
    grid = (m // tm, d // tn)
    job_in, job_out, job_shape = _cast_job_specs(cast_jobs, grid)
    return pl.pallas_call(
        functools.partial(_proj_res_kernel, n_in=n_in, row=row, n_jobs=len(cast_jobs)),
        grid=grid,
        in_specs=in_specs + job_in,
        out_specs=[pl.BlockSpec((tm, tn), lambda i, j: (i, j))] + job_out,
        out_shape=[jax.ShapeDtypeStruct((m, d), F32)] + job_shape,
        compiler_params=_cparams("parallel", "arbitrary"),
        name="proj_res",
    )(*acts, *([w] * n_in), y, mod, *[job.src for job in cast_jobs])


def _rope_half_roll(x, cos2, sin2):
    return x * cos2 + pltpu.roll(x, x.shape[1] // 2, axis=1) * sin2


def _attn_kernel(*refs, n, n_ctx, rope, emit_kv, tq, n_jobs):
    q_ref, k_ref, v_ref, qw_ref, kw_ref = refs[:5]
    pos = 5
    if rope:
        cos_ref, sin_ref, ck_ref, cv_ref = refs[pos:pos + 4]
        pos += 4
    job_src = refs[pos:pos + n_jobs]
    pos += n_jobs
    o_ref = refs[pos]
    pos += 1
    if emit_kv:
        nk_ref, nv_ref = refs[pos:pos + 2]
        pos += 2
    job_dst = refs[pos:pos + n_jobs]
    pos += n_jobs
    kall, vall = refs[pos:pos + 2]
    _run_cast_jobs(job_src, job_dst)

    kn = _rms(k_ref[...].astype(F32), kw_ref[...])
    vv = v_ref[...]
    if emit_kv:
        nk_ref[...] = kn
        nv_ref[...] = vv.astype(F32)
    if rope:
        kn = _rope_half_roll(kn, cos_ref[...], sin_ref[...])
        kall[0:n_ctx, :] = ck_ref[...].astype(BF16)
        vall[0:n_ctx, :] = cv_ref[...].astype(BF16)
    kall[n_ctx:n_ctx + n, :] = kn.astype(BF16)
    vall[n_ctx:n_ctx + n, :] = vv.astype(BF16)

    scale = A_HEAD_DIM ** -0.5

    def chunk(c, carry):
        r0 = pl.multiple_of(c * tq, tq)
        rows = pl.ds(r0, tq)
        heads = [slice(g * A_HEAD_DIM, (g + 1) * A_HEAD_DIM) for g in range(A_GROUP)]
        qs = []
        for cols in heads:
            qh = _rms(q_ref[rows, cols].astype(F32), qw_ref[...])
            if rope:
                qh = _rope_half_roll(qh, cos_ref[rows, :], sin_ref[rows, :])
            qs.append((qh * scale).astype(BF16))
        ss = [_dot_nt(qh, kall[...]) for qh in qs]
        ps = [jnp.exp(s - jnp.max(s, axis=-1, keepdims=True)) for s in ss]
        ls = [jnp.sum(p, axis=-1, keepdims=True) for p in ps]
        os = [_dot(p.astype(BF16), vall[...]) / l for p, l in zip(ps, ls)]
        for cols, o in zip(heads, os):
            o_ref[rows, cols] = o.astype(o_ref.dtype)
        return carry

    lax.fori_loop(0, n // tq, chunk, 0, unroll=True)


def _attention(proj, n_seq, n, q_norm_w, k_norm_w, e, rope_tabs, cache_k, cache_v, emit_kv,
               cast_jobs=()):
    rope = rope_tabs is not None
    n_ctx = cache_k.shape[2] if rope else 0
    hd = A_HEAD_DIM
    qcols = A_GROUP * hd
    in_specs = [
        pl.BlockSpec((n, qcols), lambda b, kv: (b, kv)),
        pl.BlockSpec((n, hd), lambda b, kv: (b, A_Q // hd + kv)),
        pl.BlockSpec((n, hd), lambda b, kv: (b, (A_Q + A_KV) // hd + kv)),
        pl.BlockSpec((None, 1, hd), lambda b, kv: (e, 0, 0)),
        pl.BlockSpec((None, 1, hd), lambda b, kv: (e, 0, 0)),
    ]
    args = [proj, proj, proj, q_norm_w, k_norm_w]
    if rope:
        cos2, sin2 = rope_tabs
        in_specs += [
            pl.BlockSpec((n, hd), lambda b, kv: (0, 0)),
            pl.BlockSpec((n, hd), lambda b, kv: (0, 0)),
            pl.BlockSpec((None, None, n_ctx, hd), lambda b, kv: (b, e, 0, kv)),
            pl.BlockSpec((None, None, n_ctx, hd), lambda b, kv: (b, e, 0, kv)),
        ]
        ck = cache_k.reshape(cache_k.shape[0], cache_k.shape[1], n_ctx, A_KV)
        cv = cache_v.reshape(cache_v.shape[0], cache_v.shape[1], n_ctx, A_KV)
        args += [cos2, sin2, ck, cv]
    out_specs = [pl.BlockSpec((n, qcols), lambda b, kv: (b, kv))]
    out_shape = [jax.ShapeDtypeStruct((n_seq * n, A_Q), BF16)]
    if emit_kv:
        out_specs += [pl.BlockSpec((n, hd), lambda b, kv: (b, kv))] * 2
        out_shape += [jax.ShapeDtypeStruct((n_seq * n, A_KV), F32)] * 2
    grid = (n_seq, A_KV_HEADS)
    job_in, job_out, job_shape = _cast_job_specs(cast_jobs, grid)
    in_specs += job_in
    args += [job.src for job in cast_jobs]
    out_specs += job_out
    out_shape += job_shape
    return pl.pallas_call(
        functools.partial(_attn_kernel, n=n, n_ctx=n_ctx, rope=rope, emit_kv=emit_kv,
                          tq=min(n, 256), n_jobs=len(cast_jobs)),
        grid=grid,
        in_specs=in_specs,
        out_specs=out_specs,
        out_shape=out_shape,
        scratch_shapes=[pltpu.VMEM((n_ctx + n, hd), BF16), pltpu.VMEM((n_ctx + n, hd), BF16)],
        compiler_params=_cparams("parallel", "arbitrary"),
        name="attention",
    )(*args)


def _gla_levels(c, rev):
    row = lax.broadcasted_iota(jnp.int32, (c, c), 0)
    col = lax.broadcasted_iota(jnp.int32, (c, c), 1)
    shift = HGRN_BASE.bit_length() - 1
    x = (row >> shift) ^ (col >> shift)
    lvl = jnp.zeros((c, c), jnp.int32)
    for l in range(1, (c // HGRN_BASE).bit_length()):
        lvl = jnp.where(x >= (1 << (l - 1)), l, lvl)
    causal = (col >= row) if rev else (col <= row)
    return jnp.where(causal, lvl, -1)


def _gla_blocks(chains):
    c, dk = chains[0]["q"].shape
    each = lambda fn: [fn(ch) for ch in chains]

    def split3(ch):
        hi = ch["lf"].astype(BF16)
        r1 = ch["lf"] - hi.astype(F32)
        mid = r1.astype(BF16)
        ch["parts"] = (hi, mid, (r1 - mid.astype(F32)).astype(BF16))

    def cumulate(ch):
        hi, mid, lo = ch["parts"]
        ch["b"] = _dot(ch["tri"], hi) + _dot(ch["tri"], mid) + _dot(ch["tri"], lo)
        ch["tot"] = ch["b"][0:1, :] if ch["rev"] else ch["b"][c - 1:c, :]

    def inter(ch):
        ch["o"] = _dot_nt((ch["q"] * jnp.exp(ch["b"])).astype(BF16), ch["st"].astype(BF16))
        ch["khat"] = (ch["k"] * jnp.exp(ch["tot"] - ch["b"])).astype(BF16)
        ch["vb"] = ch["v"].astype(BF16)

    def level0(ch):
        b3 = ch["b"].reshape(c // HGRN_BASE, HGRN_BASE, dk)
        mid_row = HGRN_BASE // 2 if ch["rev"] else HGRN_BASE // 2 - 1
        a = jnp.clip(b3 - b3[:, mid_row:mid_row + 1, :], -EXP_CLAMP, EXP_CLAMP).reshape(c, dk)
        p = _dot_nt((ch["q"] * jnp.exp(a)).astype(BF16), (ch["k"] * jnp.exp(-a)).astype(BF16))
        ch["scores"] = jnp.where(ch["lvl"] == 0, p, 0.0)

    def upper_level(ch, h, level):
        b3 = ch["b"].reshape(c // (2 * h), 2 * h, dk)
        ref_row = h if ch["rev"] else h - 1
        e = jnp.exp(-jnp.abs(b3 - b3[:, ref_row:ref_row + 1, :])).reshape(c, dk)
        p = _dot_nt((ch["q"] * e).astype(BF16), (ch["k"] * e).astype(BF16))
        ch["scores"] = jnp.where(ch["lvl"] == level, p, ch["scores"])

    def combine(ch):
        o = ch["o"] + _dot(ch["scores"].astype(BF16), ch["vb"])
        st_new = ch["st"] * jnp.exp(ch["tot"]) + _dot_tn(ch["vb"], ch["khat"])
        return o, st_new

    each(split3)
    each(cumulate)
    each(inter)
    each(level0)
    h, level = HGRN_BASE, 1
    while h < c:
        each(functools.partial(upper_level, h=h, level=level))
        h, level = 2 * h, level + 1
    return each(combine)


def _hgrn_kernel(*refs, n, has_state, emit_state, n_jobs):
    q_ref, i_ref, zf_ref, zb_ref, g_ref, lb_ref, nw_ref = refs[:7]
    pos = 7
    if has_state:
        s0f_ref, s0b_ref = refs[pos:pos + 2]
        pos += 2
    job_src = refs[pos:pos + n_jobs]
    pos += n_jobs
    o_ref = refs[pos]
    pos += 1
    if emit_state:
        sf_ref, sb_ref = refs[pos:pos + 2]
        pos += 2
    job_dst = refs[pos:pos + n_jobs]
    pos += n_jobs
    of_scr, ob_scr, st_scr = refs[pos:pos + 3]
    _run_cast_jobs(job_src, job_dst)

    c = MIX_BLOCK
    nb = n // c
    qscale = B_DK ** -0.5
    lvl_f = _gla_levels(c, False)
    lvl_b = _gla_levels(c, True)
    tri_f = jnp.where(lvl_f >= 0, 1.0, 0.0).astype(BF16)
    tri_b = jnp.where(lvl_b >= 0, 1.0, 0.0).astype(BF16)
    heads = [slice(hh * B_DK, (hh + 1) * B_DK) for hh in range(HGRN_HEADS_PER_STEP)]

    def chain(z_ref, rows, hh, rev):
        cols = heads[hh]
        lb = lb_ref[hh]
        f = lb + (1.0 - lb) * _sigmoid(z_ref[rows, cols].astype(F32))
        return dict(q=q_ref[rows, cols].astype(F32) * qscale, k=1.0 - f,
                    v=i_ref[rows, cols].astype(F32), lf=jnp.log(f),
                    st=st_scr[2 * hh + int(rev)], rev=rev,
                    lvl=lvl_b if rev else lvl_f, tri=tri_b if rev else tri_f)

    def step(j, carry):
        rows_f = pl.ds(pl.multiple_of(j * c, c), c)
        rows_b = pl.ds(pl.multiple_of((nb - 1 - j) * c, c), c)
        chains = []
        for hh in range(len(heads)):
            chains += [chain(zf_ref, rows_f, hh, False), chain(zb_ref, rows_b, hh, True)]
        outs = _gla_blocks(chains)
        for hh, cols in enumerate(heads):
            (o_f, st_f), (o_b, st_b) = outs[2 * hh], outs[2 * hh + 1]
            of_scr[rows_f, cols] = o_f
            ob_scr[rows_b, cols] = o_b
            st_scr[2 * hh] = st_f
            st_scr[2 * hh + 1] = st_b
        return carry

    def finish(j, carry):
        rows = pl.ds(pl.multiple_of(j * c, c), c)
        for cols in heads:
            y = _rms(of_scr[rows, cols] + ob_scr[rows, cols], nw_ref[...])
            g = g_ref[rows, cols].astype(F32)
            o_ref[rows, cols] = (y * (g * _sigmoid(g))).astype(o_ref.dtype)
        return carry

    for hh in range(len(heads)):
        if has_state:
            st_scr[2 * hh] = s0f_ref[hh].T
            st_scr[2 * hh + 1] = s0b_ref[hh].T
        else:
            st_scr[2 * hh] = jnp.zeros((B_DV, B_DK), F32)
            st_scr[2 * hh + 1] = jnp.zeros((B_DV, B_DK), F32)
    lax.fori_loop(0, nb, step, 0, unroll=2)
    lax.fori_loop(0, nb, finish, 0, unroll=2)
    if emit_state:
        for hh in range(len(heads)):
            sf_ref[hh] = st_scr[2 * hh].T
            sb_ref[hh] = st_scr[2 * hh + 1].T


def _hgrn(proj, n_seq, n, lb, o_norm_w, e, state_f, state_b, emit_state, cast_jobs=()):
    has_state = state_f is not None
    hp = HGRN_HEADS_PER_STEP
    d = B_DK * hp
    base = (A_Q + 2 * A_KV) // d
    col = lambda k: (lambda b, h: (b, base + k * (B_HEADS // hp) + h))
    in_specs = [pl.BlockSpec((n, d), col(k)) for k in range(5)]
    in_specs += [
        pl.BlockSpec((hp, 1, B_DK), lambda b, h: (h, 0, 0)),
        pl.BlockSpec((None, 1, B_DV), lambda b, h: (e, 0, 0)),
    ]
    args = [proj] * 5 + [lb, o_norm_w]
    if has_state:
        st_spec = pl.BlockSpec((None, None, hp, B_DK, B_DV), lambda b, h: (b, e, h, 0, 0))
        in_specs += [st_spec, st_spec]
        args += [state_f, state_b]
    out_specs = [pl.BlockSpec((n, d), lambda b, h: (b, h))]
    out_shape = [jax.ShapeDtypeStruct((n_seq * n, B_V), BF16)]
    if emit_state:
        so = pl.BlockSpec((None, hp, B_DK, B_DV), lambda b, h: (b, h, 0, 0))
        out_specs += [so, so]
        out_shape += [jax.ShapeDtypeStruct((n_seq, B_HEADS, B_DK, B_DV), F32)] * 2
    grid = (n_seq, B_HEADS // hp)
    job_in, job_out, job_shape = _cast_job_specs(cast_jobs, grid)
    in_specs += job_in
    args += [job.src for job in cast_jobs]
    out_specs += job_out
    out_shape += job_shape
    return pl.pallas_call(
        functools.partial(_hgrn_kernel, n=n, has_state=has_state, emit_state=emit_state,
                          n_jobs=len(cast_jobs)),
        grid=grid,
        in_specs=in_specs,
        out_specs=out_specs,
        out_shape=out_shape,
        scratch_shapes=[pltpu.VMEM((n, d), F32), pltpu.VMEM((n, d), F32),
                        pltpu.VMEM((2 * hp, B_DV, B_DK), F32)],
        compiler_params=_cparams("parallel", "arbitrary"),
        name="hgrn2",
    )(*args)


def _rope_split(x, cos, sin):
    half = x.shape[1] // 2
    x1, x2 = x[:, :half], x[:, half:]
    return jnp.concatenate([x1 * cos - x2 * sin, x1 * sin + x2 * cos], axis=1)


def _ret_kernel(*refs, n, n_seqs, rope, has_state, emit_state, n_jobs):
    q_ref, k_ref, v_ref, g_ref, lgf_ref, lgb_ref, nw_ref = refs[:7]
    pos = 7
    if rope:
        cos_ref, sin_ref = refs[pos:pos + 2]
        pos += 2
    if has_state:
        s0f_ref, s0b_ref = refs[pos:pos + 2]
        pos += 2
    job_src = refs[pos:pos + n_jobs]
    pos += n_jobs
    o_ref = refs[pos]
    pos += 1
    if emit_state:
        sf_ref, sb_ref = refs[pos:pos + 2]
        pos += 2
    job_dst = refs[pos:pos + n_jobs]
    pos += n_jobs
    _run_cast_jobs(job_src, job_dst)
    qs_scr, ks_scr, of_scr, ob_scr, stf_scr, stb_scr, dm_scr, qd_scr, kd_scr = refs[pos:pos + 9]

    c = min(RET_BLOCK, n)
    nb = n // c
    seqs = range(n_seqs)
    skip_inter = (not has_state) and nb == 1
    kscale = C_DK ** -0.5
    rowi = lax.broadcasted_iota(jnp.int32, (c, c), 0)
    coli = lax.broadcasted_iota(jnp.int32, (c, c), 1)
    rowq = lax.broadcasted_iota(jnp.int32, (c, C_DK), 0).astype(F32)

    def prep(j, carry):
        rows = pl.ds(pl.multiple_of(j * MIX_BLOCK, MIX_BLOCK), MIX_BLOCK)
        q = q_ref[rows, :].astype(F32)
        k = k_ref[rows, :].astype(F32)
        if rope:
            q = _rope_split(q, cos_ref[rows, :], sin_ref[rows, :])
            k = _rope_split(k, cos_ref[rows, :], sin_ref[rows, :])
        qs_scr[rows, :] = q
        ks_scr[rows, :] = k * kscale
        return carry

    @pl.when(pl.program_id(1) == 0)
    def _():
        lgs = (lgf_ref[...], lgb_ref[...])
        dist = (rowi - coli).astype(F32)
        dm_scr[...] = (
            jnp.where(dist >= 0.0, jnp.exp(lgs[0][:, :c] * jnp.maximum(dist, 0.0)), 0.0)
            + jnp.where(dist <= 0.0, jnp.exp(lgs[1][:, :c] * jnp.maximum(-dist, 0.0)), 0.0))
        for d, rev in enumerate((False, True)):
            lgq = lgs[d][:, :C_DK]
            qd_scr[d] = jnp.exp(lgq * ((c - rowq) if rev else (rowq + 1.0)))
            kd_scr[d] = jnp.exp(lgq * (rowq if rev else (c - 1.0 - rowq)))

    def block_rows(s, j):
        return pl.ds(pl.multiple_of(s * n + j * c, c), c)

    def sweep_block(s, rows, d, lg_ref, st_scr, o_scr):
        vb = v_ref[rows, :]
        u = _dot_tn((ks_scr[rows, :] * kd_scr[d]).astype(BF16), vb)
        if skip_inter:
            st_scr[s] = u
        else:
            st = st_scr[s]
            o_scr[rows, :] = _dot((qs_scr[rows, :] * qd_scr[d]).astype(BF16), st.astype(BF16))
            st_scr[s] = st * jnp.exp(lg_ref[...] * float(c)) + u

    def sweep(j, carry):
        for s in seqs:
            sweep_block(s, block_rows(s, j), 0, lgf_ref, stf_scr, of_scr)
            sweep_block(s, block_rows(s, nb - 1 - j), 1, lgb_ref, stb_scr, ob_scr)
        return carry

    def finish(j, carry):
        pieces = []
        for s in seqs:
            keys = block_rows(s, j)
            kb, vb = ks_scr[keys, :].astype(BF16), v_ref[keys, :]
            for p in range(c // MIX_BLOCK):
                rows = pl.ds(pl.multiple_of(s * n + j * c + p * MIX_BLOCK, MIX_BLOCK), MIX_BLOCK)
                pieces.append((rows, slice(p * MIX_BLOCK, (p + 1) * MIX_BLOCK), kb, vb))
        ss = [_dot_nt(qs_scr[rows, :].astype(BF16), kb) * dm_scr[within, :]
              for rows, within, kb, _ in pieces]
        os = [_dot(sc.astype(BF16), vb) for sc, (_, _, _, vb) in zip(ss, pieces)]
        if not skip_inter:
            os = [o + of_scr[rows, :] + ob_scr[rows, :] for o, (rows, *_) in zip(os, pieces)]
        ys = [_rms(o, nw_ref[...]) for o in os]
        for y, (rows, *_) in zip(ys, pieces):
            g = g_ref[rows, :].astype(F32)
            o_ref[rows, :] = (y * (g * _sigmoid(g))).astype(o_ref.dtype)
        return carry

    lax.fori_loop(0, n_seqs * n // MIX_BLOCK, prep, 0)
    if has_state:
        stf_scr[...] = s0f_ref[...]
        stb_scr[...] = s0b_ref[...]
    elif not skip_inter:
        stf_scr[...] = jnp.zeros(stf_scr.shape, F32)
        stb_scr[...] = jnp.zeros(stb_scr.shape, F32)
    if emit_state or not skip_inter:
        lax.fori_loop(0, nb, sweep, 0, unroll=True)
    lax.fori_loop(0, nb, finish, 0, unroll=True)
    if emit_state:
        sf_ref[...] = stf_scr[...]
        sb_ref[...] = stb_scr[...]


def _retention(proj, n_seq, n, lg_f, lg_b, o_norm_w, o_idx, rope_tabs, state_f, state_b,
               emit_state, cast_jobs=()):
    rope = rope_tabs is not None
    has_state = state_f is not None
    nq = C_QK // C_DK
    c = min(RET_BLOCK, n)
    per_step = 1 if rope else max(1, ROW_TILE // n)
    assert n_seq % per_step == 0
    rows = per_step * n
    in_specs = [
        pl.BlockSpec((rows, C_DK), lambda h, b: (b, h)),
        pl.BlockSpec((rows, C_DK), lambda h, b: (b, nq + h)),
        pl.BlockSpec((rows, C_DV), lambda h, b: (b, 2 * C_QK // C_DV + h)),
        pl.BlockSpec((rows, C_DV), lambda h, b: (b, (2 * C_QK + C_V) // C_DV + h)),
        pl.BlockSpec((None, 1, C_DV), lambda h, b: (h, 0, 0)),
        pl.BlockSpec((None, 1, C_DV), lambda h, b: (h, 0, 0)),
        pl.BlockSpec((None, 1, C_DV), lambda h, b: (o_idx, 0, 0)),
    ]
    args = [proj] * 4 + [lg_f, lg_b, o_norm_w]
    if rope:
        in_specs += [pl.BlockSpec((n, C_DK // 2), lambda h, b: (0, 0))] * 2
        args += list(rope_tabs)
    if has_state:
        st_spec = pl.BlockSpec((per_step, None, None, C_DK, C_DV),
                               lambda h, b: (b, o_idx, h, 0, 0))
        in_specs += [st_spec, st_spec]
        args += [state_f, state_b]
    out_specs = [pl.BlockSpec((rows, C_DV), lambda h, b: (b, h))]
    out_shape = [jax.ShapeDtypeStruct((n_seq * n, C_V), BF16)]
    if emit_state:
        so = pl.BlockSpec((per_step, None, C_DK, C_DV), lambda h, b: (b, h, 0, 0))
        out_specs += [so, so]
        out_shape += [jax.ShapeDtypeStruct((n_seq, C_HEADS, C_DK, C_DV), F32)] * 2
    grid = (C_HEADS, n_seq // per_step)
    job_in, job_out, job_shape = _cast_job_specs(cast_jobs, grid)
    in_specs += job_in
    args += [job.src for job in cast_jobs]
    out_specs += job_out
    out_shape += job_shape
    return pl.pallas_call(
        functools.partial(_ret_kernel, n=n, n_seqs=per_step, rope=rope, has_state=has_state,
                          emit_state=emit_state, n_jobs=len(cast_jobs)),
        grid=grid,
        in_specs=in_specs,
        out_specs=out_specs,
        out_shape=out_shape,
        scratch_shapes=[pltpu.VMEM((rows, C_DK), F32), pltpu.VMEM((rows, C_DK), F32),
                        pltpu.VMEM((rows, C_DV), F32), pltpu.VMEM((rows, C_DV), F32),
                        pltpu.VMEM((per_step, C_DK, C_DV), F32),
                        pltpu.VMEM((per_step, C_DK, C_DV), F32),
                        pltpu.VMEM((c, c), F32), pltpu.VMEM((2, c, C_DK), F32),
                        pltpu.VMEM((2, c, C_DK), F32)],
        compiler_params=_cparams("arbitrary", "arbitrary"),
        name="retention",
    )(*args)


def _rope_tables(n_tokens, head_dim):
    rows = n_tokens // GRID_W
    row = jnp.repeat(jnp.arange(rows, dtype=F32), GRID_W)
    col = jnp.tile(jnp.arange(GRID_W, dtype=F32), rows)
    quarter = head_dim // 4
    inv_freq = jnp.power(ROPE_BASE, -jnp.arange(quarter, dtype=F32) / quarter)
    ang = jnp.concatenate([row[:, None] * inv_freq, col[:, None] * inv_freq], axis=-1)
    return jnp.cos(ang), jnp.sin(ang)


def kernel(x_prompt, x_sample, cache_attn_k, cache_attn_v, state_hgrn_fwd, state_hgrn_bwd,
           state_ret_fwd, state_ret_bwd, c, c_ctx, w_mod, b_mod, norm_mix_w, norm_ffn_w,
           w_in_even, w_out_even, attn_q_norm_w, attn_k_norm_w, hgrn_lb, hgrn_o_norm_w,
           w_in_odd, w_out_odd, ret_decay_fwd, ret_decay_bwd, ret_o_norm_w,
           w_up, conv_w, conv_b, w_down):
    depth, d_model = norm_mix_w.shape
    bp, np_, _ = x_prompt.shape
    bs, ns, _ = x_sample.shape

    lb_all = jnp.cumsum(jax.nn.softmax(hgrn_lb.astype(F32), axis=0), axis=0)
    lg_f = jnp.broadcast_to(jax.nn.log_sigmoid(ret_decay_fwd.astype(F32))[:, :, None, None],
                            ret_decay_fwd.shape + (1, C_DV))
    lg_b = jnp.broadcast_to(jax.nn.log_sigmoid(ret_decay_bwd.astype(F32))[:, :, None, None],
                            ret_decay_bwd.shape + (1, C_DV))
    cos_a, sin_a = _rope_tables(ns, A_HEAD_DIM)
    rope_a = (jnp.concatenate([cos_a, cos_a], axis=1), jnp.concatenate([-sin_a, sin_a], axis=1))
    rope_c = _rope_tables(ns, C_DK)
    nmw = norm_mix_w.reshape(depth, 1, d_model)
    nfw = norm_ffn_w.reshape(depth, 1, d_model)
    qnw = attn_q_norm_w.reshape(-1, 1, A_HEAD_DIM)
    knw = attn_k_norm_w.reshape(-1, 1, A_HEAD_DIM)
    hnw = hgrn_o_norm_w.reshape(-1, 1, B_DV)
    rnw = ret_o_norm_w.reshape(-1, 1, C_DV)
    w_in = {0: w_in_even[0:1].astype(BF16)}
    w_out, w_gate_val, w_down_b = {}, {}, {}
    n_cond = 1 + bs
    pad = (-n_cond) % 8
    cond = jnp.concatenate([c_ctx[None, :], c, jnp.zeros((pad, d_model), F32)], axis=0)
    mod_all = _modulation(cond, w_mod, b_mod).reshape(depth, n_cond + pad, 6, d_model)

    groups = (
        dict(x=x_prompt.reshape(bp * np_, d_model), n_seq=bp, n=np_, row0=0,
             rows_per_cond=bp * np_, latent=False),
        dict(x=x_sample.reshape(bs * ns, d_model), n_seq=bs, n=ns, row0=1,
             rows_per_cond=ns, latent=True),
    )
    def w_in_f32(l):
        return (w_in_even, l // 2) if l % 2 == 0 else (w_in_odd, l // 2)

    results = []
    for grp in groups:
        y, n_seq, n = grp["x"], grp["n_seq"], grp["n"]
        row0, rpc, latent = grp["row0"], grp["rows_per_cond"], grp["latent"]
        new = dict(k=[], v=[], hf=[], hb=[], rf=[], rb=[])
        for l in range(depth):
            mod = mod_all[l]
            first = l not in w_out
            next_in = l + 1 < depth and l + 1 not in w_in
            jobs = [_CastJob(*w_in_f32(l + 1))] if next_in else []
            if l % 2 == 1 and first:
                jobs.append(_CastJob(w_up, l))
            proj, *copies = _norm_proj(y, mod, row0, rpc, nmw, l, w_in[l], 0, row=0, cast_jobs=jobs)
            if next_in:
                w_in[l + 1] = copies.pop(0)[None]
            if copies:
                w_gate_val[l] = copies.pop(0)[None]
            if l % 2 == 0:
                e = l // 2
                att = _attention(proj, n_seq, n, qnw, knw, e, rope_a if latent else None,
                                 cache_attn_k, cache_attn_v, emit_kv=not latent,
                                 cast_jobs=[_CastJob(w_out_even, e)] if first else ())
                lb = lb_all[e].reshape(B_HEADS, 1, B_DK)
                hg = _hgrn(proj, n_seq, n, lb, hnw, e,
                           state_hgrn_fwd if latent else None,
                           state_hgrn_bwd if latent else None, emit_state=not latent,
                           cast_jobs=[_CastJob(w_up, l), _CastJob(w_down, l)] if first else ())
                if first:
                    w_out[l], w_gate_val[l], w_down_b[l] = att[-1][None], hg[-2][None], hg[-1][None]
                if not latent:
                    new["k"].append(att[1].reshape(n_seq, n, A_KV_HEADS, A_HEAD_DIM))
                    new["v"].append(att[2].reshape(n_seq, n, A_KV_HEADS, A_HEAD_DIM))
                    new["hf"].append(hg[1])
                    new["hb"].append(hg[2])
                mixed, out_jobs = [att[0], hg[0]], []
            else:
                o = l // 2
                rt = _retention(proj, n_seq, n, lg_f[o], lg_b[o], rnw, o,
                                rope_c if latent else None,
                                state_ret_fwd if latent else None,
                                state_ret_bwd if latent else None, emit_state=not latent,
                                cast_jobs=[_CastJob(w_out_odd, o)] if first else ())
                if first:
                    w_out[l] = rt[-1][None]
                if not latent:
                    new["rf"].append(rt[1])
                    new["rb"].append(rt[2])
                mixed, out_jobs = [rt[0]], ([_CastJob(w_down, l)] if first else [])
            y, *copies = _proj_res(mixed, w_out[l], 0, y, mod, row0, rpc, row=2, cast_jobs=out_jobs)
            if copies:
                w_down_b[l] = copies[0][None]
            act = _ffn_up(y, mod, row0, rpc, nfw, l, w_gate_val[l], 0, conv_w, conv_b, n)
            y, = _proj_res([act], w_down_b[l], 0, y, mod, row0, rpc, row=5)
        results.append((y.reshape(n_seq, n, d_model), new))

    (y_p, new), (y_s, _) = results
    stack = lambda xs: jnp.stack(xs, axis=1)
    return (y_p, y_s, stack(new["k"]), stack(new["v"]), stack(new["hf"]), stack(new["hb"]),
            stack(new["rf"]), stack(new["rb"]))
```

```python
import functools
from typing import NamedTuple

import jax
import jax.numpy as jnp
from jax import lax
from jax.experimental import pallas as pl
from jax.experimental.pallas import tpu as pltpu

F32 = jnp.float32
BF16 = jnp.bfloat16

GRID_W = 64
A_HEADS = 8
A_KV_HEADS = 2
A_HEAD_DIM = 128
ROPE_BASE = 10000.0
B_HEADS = 8
B_DK = 128
B_DV = 128
C_HEADS = 8
C_DK = 256
C_DV = 512
EPS = 1e-6

A_Q = A_HEADS * A_HEAD_DIM
A_KV = A_KV_HEADS * A_HEAD_DIM
A_GROUP = A_HEADS // A_KV_HEADS
B_QK = B_HEADS * B_DK
B_V = B_HEADS * B_DV
C_QK = C_HEADS * C_DK
C_V = C_HEADS * C_DV

VMEM_LIMIT_BYTES = 56 * 1024 * 1024
ROW_TILE = 1024
COL_TILE = 512
MIX_BLOCK = 128
RET_BLOCK = 256
RET_MIN_SEQS = 2
NORM_ROWS = 16
HGRN_BASE = 16
HGRN_HEADS_PER_STEP = 4
EXP_CLAMP = 80.0


def _cparams(*sem):
    return pltpu.CompilerParams(dimension_semantics=sem, vmem_limit_bytes=VMEM_LIMIT_BYTES)


def _dot(a, b):
    return jnp.dot(a, b, preferred_element_type=F32)


def _dot_nt(a, b):
    return lax.dot_general(a, b, (((1,), (1,)), ((), ())), preferred_element_type=F32)


def _dot_tn(a, b):
    return lax.dot_general(a, b, (((0,), (0,)), ((), ())), preferred_element_type=F32)


def _sigmoid(x):
    return 1.0 / (1.0 + jnp.exp(-x))


def _rms(x, w):
    return x * lax.rsqrt(jnp.mean(x * x, axis=-1, keepdims=True) + EPS) * w


class _CastJob(NamedTuple):
    src: jax.Array
    lead: int


def _cast_job_specs(jobs, grid):
    n_steps = grid[0] * grid[1]
    in_specs, out_specs, out_shape = [], [], []
    for job in jobs:
        _, r, c = job.src.shape
        n_blocks = max(nb for nb in range(1, n_steps + 1)
                       if r % nb == 0 and (r // nb) % 16 == 0)
        rows, rep = r // n_blocks, n_steps // n_blocks
        blk = lambda i, j, rep=rep, last=n_blocks - 1: jnp.minimum((i * grid[1] + j) // rep, last)
        in_specs.append(pl.BlockSpec((None, rows, c),
                                     lambda i, j, job=job, blk=blk: (job.lead, blk(i, j), 0)))
        out_specs.append(pl.BlockSpec((rows, c), lambda i, j, blk=blk: (blk(i, j), 0)))
        out_shape.append(jax.ShapeDtypeStruct((r, c), BF16))
    return in_specs, out_specs, out_shape


def _run_cast_jobs(src_refs, dst_refs):
    for src_ref, dst_ref in zip(src_refs, dst_refs):
        dst_ref[...] = src_ref[...].astype(BF16)


def _mod_kernel(c_ref, w_ref, b_ref, o_ref):
    c = c_ref[...]
    s = (c * _sigmoid(c)).astype(BF16)
    o_ref[...] = _dot(s, w_ref[...].astype(BF16)) + b_ref[...]


def _modulation(cond, w_mod, b_mod):
    n_layers, d, n_out = w_mod.shape
    r = cond.shape[0]
    tn = 1024
    return pl.pallas_call(
        _mod_kernel,
        grid=(n_layers, n_out // tn),
        in_specs=[
            pl.BlockSpec((r, d), lambda l, j: (0, 0)),
            pl.BlockSpec((None, d, tn), lambda l, j: (l, 0, j)),
            pl.BlockSpec((None, 1, tn), lambda l, j: (l, 0, j)),
        ],
        out_specs=pl.BlockSpec((None, r, tn), lambda l, j: (l, 0, j)),
        out_shape=jax.ShapeDtypeStruct((n_layers, r, n_out), F32),
        compiler_params=_cparams("arbitrary", "arbitrary"),
        name="modulation",
    )(cond, w_mod, b_mod.reshape(n_layers, 1, n_out))


def _norm_mod(x_ref, mod_ref, nw_ref, h_scr, row):
    gain = nw_ref[...] * (1.0 + mod_ref[row + 1:row + 2, :])
    shift = mod_ref[row:row + 1, :]

    def chunk(c, carry):
        rows = pl.ds(pl.multiple_of(c * NORM_ROWS, NORM_ROWS), NORM_ROWS)
        x = x_ref[rows, :]
        r = lax.rsqrt(jnp.mean(x * x, axis=-1, keepdims=True) + EPS)
        h_scr[rows, :] = (x * r * gain + shift).astype(BF16)
        return carry

    lax.fori_loop(0, x_ref.shape[0] // NORM_ROWS, chunk, 0, unroll=4)


def _norm_proj_kernel(*refs, row, n_jobs):
    x_ref, mod_ref, nw_ref, w_ref = refs[:4]
    job_src = refs[4:4 + n_jobs]
    o_ref = refs[4 + n_jobs]
    job_dst = refs[5 + n_jobs:5 + 2 * n_jobs]
    h_scr = refs[-1]
    _run_cast_jobs(job_src, job_dst)

    @pl.when(pl.program_id(1) == 0)
    def _():
        _norm_mod(x_ref, mod_ref, nw_ref, h_scr, row)

    o_ref[...] = _dot(h_scr[...], w_ref[...]).astype(o_ref.dtype)


def _norm_proj(x, mod, mod_row0, rows_per_cond, norm_w, layer, w, w_idx, *, row, cast_jobs=()):
    m, d = x.shape
    n_out = w.shape[-1]
    tm = ROW_TILE
    tn = 2 * COL_TILE if n_out % (2 * COL_TILE) == 0 else COL_TILE
    grid = (m // tm, n_out // tn)
    cond_of = lambda i: mod_row0 + (i * tm) // rows_per_cond
    job_in, job_out, job_shape = _cast_job_specs(cast_jobs, grid)
    return pl.pallas_call(
        functools.partial(_norm_proj_kernel, row=row, n_jobs=len(cast_jobs)),
        grid=grid,
        in_specs=[
            pl.BlockSpec((tm, d), lambda i, j: (i, 0)),
            pl.BlockSpec((None, 6, d), lambda i, j: (cond_of(i), 0, 0)),
            pl.BlockSpec((None, 1, d), lambda i, j: (layer, 0, 0)),
            pl.BlockSpec((None, d, tn), lambda i, j: (w_idx, 0, j)),
        ] + job_in,
        out_specs=[pl.BlockSpec((tm, tn), lambda i, j: (i, j))] + job_out,
        out_shape=[jax.ShapeDtypeStruct((m, n_out), BF16)] + job_shape,
        scratch_shapes=[pltpu.VMEM((tm, d), BF16)],
        compiler_params=_cparams("parallel", "arbitrary"),
        name="norm_proj",
    )(x, mod, norm_w, w, *[job.src for job in cast_jobs])


def _conv3(u, cw, cb, first, last):
    t = u.shape[0]
    left = jnp.where(first, 0.0, pltpu.roll(u, 1, axis=0))
    right = jnp.where(last, 0.0, pltpu.roll(u, t - 1, axis=0))
    return left * cw[0:1, :] + u * cw[1:2, :] + right * cw[2:3, :] + cb


def _ffn_up_kernel(x_ref, mod_ref, nw_ref, wa_ref, wv_ref, cwa_ref, cwv_ref, cba_ref, cbv_ref,
                   o_ref, h_scr, *, seq_len):
    @pl.when(pl.program_id(1) == 0)
    def _():
        _norm_mod(x_ref, mod_ref, nw_ref, h_scr, 3)

    h = h_scr[...]
    tm = h.shape[0]
    pos = lax.broadcasted_iota(jnp.int32, (tm, 1), 0) % seq_len
    first = pos == 0
    last = pos == seq_len - 1
    a = _conv3(_dot(h, wa_ref[...]), cwa_ref[...], cba_ref[...], first, last)
    v = _conv3(_dot(h, wv_ref[...]), cwv_ref[...], cbv_ref[...], first, last)
    o_ref[...] = (a * _sigmoid(a) * v).astype(o_ref.dtype)


def _ffn_up(x, mod, mod_row0, rows_per_cond, norm_w, layer, w_up, w_idx, conv_w, conv_b,
            seq_len):
    m, d = x.shape
    d_ff = w_up.shape[-1] // 2
    tm, tn = ROW_TILE, COL_TILE
    nj = d_ff // tn
    cond_of = lambda i: mod_row0 + (i * tm) // rows_per_cond
    conv_b3 = conv_b.reshape(conv_b.shape[0], 1, 2 * d_ff)
    return pl.pallas_call(
        functools.partial(_ffn_up_kernel, seq_len=seq_len),
        grid=(m // tm, nj),
        in_specs=[
            pl.BlockSpec((tm, d), lambda i, j: (i, 0)),
            pl.BlockSpec((None, 6, d), lambda i, j: (cond_of(i), 0, 0)),
            pl.BlockSpec((None, 1, d), lambda i, j: (layer, 0, 0)),
            pl.BlockSpec((None, d, tn), lambda i, j: (w_idx, 0, j)),
            pl.BlockSpec((None, d, tn), lambda i, j: (w_idx, 0, nj + j)),
            pl.BlockSpec((None, 3, tn), lambda i, j: (layer, 0, j)),
            pl.BlockSpec((None, 3, tn), lambda i, j: (layer, 0, nj + j)),
            pl.BlockSpec((None, 1, tn), lambda i, j: (layer, 0, j)),
            pl.BlockSpec((None, 1, tn), lambda i, j: (layer, 0, nj + j)),
        ],
        out_specs=pl.BlockSpec((tm, tn), lambda i, j: (i, j)),
        out_shape=jax.ShapeDtypeStruct((m, d_ff), BF16),
        scratch_shapes=[pltpu.VMEM((tm, d), BF16)],
        compiler_params=_cparams("parallel", "arbitrary"),
        name="ffn_up",
    )(x, mod, norm_w, w_up, w_up, conv_w, conv_w, conv_b3, conv_b3)


def _proj_res_kernel(*refs, n_in, row, n_jobs):
    a_refs = refs[:n_in]
    w_refs = refs[n_in:2 * n_in]
    y_ref, mod_ref = refs[2 * n_in:2 * n_in + 2]
    job_src = refs[2 * n_in + 2:2 * n_in + 2 + n_jobs]
    o_ref = refs[2 * n_in + 2 + n_jobs]
    job_dst = refs[2 * n_in + 3 + n_jobs:]
    _run_cast_jobs(job_src, job_dst)
    acc = _dot(a_refs[0][...], w_refs[0][...])
    for a_ref, w_ref in zip(a_refs[1:], w_refs[1:]):
        acc += _dot(a_ref[...], w_ref[...])
    o_ref[...] = y_ref[...] + mod_ref[row:row + 1, :] * acc


def _proj_res(acts, w, w_idx, y, mod, mod_row0, rows_per_cond, *, row, cast_jobs=()):
    m, d = y.shape
    tm, tn = ROW_TILE, COL_TILE
    n_in = len(acts)
    cond_of = lambda i: mod_row0 + (i * tm) // rows_per_cond
    in_specs = [pl.BlockSpec((tm, a.shape[1]), lambda i, j: (i, 0)) for a in acts]
    k_blk = acts[0].shape[1]
    assert all(a.shape[1] == k_blk for a in acts)
    for k in range(n_in):
        in_specs.append(pl.BlockSpec((None, k_blk, tn), lambda i, j, k=k: (w_idx, k, j)))
    in_specs += [
        pl.BlockSpec((tm, tn), lambda i, j: (i, j)),
        pl.BlockSpec((None, 6, tn), lambda i, j: (cond_of(i), 0, j)),
    ]
    grid = (m // tm, d // tn)
    job_in, job_out, job_shape = _cast_job_specs(cast_jobs, grid)
    return pl.pallas_call(
        functools.partial(_proj_res_kernel, n_in=n_in, row=row, n_jobs=len(cast_jobs)),
        grid=grid,
        in_specs=in_specs + job_in,
        out_specs=[pl.BlockSpec((tm, tn), lambda i, j: (i, j))] + job_out,
        out_shape=[jax.ShapeDtypeStruct((m, d), F32)] + job_shape,
        compiler_params=_cparams("parallel", "arbitrary"),
        name="proj_res",
    )(*acts, *([w] * n_in), y, mod, *[job.src for job in cast_jobs])


def _rope_half_roll(x, cos2, sin2):
    return x * cos2 + pltpu.roll(x, x.shape[1] // 2, axis=1) * sin2


def _attn_kernel(*refs, n, n_ctx, rope, emit_kv, tq, n_jobs):
    q_ref, k_ref, v_ref, qw_ref, kw_ref = refs[:5]
    pos = 5
    if rope:
        cos_ref, sin_ref, ck_ref, cv_ref = refs[pos:pos + 4]
        pos += 4
    job_src = refs[pos:pos + n_jobs]
    pos += n_jobs
    o_ref = refs[pos]
    pos += 1
    if emit_kv:
        nk_ref, nv_ref = refs[pos:pos + 2]
        pos += 2
    job_dst = refs[pos:pos + n_jobs]
    pos += n_jobs
    kall, vall = refs[pos:pos + 2]
    _run_cast_jobs(job_src, job_dst)

    kn = _rms(k_ref[...].astype(F32), kw_ref[...])
    vv = v_ref[...]
    if emit_kv:
        nk_ref[...] = kn
        nv_ref[...] = vv.astype(F32)
    if rope:
        kn = _rope_half_roll(kn, cos_ref[...], sin_ref[...])
        kall[0:n_ctx, :] = ck_ref[...].astype(BF16)
        vall[0:n_ctx, :] = cv_ref[...].astype(BF16)
    kall[n_ctx:n_ctx + n, :] = kn.astype(BF16)
    vall[n_ctx:n_ctx + n, :] = vv.astype(BF16)

    scale = A_HEAD_DIM ** -0.5

    def chunk(c, carry):
        r0 = pl.multiple_of(c * tq, tq)
        rows = pl.ds(r0, tq)
        heads = [slice(g * A_HEAD_DIM, (g + 1) * A_HEAD_DIM) for g in range(A_GROUP)]
        qs = []
        for cols in heads:
            qh = _rms(q_ref[rows, cols].astype(F32), qw_ref[...])
            if rope:
                qh = _rope_half_roll(qh, cos_ref[rows, :], sin_ref[rows, :])
            qs.append((qh * scale).astype(BF16))
        ss = [_dot_nt(qh, kall[...]) for qh in qs]
        ps = [jnp.exp(s - jnp.max(s, axis=-1, keepdims=True)) for s in ss]
        ls = [jnp.sum(p, axis=-1, keepdims=True) for p in ps]
        os = [_dot(p.astype(BF16), vall[...]) / l for p, l in zip(ps, ls)]
        for cols, o in zip(heads, os):
            o_ref[rows, cols] = o.astype(o_ref.dtype)
        return carry

    lax.fori_loop(0, n // tq, chunk, 0, unroll=True)


def _attention(proj, n_seq, n, q_norm_w, k_norm_w, e, rope_tabs, cache_k, cache_v, emit_kv,
               cast_jobs=()):
    rope = rope_tabs is not None
    n_ctx = cache_k.shape[2] if rope else 0
    hd = A_HEAD_DIM
    qcols = A_GROUP * hd
    in_specs = [
        pl.BlockSpec((n, qcols), lambda b, kv: (b, kv)),
        pl.BlockSpec((n, hd), lambda b, kv: (b, A_Q // hd + kv)),
        pl.BlockSpec((n, hd), lambda b, kv: (b, (A_Q + A_KV) // hd + kv)),
        pl.BlockSpec((None, 1, hd), lambda b, kv: (e, 0, 0)),
        pl.BlockSpec((None, 1, hd), lambda b, kv: (e, 0, 0)),
    ]
    args = [proj, proj, proj, q_norm_w, k_norm_w]
    if rope:
        cos2, sin2 = rope_tabs
        in_specs += [
            pl.BlockSpec((n, hd), lambda b, kv: (0, 0)),
            pl.BlockSpec((n, hd), lambda b, kv: (0, 0)),
            pl.BlockSpec((None, None, n_ctx, hd), lambda b, kv: (b, e, 0, kv)),
            pl.BlockSpec((None, None, n_ctx, hd), lambda b, kv: (b, e, 0, kv)),
        ]
        ck = cache_k.reshape(cache_k.shape[0], cache_k.shape[1], n_ctx, A_KV)
        cv = cache_v.reshape(cache_v.shape[0], cache_v.shape[1], n_ctx, A_KV)
        args += [cos2, sin2, ck, cv]
    out_specs = [pl.BlockSpec((n, qcols), lambda b, kv: (b, kv))]
    out_shape = [jax.ShapeDtypeStruct((n_seq * n, A_Q), BF16)]
    if emit_kv:
        out_specs += [pl.BlockSpec((n, hd), lambda b, kv: (b, kv))] * 2
        out_shape += [jax.ShapeDtypeStruct((n_seq * n, A_KV), F32)] * 2
    grid = (n_seq, A_KV_HEADS)
    job_in, job_out, job_shape = _cast_job_specs(cast_jobs, grid)
    in_specs += job_in
    args += [job.src for job in cast_jobs]
    out_specs += job_out
    out_shape += job_shape
    return pl.pallas_call(
        functools.partial(_attn_kernel, n=n, n_ctx=n_ctx, rope=rope, emit_kv=emit_kv,
                          tq=min(n, 256), n_jobs=len(cast_jobs)),
        grid=grid,
        in_specs=in_specs,
        out_specs=out_specs,
        out_shape=out_shape,
        scratch_shapes=[pltpu.VMEM((n_ctx + n, hd), BF16), pltpu.VMEM((n_ctx + n, hd), BF16)],
        compiler_params=_cparams("parallel", "arbitrary"),
        name="attention",
    )(*args)


def _gla_levels(c, rev):
    row = lax.broadcasted_iota(jnp.int32, (c, c), 0)
    col = lax.broadcasted_iota(jnp.int32, (c, c), 1)
    shift = HGRN_BASE.bit_length() - 1
    x = (row >> shift) ^ (col >> shift)
    lvl = jnp.zeros((c, c), jnp.int32)
    for l in range(1, (c // HGRN_BASE).bit_length()):
        lvl = jnp.where(x >= (1 << (l - 1)), l, lvl)
    causal = (col >= row) if rev else (col <= row)
    return jnp.where(causal, lvl, -1)


def _gla_blocks(chains):
    c, dk = chains[0]["q"].shape
    each = lambda fn: [fn(ch) for ch in chains]

    def split3(ch):
        hi = ch["lf"].astype(BF16)
        r1 = ch["lf"] - hi.astype(F32)
        mid = r1.astype(BF16)
        ch["parts"] = (hi, mid, (r1 - mid.astype(F32)).astype(BF16))

    def cumulate(ch):
        hi, mid, lo = ch["parts"]
        ch["b"] = _dot(ch["tri"], hi) + _dot(ch["tri"], mid) + _dot(ch["tri"], lo)
        ch["tot"] = ch["b"][0:1, :] if ch["rev"] else ch["b"][c - 1:c, :]

    def inter(ch):
        ch["o"] = _dot_nt((ch["q"] * jnp.exp(ch["b"])).astype(BF16), ch["st"].astype(BF16))
        ch["khat"] = (ch["k"] * jnp.exp(ch["tot"] - ch["b"])).astype(BF16)
        ch["vb"] = ch["v"].astype(BF16)

    def level0(ch):
        b3 = ch["b"].reshape(c // HGRN_BASE, HGRN_BASE, dk)
        mid_row = HGRN_BASE // 2 if ch["rev"] else HGRN_BASE // 2 - 1
        a = jnp.clip(b3 - b3[:, mid_row:mid_row + 1, :], -EXP_CLAMP, EXP_CLAMP).reshape(c, dk)
        p = _dot_nt((ch["q"] * jnp.exp(a)).astype(BF16), (ch["k"] * jnp.exp(-a)).astype(BF16))
        ch["scores"] = jnp.where(ch["lvl"] == 0, p, 0.0)

    def upper_level(ch, h, level):
        b3 = ch["b"].reshape(c // (2 * h), 2 * h, dk)
        ref_row = h if ch["rev"] else h - 1
        e = jnp.exp(-jnp.abs(b3 - b3[:, ref_row:ref_row + 1, :])).reshape(c, dk)
        p = _dot_nt((ch["q"] * e).astype(BF16), (ch["k"] * e).astype(BF16))
        ch["scores"] = jnp.where(ch["lvl"] == level, p, ch["scores"])

    def combine(ch):
        o = ch["o"] + _dot(ch["scores"].astype(BF16), ch["vb"])
        st_new = ch["st"] * jnp.exp(ch["tot"]) + _dot_tn(ch["vb"], ch["khat"])
        return o, st_new

    each(split3)
    each(cumulate)
    each(inter)
    each(level0)
    h, level = HGRN_BASE, 1
    while h < c:
        each(functools.partial(upper_level, h=h, level=level))
        h, level = 2 * h, level + 1
    return each(combine)


def _hgrn_kernel(*refs, n, has_state, emit_state, n_jobs):
    q_ref, i_ref, zf_ref, zb_ref, g_ref, lb_ref, nw_ref = refs[:7]
    pos = 7
    if has_state:
        s0f_ref, s0b_ref = refs[pos:pos + 2]
        pos += 2
    job_src = refs[pos:pos + n_jobs]
    pos += n_jobs
    o_ref = refs[pos]
    pos += 1
    if emit_state:
        sf_ref, sb_ref = refs[pos:pos + 2]
        pos += 2
    job_dst = refs[pos:pos + n_jobs]
    pos += n_jobs
    of_scr, ob_scr, st_scr = refs[pos:pos + 3]
    _run_cast_jobs(job_src, job_dst)

    c = MIX_BLOCK
    nb = n // c
    qscale = B_DK ** -0.5
    lvl_f = _gla_levels(c, False)
    lvl_b = _gla_levels(c, True)
    tri_f = jnp.where(lvl_f >= 0, 1.0, 0.0).astype(BF16)
    tri_b = jnp.where(lvl_b >= 0, 1.0, 0.0).astype(BF16)
    heads = [slice(hh * B_DK, (hh + 1) * B_DK) for hh in range(HGRN_HEADS_PER_STEP)]

    def chain(z_ref, rows, hh, rev):
        cols = heads[hh]
        lb = lb_ref[hh]
        f = lb + (1.0 - lb) * _sigmoid(z_ref[rows, cols].astype(F32))
        return dict(q=q_ref[rows, cols].astype(F32) * qscale, k=1.0 - f,
                    v=i_ref[rows, cols].astype(F32), lf=jnp.log(f),
                    st=st_scr[2 * hh + int(rev)], rev=rev,
                    lvl=lvl_b if rev else lvl_f, tri=tri_b if rev else tri_f)

    def step(j, carry):
        rows_f = pl.ds(pl.multiple_of(j * c, c), c)
        rows_b = pl.ds(pl.multiple_of((nb - 1 - j) * c, c), c)
        chains = []
        for hh in range(len(heads)):
            chains += [chain(zf_ref, rows_f, hh, False), chain(zb_ref, rows_b, hh, True)]
        outs = _gla_blocks(chains)
        for hh, cols in enumerate(heads):
            (o_f, st_f), (o_b, st_b) = outs[2 * hh], outs[2 * hh + 1]
            of_scr[rows_f, cols] = o_f
            ob_scr[rows_b, cols] = o_b
            st_scr[2 * hh] = st_f
            st_scr[2 * hh + 1] = st_b
        return carry

    def finish(j, carry):
        rows = pl.ds(pl.multiple_of(j * c, c), c)
        for cols in heads:
            y = _rms(of_scr[rows, cols] + ob_scr[rows, cols], nw_ref[...])
            g = g_ref[rows, cols].astype(F32)
            o_ref[rows, cols] = (y * (g * _sigmoid(g))).astype(o_ref.dtype)
        return carry

    for hh in range(len(heads)):
        if has_state:
            st_scr[2 * hh] = s0f_ref[hh].T
            st_scr[2 * hh + 1] = s0b_ref[hh].T
        else:
            st_scr[2 * hh] = jnp.zeros((B_DV, B_DK), F32)
            st_scr[2 * hh + 1] = jnp.zeros((B_DV, B_DK), F32)
    lax.fori_loop(0, nb, step, 0, unroll=2)
    lax.fori_loop(0, nb, finish, 0, unroll=2)
    if emit_state:
        for hh in range(len(heads)):
            sf_ref[hh] = st_scr[2 * hh].T
            sb_ref[hh] = st_scr[2 * hh + 1].T


def _hgrn(proj, n_seq, n, lb, o_norm_w, e, state_f, state_b, emit_state, cast_jobs=()):
    has_state = state_f is not None
    hp = HGRN_HEADS_PER_STEP
    d = B_DK * hp
    base = (A_Q + 2 * A_KV) // d
    col = lambda k: (lambda b, h: (b, base + k * (B_HEADS // hp) + h))
    in_specs = [pl.BlockSpec((n, d), col(k)) for k in range(5)]
    in_specs += [
        pl.BlockSpec((hp, 1, B_DK), lambda b, h: (h, 0, 0)),
        pl.BlockSpec((None, 1, B_DV), lambda b, h: (e, 0, 0)),
    ]
    args = [proj] * 5 + [lb, o_norm_w]
    if has_state:
        st_spec = pl.BlockSpec((None, None, hp, B_DK, B_DV), lambda b, h: (b, e, h, 0, 0))
        in_specs += [st_spec, st_spec]
        args += [state_f, state_b]
    out_specs = [pl.BlockSpec((n, d), lambda b, h: (b, h))]
    out_shape = [jax.ShapeDtypeStruct((n_seq * n, B_V), BF16)]
    if emit_state:
        so = pl.BlockSpec((None, hp, B_DK, B_DV), lambda b, h: (b, h, 0, 0))
        out_specs += [so, so]
        out_shape += [jax.ShapeDtypeStruct((n_seq, B_HEADS, B_DK, B_DV), F32)] * 2
    grid = (n_seq, B_HEADS // hp)
    job_in, job_out, job_shape = _cast_job_specs(cast_jobs, grid)
    in_specs += job_in
    args += [job.src for job in cast_jobs]
    out_specs += job_out
    out_shape += job_shape
    return pl.pallas_call(
        functools.partial(_hgrn_kernel, n=n, has_state=has_state, emit_state=emit_state,
                          n_jobs=len(cast_jobs)),
        grid=grid,
        in_specs=in_specs,
        out_specs=out_specs,
        out_shape=out_shape,
        scratch_shapes=[pltpu.VMEM((n, d), F32), pltpu.VMEM((n, d), F32),
                        pltpu.VMEM((2 * hp, B_DV, B_DK), F32)],
        compiler_params=_cparams("parallel", "arbitrary"),
        name="hgrn2",
    )(*args)


def _rope_split(x, cos, sin):
    half = x.shape[1] // 2
    x1, x2 = x[:, :half], x[:, half:]
    return jnp.concatenate([x1 * cos - x2 * sin, x1 * sin + x2 * cos], axis=1)


def _ret_kernel(*refs, n, n_seqs, rope, has_state, emit_state, n_jobs):
    q_ref, k_ref, v_ref, g_ref, lgf_ref, lgb_ref, nw_ref = refs[:7]
    pos = 7
    if rope:
        cos_ref, sin_ref = refs[pos:pos + 2]
        pos += 2
    if has_state:
        s0f_ref, s0b_ref = refs[pos:pos + 2]
        pos += 2
    job_src = refs[pos:pos + n_jobs]
    pos += n_jobs
    o_ref = refs[pos]
    pos += 1
    if emit_state:
        sf_ref, sb_ref = refs[pos:pos + 2]
        pos += 2
    job_dst = refs[pos:pos + n_jobs]
    pos += n_jobs
    _run_cast_jobs(job_src, job_dst)
    qs_scr, ks_scr, of_scr, ob_scr, stf_scr, stb_scr, dm_scr, qd_scr, kd_scr = refs[pos:pos + 9]

    c = min(RET_BLOCK, n)
    nb = n // c
    seqs = range(n_seqs)
    skip_inter = (not has_state) and nb == 1
    kscale = C_DK ** -0.5
    rowi = lax.broadcasted_iota(jnp.int32, (c, c), 0)
    coli = lax.broadcasted_iota(jnp.int32, (c, c), 1)
    rowq = lax.broadcasted_iota(jnp.int32, (c, C_DK), 0).astype(F32)

    def prep(j, carry):
        rows = pl.ds(pl.multiple_of(j * MIX_BLOCK, MIX_BLOCK), MIX_BLOCK)
        q = q_ref[rows, :].astype(F32)
        k = k_ref[rows, :].astype(F32)
        if rope:
            pos = pl.ds(pl.multiple_of((j % (n // MIX_BLOCK)) * MIX_BLOCK, MIX_BLOCK), MIX_BLOCK)
            q = _rope_split(q, cos_ref[pos, :], sin_ref[pos, :])
            k = _rope_split(k, cos_ref[pos, :], sin_ref[pos, :])
        qs_scr[rows, :] = q
        ks_scr[rows, :] = k * kscale
        return carry

    @pl.when(pl.program_id(1) == 0)
    def _():
        lgs = (lgf_ref[...], lgb_ref[...])
        dist = (rowi - coli).astype(F32)
        dm_scr[...] = (
            jnp.where(dist >= 0.0, jnp.exp(lgs[0][:, :c] * jnp.maximum(dist, 0.0)), 0.0)
            + jnp.where(dist <= 0.0, jnp.exp(lgs[1][:, :c] * jnp.maximum(-dist, 0.0)), 0.0))
        for d, rev in enumerate((False, True)):
            lgq = lgs[d][:, :C_DK]
            qd_scr[d] = jnp.exp(lgq * ((c - rowq) if rev else (rowq + 1.0)))
            kd_scr[d] = jnp.exp(lgq * (rowq if rev else (c - 1.0 - rowq)))

    def block_rows(s, j):
        return pl.ds(pl.multiple_of(s * n + j * c, c), c)

    def sweep_block(s, rows, d, lg_ref, st_scr, o_scr):
        vb = v_ref[rows, :]
        u = _dot_tn((ks_scr[rows, :] * kd_scr[d]).astype(BF16), vb)
        if skip_inter:
            st_scr[s] = u
        else:
            st = st_scr[s]
            o_scr[rows, :] = _dot((qs_scr[rows, :] * qd_scr[d]).astype(BF16), st.astype(BF16))
            st_scr[s] = st * jnp.exp(lg_ref[...] * float(c)) + u

    def sweep(j, carry):
        for s in seqs:
            sweep_block(s, block_rows(s, j), 0, lgf_ref, stf_scr, of_scr)
            sweep_block(s, block_rows(s, nb - 1 - j), 1, lgb_ref, stb_scr, ob_scr)
        return carry

    def finish(j, carry):
        pieces = []
        for s in seqs:
            keys = block_rows(s, j)
            kb, vb = ks_scr[keys, :].astype(BF16), v_ref[keys, :]
            for p in range(c // MIX_BLOCK):
                rows = pl.ds(pl.multiple_of(s * n + j * c + p * MIX_BLOCK, MIX_BLOCK), MIX_BLOCK)
                pieces.append((rows, slice(p * MIX_BLOCK, (p + 1) * MIX_BLOCK), kb, vb))
        ss = [_dot_nt(qs_scr[rows, :].astype(BF16), kb) * dm_scr[within, :]
              for rows, within, kb, _ in pieces]
        os = [_dot(sc.astype(BF16), vb) for sc, (_, _, _, vb) in zip(ss, pieces)]
        if not skip_inter:
            os = [o + of_scr[rows, :] + ob_scr[rows, :] for o, (rows, *_) in zip(os, pieces)]
        ys = [_rms(o, nw_ref[...]) for o in os]
        for y, (rows, *_) in zip(ys, pieces):
            g = g_ref[rows, :].astype(F32)
            o_ref[rows, :] = (y * (g * _sigmoid(g))).astype(o_ref.dtype)
        return carry

    lax.fori_loop(0, n_seqs * n // MIX_BLOCK, prep, 0)
    if has_state:
        stf_scr[...] = s0f_ref[...]
        stb_scr[...] = s0b_ref[...]
    elif not skip_inter:
        stf_scr[...] = jnp.zeros(stf_scr.shape, F32)
        stb_scr[...] = jnp.zeros(stb_scr.shape, F32)
    if emit_state or not skip_inter:
        lax.fori_loop(0, nb, sweep, 0, unroll=True)
    lax.fori_loop(0, nb, finish, 0, unroll=True)
    if emit_state:
        sf_ref[...] = stf_scr[...]
        sb_ref[...] = stb_scr[...]


def _retention(proj, n_seq, n, lg_f, lg_b, o_norm_w, o_idx, rope_tabs, state_f, state_b,
               emit_state, cast_jobs=()):
    rope = rope_tabs is not None
    has_state = state_f is not None
    nq = C_QK // C_DK
    c = min(RET_BLOCK, n)
    per_step = max(RET_MIN_SEQS, ROW_TILE // n)
    assert n_seq % per_step == 0
    rows = per_step * n
    in_specs = [
        pl.BlockSpec((rows, C_DK), lambda h, b: (b, h)),
        pl.BlockSpec((rows, C_DK), lambda h, b: (b, nq + h)),
        pl.BlockSpec((rows, C_DV), lambda h, b: (b, 2 * C_QK // C_DV + h)),
        pl.BlockSpec((rows, C_DV), lambda h, b: (b, (2 * C_QK + C_V) // C_DV + h)),
        pl.BlockSpec((None, 1, C_DV), lambda h, b: (h, 0, 0)),
        pl.BlockSpec((None, 1, C_DV), lambda h, b: (h, 0, 0)),
        pl.BlockSpec((None, 1, C_DV), lambda h, b: (o_idx, 0, 0)),
    ]
    args = [proj] * 4 + [lg_f, lg_b, o_norm_w]
    if rope:
        in_specs += [pl.BlockSpec((n, C_DK // 2), lambda h, b: (0, 0))] * 2
        args += list(rope_tabs)
    if has_state:
        st_spec = pl.BlockSpec((per_step, None, None, C_DK, C_DV),
                               lambda h, b: (b, o_idx, h, 0, 0))
        in_specs += [st_spec, st_spec]
        args += [state_f, state_b]
    out_specs = [pl.BlockSpec((rows, C_DV), lambda h, b: (b, h))]
    out_shape = [jax.ShapeDtypeStruct((n_seq * n, C_V), BF16)]
    if emit_state:
        so = pl.BlockSpec((per_step, None, C_DK, C_DV), lambda h, b: (b, h, 0, 0))
        out_specs += [so, so]
        out_shape += [jax.ShapeDtypeStruct((n_seq, C_HEADS, C_DK, C_DV), F32)] * 2
    grid = (C_HEADS, n_seq // per_step)
    job_in, job_out, job_shape = _cast_job_specs(cast_jobs, grid)
    in_specs += job_in
    args += [job.src for job in cast_jobs]
    out_specs += job_out
    out_shape += job_shape
    return pl.pallas_call(
        functools.partial(_ret_kernel, n=n, n_seqs=per_step, rope=rope, has_state=has_state,
                          emit_state=emit_state, n_jobs=len(cast_jobs)),
        grid=grid,
        in_specs=in_specs,
        out_specs=out_specs,
        out_shape=out_shape,
        scratch_shapes=[pltpu.VMEM((rows, C_DK), F32), pltpu.VMEM((rows, C_DK), F32),
                        pltpu.VMEM((rows, C_DV), F32), pltpu.VMEM((rows, C_DV), F32),
                        pltpu.VMEM((per_step, C_DK, C_DV), F32),
                        pltpu.VMEM((per_step, C_DK, C_DV), F32),
                        pltpu.VMEM((c, c), F32), pltpu.VMEM((2, c, C_DK), F32),
                        pltpu.VMEM((2, c, C_DK), F32)],
        compiler_params=_cparams("arbitrary", "arbitrary"),
        name="retention",
    )(*args)


def _rope_tables(n_tokens, head_dim):
    rows = n_tokens // GRID_W
    row = jnp.repeat(jnp.arange(rows, dtype=F32), GRID_W)
    col = jnp.tile(jnp.arange(GRID_W, dtype=F32), rows)
    quarter = head_dim // 4
    inv_freq = jnp.power(ROPE_BASE, -jnp.arange(quarter, dtype=F32) / quarter)
    ang = jnp.concatenate([row[:, None] * inv_freq, col[:, None] * inv_freq], axis=-1)
    return jnp.cos(ang), jnp.sin(ang)


def kernel(x_prompt, x_sample, cache_attn_k, cache_attn_v, state_hgrn_fwd, state_hgrn_bwd,
           state_ret_fwd, state_ret_bwd, c, c_ctx, w_mod, b_mod, norm_mix_w, norm_ffn_w,
           w_in_even, w_out_even, attn_q_norm_w, attn_k_norm_w, hgrn_lb, hgrn_o_norm_w,
           w_in_odd, w_out_odd, ret_decay_fwd, ret_decay_bwd, ret_o_norm_w,
           w_up, conv_w, conv_b, w_down):
    depth, d_model = norm_mix_w.shape
    bp, np_, _ = x_prompt.shape
    bs, ns, _ = x_sample.shape

    lb_all = jnp.cumsum(jax.nn.softmax(hgrn_lb.astype(F32), axis=0), axis=0)
    lg_f = jnp.broadcast_to(jax.nn.log_sigmoid(ret_decay_fwd.astype(F32))[:, :, None, None],
                            ret_decay_fwd.shape + (1, C_DV))
    lg_b = jnp.broadcast_to(jax.nn.log_sigmoid(ret_decay_bwd.astype(F32))[:, :, None, None],
                            ret_decay_bwd.shape + (1, C_DV))
    cos_a, sin_a = _rope_tables(ns, A_HEAD_DIM)
    rope_a = (jnp.concatenate([cos_a, cos_a], axis=1), jnp.concatenate([-sin_a, sin_a], axis=1))
    rope_c = _rope_tables(ns, C_DK)
    nmw = norm_mix_w.reshape(depth, 1, d_model)
    nfw = norm_ffn_w.reshape(depth, 1, d_model)
    qnw = attn_q_norm_w.reshape(-1, 1, A_HEAD_DIM)
    knw = attn_k_norm_w.reshape(-1, 1, A_HEAD_DIM)
    hnw = hgrn_o_norm_w.reshape(-1, 1, B_DV)
    rnw = ret_o_norm_w.reshape(-1, 1, C_DV)
    w_in = {0: w_in_even[0:1].astype(BF16)}
    w_out, w_gate_val, w_down_b = {}, {}, {}
    n_cond = 1 + bs
    pad = (-n_cond) % 8
    cond = jnp.concatenate([c_ctx[None, :], c, jnp.zeros((pad, d_model), F32)], axis=0)
    mod_all = _modulation(cond, w_mod, b_mod).reshape(depth, n_cond + pad, 6, d_model)

    groups = (
        dict(x=x_prompt.reshape(bp * np_, d_model), n_seq=bp, n=np_, row0=0,
             rows_per_cond=bp * np_, latent=False),
        dict(x=x_sample.reshape(bs * ns, d_model), n_seq=bs, n=ns, row0=1,
             rows_per_cond=ns, latent=True),
    )
    def w_in_f32(l):
        return (w_in_even, l // 2) if l % 2 == 0 else (w_in_odd, l // 2)

    results = []
    for grp in groups:
        y, n_seq, n = grp["x"], grp["n_seq"], grp["n"]
        row0, rpc, latent = grp["row0"], grp["rows_per_cond"], grp["latent"]
        new = dict(k=[], v=[], hf=[], hb=[], rf=[], rb=[])
        for l in range(depth):
            mod = mod_all[l]
            first = l not in w_out
            next_in = l + 1 < depth and l + 1 not in w_in
            jobs = [_CastJob(*w_in_f32(l + 1))] if next_in else []
            if l % 2 == 1 and first:
                jobs.append(_CastJob(w_up, l))
            proj, *copies = _norm_proj(y, mod, row0, rpc, nmw, l, w_in[l], 0, row=0, cast_jobs=jobs)
            if next_in:
                w_in[l + 1] = copies.pop(0)[None]
            if copies:
                w_gate_val[l] = copies.pop(0)[None]
            if l % 2 == 0:
                e = l // 2
                att = _attention(proj, n_seq, n, qnw, knw, e, rope_a if latent else None,
                                 cache_attn_k, cache_attn_v, emit_kv=not latent,
                                 cast_jobs=[_CastJob(w_out_even, e)] if first else ())
                lb = lb_all[e].reshape(B_HEADS, 1, B_DK)
                hg = _hgrn(proj, n_seq, n, lb, hnw, e,
                           state_hgrn_fwd if latent else None,
                           state_hgrn_bwd if latent else None, emit_state=not latent,
                           cast_jobs=[_CastJob(w_up, l), _CastJob(w_down, l)] if first else ())
                if first:
                    w_out[l], w_gate_val[l], w_down_b[l] = att[-1][None], hg[-2][None], hg[-1][None]
                if not latent:
                    new["k"].append(att[1].reshape(n_seq, n, A_KV_HEADS, A_HEAD_DIM))
                    new["v"].append(att[2].reshape(n_seq, n, A_KV_HEADS, A_HEAD_DIM))
                    new["hf"].append(hg[1])
                    new["hb"].append(hg[2])
                mixed, out_jobs = [att[0], hg[0]], []
            else:
                o = l // 2
                rt = _retention(proj, n_seq, n, lg_f[o], lg_b[o], rnw, o,
                                rope_c if latent else None,
                                state_ret_fwd if latent else None,
                                state_ret_bwd if latent else None, emit_state=not latent,
                                cast_jobs=[_CastJob(w_out_odd, o)] if first else ())
                if first:
                    w_out[l] = rt[-1][None]
                if not latent:
                    new["rf"].append(rt[1])
                    new["rb"].append(rt[2])
                mixed, out_jobs = [rt[0]], ([_CastJob(w_down, l)] if first else [])
            y, *copies = _proj_res(mixed, w_out[l], 0, y, mod, row0, rpc, row=2, cast_jobs=out_jobs)
            if copies:
                w_down_b[l] = copies[0][None]
            act = _ffn_up(y, mod, row0, rpc, nfw, l, w_gate_val[l], 0, conv_w, conv_b, n)
            y, = _proj_res([act], w_down_b[l], 0, y, mod, row0, rpc, row=5)
        results.append((y.reshape(n_seq, n, d_model), new))

    (y_p, new), (y_s, _) = results
    stack = lambda xs: jnp.stack(xs, axis=1)
    return (y_p, y_s, stack(new["k"]), stack(new["v"]), stack(new["hf"]), stack(new["hb"]),
            stack(new["rf"]), stack(new["rb"]))
```

```python
import functools
from typing import NamedTuple

import jax
import jax.numpy as jnp
from jax import lax
from jax.experimental import pallas as pl
from jax.experimental.pallas import tpu as pltpu

F32 = jnp.float32
BF16 = jnp.bfloat16

GRID_W = 64
A_HEADS = 8
A_KV_HEADS = 2
A_HEAD_DIM = 128
ROPE_BASE = 10000.0
B_HEADS = 8
B_DK = 128
B_DV = 128
C_HEADS = 8
C_DK = 256
C_DV = 512
EPS = 1e-6

A_Q = A_HEADS * A_HEAD_DIM
A_KV = A_KV_HEADS * A_HEAD_DIM
A_GROUP = A_HEADS // A_KV_HEADS
B_QK = B_HEADS * B_DK
B_V = B_HEADS * B_DV
C_QK = C_HEADS * C_DK
C_V = C_HEADS * C_DV

VMEM_LIMIT_BYTES = 56 * 1024 * 1024
ROW_TILE = 1024
COL_TILE = 512
ATTN_Q_ROWS = 256
MIX_BLOCK = 128
RET_BLOCK = 256
RET_MIN_SEQS = 2
NORM_ROWS = 16
HGRN_BASE = 16
HGRN_HEADS_PER_STEP = 4
EXP_CLAMP = 80.0


def _cparams(*sem):
    return pltpu.CompilerParams(dimension_semantics=sem, vmem_limit_bytes=VMEM_LIMIT_BYTES)


def _dot(a, b):
    return jnp.dot(a, b, preferred_element_type=F32)


def _dot_nt(a, b):
    return lax.dot_general(a, b, (((1,), (1,)), ((), ())), preferred_element_type=F32)


def _dot_tn(a, b):
    return lax.dot_general(a, b, (((0,), (0,)), ((), ())), preferred_element_type=F32)


def _sigmoid(x):
    return 1.0 / (1.0 + jnp.exp(-x))


def _rms(x, w):
    return x * lax.rsqrt(jnp.mean(x * x, axis=-1, keepdims=True) + EPS) * w


class _CastJob(NamedTuple):
    src: jax.Array
    lead: int


def _cast_job_specs(jobs, grid):
    n_steps = grid[0] * grid[1]
    in_specs, out_specs, out_shape = [], [], []
    for job in jobs:
        _, r, c = job.src.shape
        n_blocks = max(nb for nb in range(1, n_steps + 1)
                       if r % nb == 0 and (r // nb) % 16 == 0)
        rows, rep = r // n_blocks, n_steps // n_blocks
        blk = lambda i, j, rep=rep, last=n_blocks - 1: jnp.minimum((i * grid[1] + j) // rep, last)
        in_specs.append(pl.BlockSpec((None, rows, c),
                                     lambda i, j, job=job, blk=blk: (job.lead, blk(i, j), 0)))
        out_specs.append(pl.BlockSpec((rows, c), lambda i, j, blk=blk: (blk(i, j), 0)))
        out_shape.append(jax.ShapeDtypeStruct((r, c), BF16))
    return in_specs, out_specs, out_shape


def _run_cast_jobs(src_refs, dst_refs):
    for src_ref, dst_ref in zip(src_refs, dst_refs):
        dst_ref[...] = src_ref[...].astype(BF16)


def _mod_kernel(c_ref, w_ref, b_ref, o_ref):
    c = c_ref[...]
    s = (c * _sigmoid(c)).astype(BF16)
    o_ref[...] = _dot(s, w_ref[...].astype(BF16)) + b_ref[...]


def _modulation(cond, w_mod, b_mod):
    n_layers, d, n_out = w_mod.shape
    r = cond.shape[0]
    tn = 4 * COL_TILE
    return pl.pallas_call(
        _mod_kernel,
        grid=(n_layers, n_out // tn),
        in_specs=[
            pl.BlockSpec((r, d), lambda l, j: (0, 0)),
            pl.BlockSpec((None, d, tn), lambda l, j: (l, 0, j)),
            pl.BlockSpec((None, 1, tn), lambda l, j: (l, 0, j)),
        ],
        out_specs=pl.BlockSpec((None, r, tn), lambda l, j: (l, 0, j)),
        out_shape=jax.ShapeDtypeStruct((n_layers, r, n_out), F32),
        compiler_params=_cparams("arbitrary", "arbitrary"),
        name="modulation",
    )(cond, w_mod, b_mod.reshape(n_layers, 1, n_out))


def _norm_mod(x_ref, mod_ref, nw_ref, h_scr, row):
    gain = nw_ref[...] * (1.0 + mod_ref[row + 1:row + 2, :])
    shift = mod_ref[row:row + 1, :]

    def chunk(c, carry):
        rows = pl.ds(pl.multiple_of(c * NORM_ROWS, NORM_ROWS), NORM_ROWS)
        x = x_ref[rows, :]
        r = lax.rsqrt(jnp.mean(x * x, axis=-1, keepdims=True) + EPS)
        h_scr[rows, :] = (x * r * gain + shift).astype(BF16)
        return carry

    lax.fori_loop(0, x_ref.shape[0] // NORM_ROWS, chunk, 0, unroll=4)


def _norm_proj_kernel(*refs, row, n_jobs):
    x_ref, mod_ref, nw_ref, w_ref = refs[:4]
    job_src = refs[4:4 + n_jobs]
    o_ref = refs[4 + n_jobs]
    job_dst = refs[5 + n_jobs:5 + 2 * n_jobs]
    h_scr = refs[-1]
    _run_cast_jobs(job_src, job_dst)

    @pl.when(pl.program_id(1) == 0)
    def _():
        _norm_mod(x_ref, mod_ref, nw_ref, h_scr, row)

    o_ref[...] = _dot(h_scr[...], w_ref[...]).astype(o_ref.dtype)


def _norm_proj(x, mod, mod_row0, rows_per_cond, norm_w, layer, w, w_idx, *, row, cast_jobs=()):
    m, d = x.shape
    n_out = w.shape[-1]
    tm = ROW_TILE
    tn = 2 * COL_TILE if n_out % (2 * COL_TILE) == 0 else COL_TILE
    grid = (m // tm, n_out // tn)
    cond_of = lambda i: mod_row0 + (i * tm) // rows_per_cond
    job_in, job_out, job_shape = _cast_job_specs(cast_jobs, grid)
    return pl.pallas_call(
        functools.partial(_norm_proj_kernel, row=row, n_jobs=len(cast_jobs)),
        grid=grid,
        in_specs=[
            pl.BlockSpec((tm, d), lambda i, j: (i, 0)),
            pl.BlockSpec((None, 6, d), lambda i, j: (cond_of(i), 0, 0)),
            pl.BlockSpec((None, 1, d), lambda i, j: (layer, 0, 0)),
            pl.BlockSpec((None, d, tn), lambda i, j: (w_idx, 0, j)),
        ] + job_in,
        out_specs=[pl.BlockSpec((tm, tn), lambda i, j: (i, j))] + job_out,
        out_shape=[jax.ShapeDtypeStruct((m, n_out), BF16)] + job_shape,
        scratch_shapes=[pltpu.VMEM((tm, d), BF16)],
        compiler_params=_cparams("parallel", "arbitrary"),
        name="norm_proj",
    )(x, mod, norm_w, w, *[job.src for job in cast_jobs])


def _conv3(u, cw, cb, first, last):
    t = u.shape[0]
    left = jnp.where(first, 0.0, pltpu.roll(u, 1, axis=0))
    right = jnp.where(last, 0.0, pltpu.roll(u, t - 1, axis=0))
    return left * cw[0:1, :] + u * cw[1:2, :] + right * cw[2:3, :] + cb


def _ffn_up_kernel(x_ref, mod_ref, nw_ref, wa_ref, wv_ref, cwa_ref, cwv_ref, cba_ref, cbv_ref,
                   o_ref, h_scr, *, seq_len):
    @pl.when(pl.program_id(1) == 0)
    def _():
        _norm_mod(x_ref, mod_ref, nw_ref, h_scr, 3)

    h = h_scr[...]
    tm = h.shape[0]
    pos = lax.broadcasted_iota(jnp.int32, (tm, 1), 0) % seq_len
    first = pos == 0
    last = pos == seq_len - 1
    a = _conv3(_dot(h, wa_ref[...]), cwa_ref[...], cba_ref[...], first, last)
    v = _conv3(_dot(h, wv_ref[...]), cwv_ref[...], cbv_ref[...], first, last)
    o_ref[...] = (a * _sigmoid(a) * v).astype(o_ref.dtype)


def _ffn_up(x, mod, mod_row0, rows_per_cond, norm_w, layer, w_up, w_idx, conv_w, conv_b,
            seq_len):
    m, d = x.shape
    d_ff = w_up.shape[-1] // 2
    tm, tn = ROW_TILE, COL_TILE
    nj = d_ff // tn
    cond_of = lambda i: mod_row0 + (i * tm) // rows_per_cond
    conv_b3 = conv_b.reshape(conv_b.shape[0], 1, 2 * d_ff)
    return pl.pallas_call(
        functools.partial(_ffn_up_kernel, seq_len=seq_len),
        grid=(m // tm, nj),
        in_specs=[
            pl.BlockSpec((tm, d), lambda i, j: (i, 0)),
            pl.BlockSpec((None, 6, d), lambda i, j: (cond_of(i), 0, 0)),
            pl.BlockSpec((None, 1, d), lambda i, j: (layer, 0, 0)),
            pl.BlockSpec((None, d, tn), lambda i, j: (w_idx, 0, j)),
            pl.BlockSpec((None, d, tn), lambda i, j: (w_idx, 0, nj + j)),
            pl.BlockSpec((None, 3, tn), lambda i, j: (layer, 0, j)),
            pl.BlockSpec((None, 3, tn), lambda i, j: (layer, 0, nj + j)),
            pl.BlockSpec((None, 1, tn), lambda i, j: (layer, 0, j)),
            pl.BlockSpec((None, 1, tn), lambda i, j: (layer, 0, nj + j)),
        ],
        out_specs=pl.BlockSpec((tm, tn), lambda i, j: (i, j)),
        out_shape=jax.ShapeDtypeStruct((m, d_ff), BF16),
        scratch_shapes=[pltpu.VMEM((tm, d), BF16)],
        compiler_params=_cparams("parallel", "arbitrary"),
        name="ffn_up",
    )(x, mod, norm_w, w_up, w_up, conv_w, conv_w, conv_b3, conv_b3)


def _proj_res_kernel(*refs, n_in, row, n_jobs):
    a_refs = refs[:n_in]
    w_refs = refs[n_in:2 * n_in]
    y_ref, mod_ref = refs[2 * n_in:2 * n_in + 2]
    job_src = refs[2 * n_in + 2:2 * n_in + 2 + n_jobs]
    o_ref = refs[2 * n_in + 2 + n_jobs]
    job_dst = refs[2 * n_in + 3 + n_jobs:]
    _run_cast_jobs(job_src, job_dst)
    acc = _dot(a_refs[0][...], w_refs[0][...])
    for a_ref, w_ref in zip(a_refs[1:], w_refs[1:]):
        acc += _dot(a_ref[...], w_ref[...])
    o_ref[...] = y_ref[...] + mod_ref[row:row + 1, :] * acc


def _proj_res(acts, w, w_idx, y, mod, mod_row0, rows_per_cond, *, row, cast_jobs=()):
    m, d = y.shape
    tm, tn = ROW_TILE, COL_TILE
    n_in = len(acts)
    cond_of = lambda i: mod_row0 + (i * tm) // rows_per_cond
    in_specs = [pl.BlockSpec((tm, a.shape[1]), lambda i, j: (i, 0)) for a in acts]
    k_blk = acts[0].shape[1]
    assert all(a.shape[1] == k_blk for a in acts)
    for k in range(n_in):
        in_specs.append(pl.BlockSpec((None, k_blk, tn), lambda i, j, k=k: (w_idx, k, j)))
    in_specs += [
        pl.BlockSpec((tm, tn), lambda i, j: (i, j)),
        pl.BlockSpec((None, 6, tn), lambda i, j: (cond_of(i), 0, j)),
    ]
    grid = (m // tm, d // tn)
    job_in, job_out, job_shape = _cast_job_specs(cast_jobs, grid)
    return pl.pallas_call(
        functools.partial(_proj_res_kernel, n_in=n_in, row=row, n_jobs=len(cast_jobs)),
        grid=grid,
        in_specs=in_specs + job_in,
        out_specs=[pl.BlockSpec((tm, tn), lambda i, j: (i, j))] + job_out,
        out_shape=[jax.ShapeDtypeStruct((m, d), F32)] + job_shape,
        compiler_params=_cparams("parallel", "arbitrary"),
        name="proj_res",
    )(*acts, *([w] * n_in), y, mod, *[job.src for job in cast_jobs])


def _rope_half_roll(x, cos2, sin2):
    return x * cos2 + pltpu.roll(x, x.shape[1] // 2, axis=1) * sin2


def _attn_kernel(*refs, n, n_ctx, rope, emit_kv, tq, n_jobs):
    q_ref, k_ref, v_ref, qw_ref, kw_ref = refs[:5]
    pos = 5
    if rope:
        cos_ref, sin_ref, ck_ref, cv_ref = refs[pos:pos + 4]
        pos += 4
    job_src = refs[pos:pos + n_jobs]
    pos += n_jobs
    o_ref = refs[pos]
    pos += 1
    if emit_kv:
        nk_ref, nv_ref = refs[pos:pos + 2]
        pos += 2
    job_dst = refs[pos:pos + n_jobs]
    pos += n_jobs
    kall, vall = refs[pos:pos + 2]
    _run_cast_jobs(job_src, job_dst)

    kn = _rms(k_ref[...].astype(F32), kw_ref[...])
    vv = v_ref[...]
    if emit_kv:
        nk_ref[...] = kn
        nv_ref[...] = vv.astype(F32)
    if rope:
        kn = _rope_half_roll(kn, cos_ref[...], sin_ref[...])
        kall[0:n_ctx, :] = ck_ref[...].astype(BF16)
        vall[0:n_ctx, :] = cv_ref[...].astype(BF16)
    kall[n_ctx:n_ctx + n, :] = kn.astype(BF16)
    vall[n_ctx:n_ctx + n, :] = vv.astype(BF16)

    scale = A_HEAD_DIM ** -0.5

    def chunk(c, carry):
        r0 = pl.multiple_of(c * tq, tq)
        rows = pl.ds(r0, tq)
        heads = [slice(g * A_HEAD_DIM, (g + 1) * A_HEAD_DIM) for g in range(A_GROUP)]
        qs = []
        for cols in heads:
            qh = _rms(q_ref[rows, cols].astype(F32), qw_ref[...])
            if rope:
                qh = _rope_half_roll(qh, cos_ref[rows, :], sin_ref[rows, :])
            qs.append((qh * scale).astype(BF16))
        ss = [_dot_nt(qh, kall[...]) for qh in qs]
        ps = [jnp.exp(s - jnp.max(s, axis=-1, keepdims=True)) for s in ss]
        ls = [jnp.sum(p, axis=-1, keepdims=True) for p in ps]
        os = [_dot(p.astype(BF16), vall[...]) / l for p, l in zip(ps, ls)]
        for cols, o in zip(heads, os):
            o_ref[rows, cols] = o.astype(o_ref.dtype)
        return carry

    lax.fori_loop(0, n // tq, chunk, 0, unroll=True)


def _attention(proj, n_seq, n, q_norm_w, k_norm_w, e, rope_tabs, cache_k, cache_v, emit_kv,
               cast_jobs=()):
    rope = rope_tabs is not None
    n_ctx = cache_k.shape[2] if rope else 0
    hd = A_HEAD_DIM
    qcols = A_GROUP * hd
    in_specs = [
        pl.BlockSpec((n, qcols), lambda b, kv: (b, kv)),
        pl.BlockSpec((n, hd), lambda b, kv: (b, A_Q // hd + kv)),
        pl.BlockSpec((n, hd), lambda b, kv: (b, (A_Q + A_KV) // hd + kv)),
        pl.BlockSpec((None, 1, hd), lambda b, kv: (e, 0, 0)),
        pl.BlockSpec((None, 1, hd), lambda b, kv: (e, 0, 0)),
    ]
    args = [proj, proj, proj, q_norm_w, k_norm_w]
    if rope:
        cos2, sin2 = rope_tabs
        in_specs += [
            pl.BlockSpec((n, hd), lambda b, kv: (0, 0)),
            pl.BlockSpec((n, hd), lambda b, kv: (0, 0)),
            pl.BlockSpec((None, None, n_ctx, hd), lambda b, kv: (b, e, 0, kv)),
            pl.BlockSpec((None, None, n_ctx, hd), lambda b, kv: (b, e, 0, kv)),
        ]
        ck = cache_k.reshape(cache_k.shape[0], cache_k.shape[1], n_ctx, A_KV)
        cv = cache_v.reshape(cache_v.shape[0], cache_v.shape[1], n_ctx, A_KV)
        args += [cos2, sin2, ck, cv]
    out_specs = [pl.BlockSpec((n, qcols), lambda b, kv: (b, kv))]
    out_shape = [jax.ShapeDtypeStruct((n_seq * n, A_Q), BF16)]
    if emit_kv:
        out_specs += [pl.BlockSpec((n, hd), lambda b, kv: (b, kv))] * 2
        out_shape += [jax.ShapeDtypeStruct((n_seq * n, A_KV), F32)] * 2
    grid = (n_seq, A_KV_HEADS)
    job_in, job_out, job_shape = _cast_job_specs(cast_jobs, grid)
    in_specs += job_in
    args += [job.src for job in cast_jobs]
    out_specs += job_out
    out_shape += job_shape
    return pl.pallas_call(
        functools.partial(_attn_kernel, n=n, n_ctx=n_ctx, rope=rope, emit_kv=emit_kv,
                          tq=min(n, ATTN_Q_ROWS), n_jobs=len(cast_jobs)),
        grid=grid,
        in_specs=in_specs,
        out_specs=out_specs,
        out_shape=out_shape,
        scratch_shapes=[pltpu.VMEM((n_ctx + n, hd), BF16), pltpu.VMEM((n_ctx + n, hd), BF16)],
        compiler_params=_cparams("parallel", "arbitrary"),
        name="attention",
    )(*args)


def _gla_levels(c, rev):
    row = lax.broadcasted_iota(jnp.int32, (c, c), 0)
    col = lax.broadcasted_iota(jnp.int32, (c, c), 1)
    shift = HGRN_BASE.bit_length() - 1
    x = (row >> shift) ^ (col >> shift)
    lvl = jnp.zeros((c, c), jnp.int32)
    for l in range(1, (c // HGRN_BASE).bit_length()):
        lvl = jnp.where(x >= (1 << (l - 1)), l, lvl)
    causal = (col >= row) if rev else (col <= row)
    return jnp.where(causal, lvl, -1)


def _gla_blocks(chains):
    c, dk = chains[0]["q"].shape
    each = lambda fn: [fn(ch) for ch in chains]

    def split3(ch):
        hi = ch["lf"].astype(BF16)
        r1 = ch["lf"] - hi.astype(F32)
        mid = r1.astype(BF16)
        ch["parts"] = (hi, mid, (r1 - mid.astype(F32)).astype(BF16))

    def cumulate(ch):
        hi, mid, lo = ch["parts"]
        ch["b"] = _dot(ch["tri"], hi) + _dot(ch["tri"], mid) + _dot(ch["tri"], lo)
        ch["tot"] = ch["b"][0:1, :] if ch["rev"] else ch["b"][c - 1:c, :]

    def inter(ch):
        ch["o"] = _dot_nt((ch["q"] * jnp.exp(ch["b"])).astype(BF16), ch["st"].astype(BF16))
        ch["khat"] = (ch["k"] * jnp.exp(ch["tot"] - ch["b"])).astype(BF16)
        ch["vb"] = ch["v"].astype(BF16)

    def level0(ch):
        b3 = ch["b"].reshape(c // HGRN_BASE, HGRN_BASE, dk)
        mid_row = HGRN_BASE // 2 if ch["rev"] else HGRN_BASE // 2 - 1
        a = jnp.clip(b3 - b3[:, mid_row:mid_row + 1, :], -EXP_CLAMP, EXP_CLAMP).reshape(c, dk)
        p = _dot_nt((ch["q"] * jnp.exp(a)).astype(BF16), (ch["k"] * jnp.exp(-a)).astype(BF16))
        ch["scores"] = jnp.where(ch["lvl"] == 0, p, 0.0)

    def upper_level(ch, h, level):
        b3 = ch["b"].reshape(c // (2 * h), 2 * h, dk)
        ref_row = h if ch["rev"] else h - 1
        e = jnp.exp(-jnp.abs(b3 - b3[:, ref_row:ref_row + 1, :])).reshape(c, dk)
        p = _dot_nt((ch["q"] * e).astype(BF16), (ch["k"] * e).astype(BF16))
        ch["scores"] = jnp.where(ch["lvl"] == level, p, ch["scores"])

    def combine(ch):
        o = ch["o"] + _dot(ch["scores"].astype(BF16), ch["vb"])
        st_new = ch["st"] * jnp.exp(ch["tot"]) + _dot_tn(ch["vb"], ch["khat"])
        return o, st_new

    each(split3)
    each(cumulate)
    each(inter)
    each(level0)
    h, level = HGRN_BASE, 1
    while h < c:
        each(functools.partial(upper_level, h=h, level=level))
        h, level = 2 * h, level + 1
    return each(combine)


def _hgrn_kernel(*refs, n, has_state, emit_state, n_jobs):
    q_ref, i_ref, zf_ref, zb_ref, g_ref, lb_ref, nw_ref = refs[:7]
    pos = 7
    if has_state:
        s0f_ref, s0b_ref = refs[pos:pos + 2]
        pos += 2
    job_src = refs[pos:pos + n_jobs]
    pos += n_jobs
    o_ref = refs[pos]
    pos += 1
    if emit_state:
        sf_ref, sb_ref = refs[pos:pos + 2]
        pos += 2
    job_dst = refs[pos:pos + n_jobs]
    pos += n_jobs
    of_scr, ob_scr, st_scr = refs[pos:pos + 3]
    _run_cast_jobs(job_src, job_dst)

    c = MIX_BLOCK
    nb = n // c
    qscale = B_DK ** -0.5
    lvl_f = _gla_levels(c, False)
    lvl_b = _gla_levels(c, True)
    tri_f = jnp.where(lvl_f >= 0, 1.0, 0.0).astype(BF16)
    tri_b = jnp.where(lvl_b >= 0, 1.0, 0.0).astype(BF16)
    heads = [slice(hh * B_DK, (hh + 1) * B_DK) for hh in range(HGRN_HEADS_PER_STEP)]

    def chain(z_ref, rows, hh, rev):
        cols = heads[hh]
        lb = lb_ref[hh]
        f = lb + (1.0 - lb) * _sigmoid(z_ref[rows, cols].astype(F32))
        return dict(q=q_ref[rows, cols].astype(F32) * qscale, k=1.0 - f,
                    v=i_ref[rows, cols].astype(F32), lf=jnp.log(f),
                    st=st_scr[2 * hh + int(rev)], rev=rev,
                    lvl=lvl_b if rev else lvl_f, tri=tri_b if rev else tri_f)

    def step(j, carry):
        rows_f = pl.ds(pl.multiple_of(j * c, c), c)
        rows_b = pl.ds(pl.multiple_of((nb - 1 - j) * c, c), c)
        chains = []
        for hh in range(len(heads)):
            chains += [chain(zf_ref, rows_f, hh, False), chain(zb_ref, rows_b, hh, True)]
        outs = _gla_blocks(chains)
        for hh, cols in enumerate(heads):
            (o_f, st_f), (o_b, st_b) = outs[2 * hh], outs[2 * hh + 1]
            of_scr[rows_f, cols] = o_f
            ob_scr[rows_b, cols] = o_b
            st_scr[2 * hh] = st_f
            st_scr[2 * hh + 1] = st_b
        return carry

    def finish(j, carry):
        rows = pl.ds(pl.multiple_of(j * c, c), c)
        for cols in heads:
            y = _rms(of_scr[rows, cols] + ob_scr[rows, cols], nw_ref[...])
            g = g_ref[rows, cols].astype(F32)
            o_ref[rows, cols] = (y * (g * _sigmoid(g))).astype(o_ref.dtype)
        return carry

    for hh in range(len(heads)):
        if has_state:
            st_scr[2 * hh] = s0f_ref[hh].T
            st_scr[2 * hh + 1] = s0b_ref[hh].T
        else:
            st_scr[2 * hh] = jnp.zeros((B_DV, B_DK), F32)
            st_scr[2 * hh + 1] = jnp.zeros((B_DV, B_DK), F32)
    lax.fori_loop(0, nb, step, 0, unroll=2)
    lax.fori_loop(0, nb, finish, 0, unroll=2)
    if emit_state:
        for hh in range(len(heads)):
            sf_ref[hh] = st_scr[2 * hh].T
            sb_ref[hh] = st_scr[2 * hh + 1].T


def _hgrn(proj, n_seq, n, lb, o_norm_w, e, state_f, state_b, emit_state, cast_jobs=()):
    has_state = state_f is not None
    hp = HGRN_HEADS_PER_STEP
    d = B_DK * hp
    base = (A_Q + 2 * A_KV) // d
    col = lambda k: (lambda b, h: (b, base + k * (B_HEADS // hp) + h))
    in_specs = [pl.BlockSpec((n, d), col(k)) for k in range(5)]
    in_specs += [
        pl.BlockSpec((hp, 1, B_DK), lambda b, h: (h, 0, 0)),
        pl.BlockSpec((None, 1, B_DV), lambda b, h: (e, 0, 0)),
    ]
    args = [proj] * 5 + [lb, o_norm_w]
    if has_state:
        st_spec = pl.BlockSpec((None, None, hp, B_DK, B_DV), lambda b, h: (b, e, h, 0, 0))
        in_specs += [st_spec, st_spec]
        args += [state_f, state_b]
    out_specs = [pl.BlockSpec((n, d), lambda b, h: (b, h))]
    out_shape = [jax.ShapeDtypeStruct((n_seq * n, B_V), BF16)]
    if emit_state:
        so = pl.BlockSpec((None, hp, B_DK, B_DV), lambda b, h: (b, h, 0, 0))
        out_specs += [so, so]
        out_shape += [jax.ShapeDtypeStruct((n_seq, B_HEADS, B_DK, B_DV), F32)] * 2
    grid = (n_seq, B_HEADS // hp)
    job_in, job_out, job_shape = _cast_job_specs(cast_jobs, grid)
    in_specs += job_in
    args += [job.src for job in cast_jobs]
    out_specs += job_out
    out_shape += job_shape
    return pl.pallas_call(
        functools.partial(_hgrn_kernel, n=n, has_state=has_state, emit_state=emit_state,
                          n_jobs=len(cast_jobs)),
        grid=grid,
        in_specs=in_specs,
        out_specs=out_specs,
        out_shape=out_shape,
        scratch_shapes=[pltpu.VMEM((n, d), F32), pltpu.VMEM((n, d), F32),
                        pltpu.VMEM((2 * hp, B_DV, B_DK), F32)],
        compiler_params=_cparams("parallel", "arbitrary"),
        name="hgrn2",
    )(*args)


def _rope_split(x, cos, sin):
    half = x.shape[1] // 2
    x1, x2 = x[:, :half], x[:, half:]
    return jnp.concatenate([x1 * cos - x2 * sin, x1 * sin + x2 * cos], axis=1)


def _ret_kernel(*refs, n, n_seqs, rope, has_state, emit_state, n_jobs):
    q_ref, k_ref, v_ref, g_ref, lgf_ref, lgb_ref, nw_ref = refs[:7]
    pos = 7
    if rope:
        cos_ref, sin_ref = refs[pos:pos + 2]
        pos += 2
    if has_state:
        s0f_ref, s0b_ref = refs[pos:pos + 2]
        pos += 2
    job_src = refs[pos:pos + n_jobs]
    pos += n_jobs
    o_ref = refs[pos]
    pos += 1
    if emit_state:
        sf_ref, sb_ref = refs[pos:pos + 2]
        pos += 2
    job_dst = refs[pos:pos + n_jobs]
    pos += n_jobs
    _run_cast_jobs(job_src, job_dst)
    qs_scr, ks_scr, of_scr, ob_scr, stf_scr, stb_scr, dm_scr, qd_scr, kd_scr = refs[pos:pos + 9]

    c = min(RET_BLOCK, n)
    nb = n // c
    seqs = range(n_seqs)
    skip_inter = (not has_state) and nb == 1
    kscale = C_DK ** -0.5
    rowi = lax.broadcasted_iota(jnp.int32, (c, c), 0)
    coli = lax.broadcasted_iota(jnp.int32, (c, c), 1)
    rowq = lax.broadcasted_iota(jnp.int32, (c, C_DK), 0).astype(F32)

    def prep(j, carry):
        rows = pl.ds(pl.multiple_of(j * MIX_BLOCK, MIX_BLOCK), MIX_BLOCK)
        q = q_ref[rows, :].astype(F32)
        k = k_ref[rows, :].astype(F32)
        if rope:
            pos = pl.ds(pl.multiple_of((j % (n // MIX_BLOCK)) * MIX_BLOCK, MIX_BLOCK), MIX_BLOCK)
            q = _rope_split(q, cos_ref[pos, :], sin_ref[pos, :])
            k = _rope_split(k, cos_ref[pos, :], sin_ref[pos, :])
        qs_scr[rows, :] = q
        ks_scr[rows, :] = k * kscale
        return carry

    @pl.when(pl.program_id(1) == 0)
    def _():
        lgs = (lgf_ref[...], lgb_ref[...])
        dist = (rowi - coli).astype(F32)
        dm_scr[...] = (
            jnp.where(dist >= 0.0, jnp.exp(lgs[0][:, :c] * jnp.maximum(dist, 0.0)), 0.0)
            + jnp.where(dist <= 0.0, jnp.exp(lgs[1][:, :c] * jnp.maximum(-dist, 0.0)), 0.0))
        for d, rev in enumerate((False, True)):
            lgq = lgs[d][:, :C_DK]
            qd_scr[d] = jnp.exp(lgq * ((c - rowq) if rev else (rowq + 1.0)))
            kd_scr[d] = jnp.exp(lgq * (rowq if rev else (c - 1.0 - rowq)))

    def block_rows(s, j):
        return pl.ds(pl.multiple_of(s * n + j * c, c), c)

    def sweep_block(s, rows, d, lg_ref, st_scr, o_scr):
        vb = v_ref[rows, :]
        u = _dot_tn((ks_scr[rows, :] * kd_scr[d]).astype(BF16), vb)
        if skip_inter:
            st_scr[s] = u
        else:
            st = st_scr[s]
            o_scr[rows, :] = _dot((qs_scr[rows, :] * qd_scr[d]).astype(BF16), st.astype(BF16))
            st_scr[s] = st * jnp.exp(lg_ref[...] * float(c)) + u

    def sweep(j, carry):
        for s in seqs:
            sweep_block(s, block_rows(s, j), 0, lgf_ref, stf_scr, of_scr)
            sweep_block(s, block_rows(s, nb - 1 - j), 1, lgb_ref, stb_scr, ob_scr)
        return carry

    def finish(j, carry):
        pieces = []
        for s in seqs:
            keys = block_rows(s, j)
            kb, vb = ks_scr[keys, :].astype(BF16), v_ref[keys, :]
            for p in range(c // MIX_BLOCK):
                rows = pl.ds(pl.multiple_of(s * n + j * c + p * MIX_BLOCK, MIX_BLOCK), MIX_BLOCK)
                pieces.append((rows, slice(p * MIX_BLOCK, (p + 1) * MIX_BLOCK), kb, vb))
        ss = [_dot_nt(qs_scr[rows, :].astype(BF16), kb) * dm_scr[within, :]
              for rows, within, kb, _ in pieces]
        os = [_dot(sc.astype(BF16), vb) for sc, (_, _, _, vb) in zip(ss, pieces)]
        if not skip_inter:
            os = [o + of_scr[rows, :] + ob_scr[rows, :] for o, (rows, *_) in zip(os, pieces)]
        ys = [_rms(o, nw_ref[...]) for o in os]
        for y, (rows, *_) in zip(ys, pieces):
            g = g_ref[rows, :].astype(F32)
            o_ref[rows, :] = (y * (g * _sigmoid(g))).astype(o_ref.dtype)
        return carry

    lax.fori_loop(0, n_seqs * n // MIX_BLOCK, prep, 0)
    if has_state:
        stf_scr[...] = s0f_ref[...]
        stb_scr[...] = s0b_ref[...]
    elif not skip_inter:
        stf_scr[...] = jnp.zeros(stf_scr.shape, F32)
        stb_scr[...] = jnp.zeros(stb_scr.shape, F32)
    if emit_state or not skip_inter:
        lax.fori_loop(0, nb, sweep, 0, unroll=True)
    lax.fori_loop(0, nb, finish, 0, unroll=True)
    if emit_state:
        sf_ref[...] = stf_scr[...]
        sb_ref[...] = stb_scr[...]


def _retention(proj, n_seq, n, lg_f, lg_b, o_norm_w, o_idx, rope_tabs, state_f, state_b,
               emit_state, cast_jobs=()):
    rope = rope_tabs is not None
    has_state = state_f is not None
    nq = C_QK // C_DK
    c = min(RET_BLOCK, n)
    per_step = max(RET_MIN_SEQS, ROW_TILE // n)
    assert n_seq % per_step == 0
    rows = per_step * n
    in_specs = [
        pl.BlockSpec((rows, C_DK), lambda h, b: (b, h)),
        pl.BlockSpec((rows, C_DK), lambda h, b: (b, nq + h)),
        pl.BlockSpec((rows, C_DV), lambda h, b: (b, 2 * C_QK // C_DV + h)),
        pl.BlockSpec((rows, C_DV), lambda h, b: (b, (2 * C_QK + C_V) // C_DV + h)),
        pl.BlockSpec((None, 1, C_DV), lambda h, b: (h, 0, 0)),
        pl.BlockSpec((None, 1, C_DV), lambda h, b: (h, 0, 0)),
        pl.BlockSpec((None, 1, C_DV), lambda h, b: (o_idx, 0, 0)),
    ]
    args = [proj] * 4 + [lg_f, lg_b, o_norm_w]
    if rope:
        in_specs += [pl.BlockSpec((n, C_DK // 2), lambda h, b: (0, 0))] * 2
        args += list(rope_tabs)
    if has_state:
        st_spec = pl.BlockSpec((per_step, None, None, C_DK, C_DV),
                               lambda h, b: (b, o_idx, h, 0, 0))
        in_specs += [st_spec, st_spec]
        args += [state_f, state_b]
    out_specs = [pl.BlockSpec((rows, C_DV), lambda h, b: (b, h))]
    out_shape = [jax.ShapeDtypeStruct((n_seq * n, C_V), BF16)]
    if emit_state:
        so = pl.BlockSpec((per_step, None, C_DK, C_DV), lambda h, b: (b, h, 0, 0))
        out_specs += [so, so]
        out_shape += [jax.ShapeDtypeStruct((n_seq, C_HEADS, C_DK, C_DV), F32)] * 2
    grid = (C_HEADS, n_seq // per_step)
    job_in, job_out, job_shape = _cast_job_specs(cast_jobs, grid)
    in_specs += job_in
    args += [job.src for job in cast_jobs]
    out_specs += job_out
    out_shape += job_shape
    return pl.pallas_call(
        functools.partial(_ret_kernel, n=n, n_seqs=per_step, rope=rope, has_state=has_state,
                          emit_state=emit_state, n_jobs=len(cast_jobs)),
        grid=grid,
        in_specs=in_specs,
        out_specs=out_specs,
        out_shape=out_shape,
        scratch_shapes=[pltpu.VMEM((rows, C_DK), F32), pltpu.VMEM((rows, C_DK), F32),
                        pltpu.VMEM((rows, C_DV), F32), pltpu.VMEM((rows, C_DV), F32),
                        pltpu.VMEM((per_step, C_DK, C_DV), F32),
                        pltpu.VMEM((per_step, C_DK, C_DV), F32),
                        pltpu.VMEM((c, c), F32), pltpu.VMEM((2, c, C_DK), F32),
                        pltpu.VMEM((2, c, C_DK), F32)],
        compiler_params=_cparams("arbitrary", "arbitrary"),
        name="retention",
    )(*args)


def _rope_tables(n_tokens, head_dim):
    rows = n_tokens // GRID_W
    row = jnp.repeat(jnp.arange(rows, dtype=F32), GRID_W)
    col = jnp.tile(jnp.arange(GRID_W, dtype=F32), rows)
    quarter = head_dim // 4
    inv_freq = jnp.power(ROPE_BASE, -jnp.arange(quarter, dtype=F32) / quarter)
    ang = jnp.concatenate([row[:, None] * inv_freq, col[:, None] * inv_freq], axis=-1)
    return jnp.cos(ang), jnp.sin(ang)


def kernel(x_prompt, x_sample, cache_attn_k, cache_attn_v, state_hgrn_fwd, state_hgrn_bwd,
           state_ret_fwd, state_ret_bwd, c, c_ctx, w_mod, b_mod, norm_mix_w, norm_ffn_w,
           w_in_even, w_out_even, attn_q_norm_w, attn_k_norm_w, hgrn_lb, hgrn_o_norm_w,
           w_in_odd, w_out_odd, ret_decay_fwd, ret_decay_bwd, ret_o_norm_w,
           w_up, conv_w, conv_b, w_down):
    depth, d_model = norm_mix_w.shape
    bp, np_, _ = x_prompt.shape
    bs, ns, _ = x_sample.shape

    lb_all = jnp.cumsum(jax.nn.softmax(hgrn_lb.astype(F32), axis=0), axis=0)
    lg_f = jnp.broadcast_to(jax.nn.log_sigmoid(ret_decay_fwd.astype(F32))[:, :, None, None],
                            ret_decay_fwd.shape + (1, C_DV))
    lg_b = jnp.broadcast_to(jax.nn.log_sigmoid(ret_decay_bwd.astype(F32))[:, :, None, None],
                            ret_decay_bwd.shape + (1, C_DV))
    cos_a, sin_a = _rope_tables(ns, A_HEAD_DIM)
    rope_a = (jnp.concatenate([cos_a, cos_a], axis=1), jnp.concatenate([-sin_a, sin_a], axis=1))
    rope_c = _rope_tables(ns, C_DK)
    nmw = norm_mix_w.reshape(depth, 1, d_model)
    nfw = norm_ffn_w.reshape(depth, 1, d_model)
    qnw = attn_q_norm_w.reshape(-1, 1, A_HEAD_DIM)
    knw = attn_k_norm_w.reshape(-1, 1, A_HEAD_DIM)
    hnw = hgrn_o_norm_w.reshape(-1, 1, B_DV)
    rnw = ret_o_norm_w.reshape(-1, 1, C_DV)
    w_in = {0: w_in_even[0:1].astype(BF16)}
    w_out, w_gate_val, w_down_b = {}, {}, {}
    n_cond = 1 + bs
    pad = (-n_cond) % 8
    cond = jnp.concatenate([c_ctx[None, :], c, jnp.zeros((pad, d_model), F32)], axis=0)
    mod_all = _modulation(cond, w_mod, b_mod).reshape(depth, n_cond + pad, 6, d_model)

    groups = (
        dict(x=x_prompt.reshape(bp * np_, d_model), n_seq=bp, n=np_, row0=0,
             rows_per_cond=bp * np_, latent=False),
        dict(x=x_sample.reshape(bs * ns, d_model), n_seq=bs, n=ns, row0=1,
             rows_per_cond=ns, latent=True),
    )
    def w_in_f32(l):
        return (w_in_even, l // 2) if l % 2 == 0 else (w_in_odd, l // 2)

    results = []
    for grp in groups:
        y, n_seq, n = grp["x"], grp["n_seq"], grp["n"]
        row0, rpc, latent = grp["row0"], grp["rows_per_cond"], grp["latent"]
        new = dict(k=[], v=[], hf=[], hb=[], rf=[], rb=[])
        for l in range(depth):
            mod = mod_all[l]
            first = l not in w_out
            next_in = l + 1 < depth and l + 1 not in w_in
            jobs = [_CastJob(*w_in_f32(l + 1))] if next_in else []
            if l % 2 == 1 and first:
                jobs.append(_CastJob(w_up, l))
            proj, *copies = _norm_proj(y, mod, row0, rpc, nmw, l, w_in[l], 0, row=0, cast_jobs=jobs)
            if next_in:
                w_in[l + 1] = copies.pop(0)[None]
            if copies:
                w_gate_val[l] = copies.pop(0)[None]
            if l % 2 == 0:
                e = l // 2
                att = _attention(proj, n_seq, n, qnw, knw, e, rope_a if latent else None,
                                 cache_attn_k, cache_attn_v, emit_kv=not latent,
                                 cast_jobs=[_CastJob(w_out_even, e)] if first else ())
                lb = lb_all[e].reshape(B_HEADS, 1, B_DK)
                hg = _hgrn(proj, n_seq, n, lb, hnw, e,
                           state_hgrn_fwd if latent else None,
                           state_hgrn_bwd if latent else None, emit_state=not latent,
                           cast_jobs=[_CastJob(w_up, l), _CastJob(w_down, l)] if first else ())
                if first:
                    w_out[l], w_gate_val[l], w_down_b[l] = att[-1][None], hg[-2][None], hg[-1][None]
                if not latent:
                    new["k"].append(att[1].reshape(n_seq, n, A_KV_HEADS, A_HEAD_DIM))
                    new["v"].append(att[2].reshape(n_seq, n, A_KV_HEADS, A_HEAD_DIM))
                    new["hf"].append(hg[1])
                    new["hb"].append(hg[2])
                mixed, out_jobs = [att[0], hg[0]], []
            else:
                o = l // 2
                rt = _retention(proj, n_seq, n, lg_f[o], lg_b[o], rnw, o,
                                rope_c if latent else None,
                                state_ret_fwd if latent else None,
                                state_ret_bwd if latent else None, emit_state=not latent,
                                cast_jobs=[_CastJob(w_out_odd, o)] if first else ())
                if first:
                    w_out[l] = rt[-1][None]
                if not latent:
                    new["rf"].append(rt[1])
                    new["rb"].append(rt[2])
                mixed, out_jobs = [rt[0]], ([_CastJob(w_down, l)] if first else [])
            y, *copies = _proj_res(mixed, w_out[l], 0, y, mod, row0, rpc, row=2, cast_jobs=out_jobs)
            if copies:
                w_down_b[l] = copies[0][None]
            act = _ffn_up(y, mod, row0, rpc, nfw, l, w_gate_val[l], 0, conv_w, conv_b, n)
            y, = _proj_res([act], w_down_b[l], 0, y, mod, row0, rpc, row=5)
        results.append((y.reshape(n_seq, n, d_model), new))

    (y_p, new), (y_s, _) = results
    stack = lambda xs: jnp.stack(xs, axis=1)
    return (y_p, y_s, stack(new["k"]), stack(new["v"]), stack(new["hf"]), stack(new["hb"]),
            stack(new["rf"]), stack(new["rb"]))
```

```python
import functools
from typing import NamedTuple

import jax
import jax.numpy as jnp
from jax import lax
from jax.experimental import pallas as pl
from jax.experimental.pallas import tpu as pltpu

F32 = jnp.float32
BF16 = jnp.bfloat16

GRID_W = 64
A_HEADS = 8
A_KV_HEADS = 2
A_HEAD_DIM = 128
ROPE_BASE = 10000.0
B_HEADS = 8
B_DK = 128
B_DV = 128
C_HEADS = 8
C_DK = 256
C_DV = 512
EPS = 1e-6

A_Q = A_HEADS * A_HEAD_DIM
A_KV = A_KV_HEADS * A_HEAD_DIM
A_GROUP = A_HEADS // A_KV_HEADS
B_QK = B_HEADS * B_DK
B_V = B_HEADS * B_DV
C_QK = C_HEADS * C_DK
C_V = C_HEADS * C_DV

VMEM_LIMIT_BYTES = 56 * 1024 * 1024
ROW_TILE = 1024
COL_TILE = 512
ATTN_Q_ROWS = 256
MIX_BLOCK = 128
RET_BLOCK = 256
RET_MIN_SEQS = 2
NORM_ROWS = 16
HGRN_BASE = 16
HGRN_HEADS_PER_STEP = 4
EXP_CLAMP = 80.0


def _cparams(*sem):
    return pltpu.CompilerParams(dimension_semantics=sem, vmem_limit_bytes=VMEM_LIMIT_BYTES)


def _dot(a, b):
    return jnp.dot(a, b, preferred_element_type=F32)


def _dot_nt(a, b):
    return lax.dot_general(a, b, (((1,), (1,)), ((), ())), preferred_element_type=F32)


def _dot_tn(a, b):
    return lax.dot_general(a, b, (((0,), (0,)), ((), ())), preferred_element_type=F32)


def _sigmoid(x):
    return 1.0 / (1.0 + jnp.exp(-x))


def _rms(x, w):
    return x * lax.rsqrt(jnp.mean(x * x, axis=-1, keepdims=True) + EPS) * w


class _CastJob(NamedTuple):
    src: jax.Array
    lead: int


def _cast_job_specs(jobs, grid):
    n_steps = grid[0] * grid[1]
    in_specs, out_specs, out_shape = [], [], []
    for job in jobs:
        _, r, c = job.src.shape
        n_blocks = max(nb for nb in range(1, n_steps + 1)
                       if r % nb == 0 and (r // nb) % 16 == 0)
        rows, rep = r // n_blocks, n_steps // n_blocks
        blk = lambda i, j, rep=rep, last=n_blocks - 1: jnp.minimum((i * grid[1] + j) // rep, last)
        in_specs.append(pl.BlockSpec((None, rows, c),
                                     lambda i, j, job=job, blk=blk: (job.lead, blk(i, j), 0)))
        out_specs.append(pl.BlockSpec((rows, c), lambda i, j, blk=blk: (blk(i, j), 0)))
        out_shape.append(jax.ShapeDtypeStruct((r, c), BF16))
    return in_specs, out_specs, out_shape


def _run_cast_jobs(src_refs, dst_refs):
    for src_ref, dst_ref in zip(src_refs, dst_refs):
        dst_ref[...] = src_ref[...].astype(BF16)


def _mod_kernel(c_ref, w_ref, b_ref, o_ref):
    c = c_ref[...]
    s = (c * _sigmoid(c)).astype(BF16)
    o_ref[...] = _dot(s, w_ref[...].astype(BF16)) + b_ref[...]


def _modulation(cond, w_mod, b_mod):
    n_layers, d, n_out = w_mod.shape
    r = cond.shape[0]
    tn = 2 * COL_TILE
    return pl.pallas_call(
        _mod_kernel,
        grid=(n_layers, n_out // tn),
        in_specs=[
            pl.BlockSpec((r, d), lambda l, j: (0, 0)),
            pl.BlockSpec((None, d, tn), lambda l, j: (l, 0, j)),
            pl.BlockSpec((None, 1, tn), lambda l, j: (l, 0, j)),
        ],
        out_specs=pl.BlockSpec((None, r, tn), lambda l, j: (l, 0, j)),
        out_shape=jax.ShapeDtypeStruct((n_layers, r, n_out), F32),
        compiler_params=_cparams("arbitrary", "arbitrary"),
        name="modulation",
    )(cond, w_mod, b_mod.reshape(n_layers, 1, n_out))


def _norm_mod(x_ref, mod_ref, nw_ref, h_scr, row):
    gain = nw_ref[...] * (1.0 + mod_ref[row + 1:row + 2, :])
    shift = mod_ref[row:row + 1, :]

    def chunk(c, carry):
        rows = pl.ds(pl.multiple_of(c * NORM_ROWS, NORM_ROWS), NORM_ROWS)
        x = x_ref[rows, :]
        r = lax.rsqrt(jnp.mean(x * x, axis=-1, keepdims=True) + EPS)
        h_scr[rows, :] = (x * r * gain + shift).astype(BF16)
        return carry

    lax.fori_loop(0, x_ref.shape[0] // NORM_ROWS, chunk, 0, unroll=4)


def _norm_proj_kernel(*refs, row, n_jobs):
    x_ref, mod_ref, nw_ref, w_ref = refs[:4]
    job_src = refs[4:4 + n_jobs]
    o_ref = refs[4 + n_jobs]
    job_dst = refs[5 + n_jobs:5 + 2 * n_jobs]
    h_scr = refs[-1]
    _run_cast_jobs(job_src, job_dst)

    @pl.when(pl.program_id(1) == 0)
    def _():
        _norm_mod(x_ref, mod_ref, nw_ref, h_scr, row)

    o_ref[...] = _dot(h_scr[...], w_ref[...]).astype(o_ref.dtype)


def _norm_proj(x, mod, mod_row0, rows_per_cond, norm_w, layer, w, w_idx, *, row, cast_jobs=()):
    m, d = x.shape
    n_out = w.shape[-1]
    tm = ROW_TILE
    tn = 2 * COL_TILE if n_out % (2 * COL_TILE) == 0 else COL_TILE
    grid = (m // tm, n_out // tn)
    cond_of = lambda i: mod_row0 + (i * tm) // rows_per_cond
    job_in, job_out, job_shape = _cast_job_specs(cast_jobs, grid)
    return pl.pallas_call(
        functools.partial(_norm_proj_kernel, row=row, n_jobs=len(cast_jobs)),
        grid=grid,
        in_specs=[
            pl.BlockSpec((tm, d), lambda i, j: (i, 0)),
            pl.BlockSpec((None, 6, d), lambda i, j: (cond_of(i), 0, 0)),
            pl.BlockSpec((None, 1, d), lambda i, j: (layer, 0, 0)),
            pl.BlockSpec((None, d, tn), lambda i, j: (w_idx, 0, j)),
        ] + job_in,
        out_specs=[pl.BlockSpec((tm, tn), lambda i, j: (i, j))] + job_out,
        out_shape=[jax.ShapeDtypeStruct((m, n_out), BF16)] + job_shape,
        scratch_shapes=[pltpu.VMEM((tm, d), BF16)],
        compiler_params=_cparams("parallel", "arbitrary"),
        name="norm_proj",
    )(x, mod, norm_w, w, *[job.src for job in cast_jobs])


def _conv3(u, cw, cb, first, last):
    t = u.shape[0]
    left = jnp.where(first, 0.0, pltpu.roll(u, 1, axis=0))
    right = jnp.where(last, 0.0, pltpu.roll(u, t - 1, axis=0))
    return left * cw[0:1, :] + u * cw[1:2, :] + right * cw[2:3, :] + cb


def _ffn_up_kernel(x_ref, mod_ref, nw_ref, wa_ref, wv_ref, cwa_ref, cwv_ref, cba_ref, cbv_ref,
                   o_ref, h_scr, *, seq_len):
    @pl.when(pl.program_id(1) == 0)
    def _():
        _norm_mod(x_ref, mod_ref, nw_ref, h_scr, 3)

    h = h_scr[...]
    tm = h.shape[0]
    pos = lax.broadcasted_iota(jnp.int32, (tm, 1), 0) % seq_len
    first = pos == 0
    last = pos == seq_len - 1
    a = _conv3(_dot(h, wa_ref[...]), cwa_ref[...], cba_ref[...], first, last)
    v = _conv3(_dot(h, wv_ref[...]), cwv_ref[...], cbv_ref[...], first, last)
    o_ref[...] = (a * _sigmoid(a) * v).astype(o_ref.dtype)


def _ffn_up(x, mod, mod_row0, rows_per_cond, norm_w, layer, w_up, w_idx, conv_w, conv_b,
            seq_len):
    m, d = x.shape
    d_ff = w_up.shape[-1] // 2
    tm, tn = ROW_TILE, COL_TILE
    nj = d_ff // tn
    cond_of = lambda i: mod_row0 + (i * tm) // rows_per_cond
    conv_b3 = conv_b.reshape(conv_b.shape[0], 1, 2 * d_ff)
    return pl.pallas_call(
        functools.partial(_ffn_up_kernel, seq_len=seq_len),
        grid=(m // tm, nj),
        in_specs=[
            pl.BlockSpec((tm, d), lambda i, j: (i, 0)),
            pl.BlockSpec((None, 6, d), lambda i, j: (cond_of(i), 0, 0)),
            pl.BlockSpec((None, 1, d), lambda i, j: (layer, 0, 0)),
            pl.BlockSpec((None, d, tn), lambda i, j: (w_idx, 0, j)),
            pl.BlockSpec((None, d, tn), lambda i, j: (w_idx, 0, nj + j)),
            pl.BlockSpec((None, 3, tn), lambda i, j: (layer, 0, j)),
            pl.BlockSpec((None, 3, tn), lambda i, j: (layer, 0, nj + j)),
            pl.BlockSpec((None, 1, tn), lambda i, j: (layer, 0, j)),
            pl.BlockSpec((None, 1, tn), lambda i, j: (layer, 0, nj + j)),
        ],
        out_specs=pl.BlockSpec((tm, tn), lambda i, j: (i, j)),
        out_shape=jax.ShapeDtypeStruct((m, d_ff), BF16),
        scratch_shapes=[pltpu.VMEM((tm, d), BF16)],
        compiler_params=_cparams("parallel", "arbitrary"),
        name="ffn_up",
    )(x, mod, norm_w, w_up, w_up, conv_w, conv_w, conv_b3, conv_b3)


def _proj_res_kernel(*refs, n_in, row, n_jobs):
    a_refs = refs[:n_in]
    w_refs = refs[n_in:2 * n_in]
    y_ref, mod_ref = refs[2 * n_in:2 * n_in + 2]
    job_src = refs[2 * n_in + 2:2 * n_in + 2 + n_jobs]
    o_ref = refs[2 * n_in + 2 + n_jobs]
    job_dst = refs[2 * n_in + 3 + n_jobs:]
    _run_cast_jobs(job_src, job_dst)
    acc = _dot(a_refs[0][...], w_refs[0][...])
    for a_ref, w_ref in zip(a_refs[1:], w_refs[1:]):
        acc += _dot(a_ref[...], w_ref[...])
    o_ref[...] = y_ref[...] + mod_ref[row:row + 1, :] * acc


def _proj_res(acts, w, w_idx, y, mod, mod_row0, rows_per_cond, *, row, cast_jobs=()):
    m, d = y.shape
    tm = ROW_TILE
    n_in = len(acts)
    k_blk = acts[0].shape[1]
    assert all(a.shape[1] == k_blk for a in acts)

    def vmem_bytes(tn):
        return 2 * 2 * n_in * k_blk * (tm + tn) + 2 * 2 * 4 * tm * tn

    tn = next(t for t in (2 * COL_TILE, COL_TILE) if t == COL_TILE
              or (d % t == 0 and not cast_jobs and vmem_bytes(t) <= 2 * VMEM_LIMIT_BYTES // 3))
    cond_of = lambda i: mod_row0 + (i * tm) // rows_per_cond
    in_specs = [pl.BlockSpec((tm, a.shape[1]), lambda i, j: (i, 0)) for a in acts]
    for k in range(n_in):
        in_specs.append(pl.BlockSpec((None, k_blk, tn), lambda i, j, k=k: (w_idx, k, j)))
    in_specs += [
        pl.BlockSpec((tm, tn), lambda i, j: (i, j)),
        pl.BlockSpec((None, 6, tn), lambda i, j: (cond_of(i), 0, j)),
    ]
    grid = (m // tm, d // tn)
    job_in, job_out, job_shape = _cast_job_specs(cast_jobs, grid)
    return pl.pallas_call(
        functools.partial(_proj_res_kernel, n_in=n_in, row=row, n_jobs=len(cast_jobs)),
        grid=grid,
        in_specs=in_specs + job_in,
        out_specs=[pl.BlockSpec((tm, tn), lambda i, j: (i, j))] + job_out,
        out_shape=[jax.ShapeDtypeStruct((m, d), F32)] + job_shape,
        compiler_params=_cparams("parallel", "arbitrary"),
        name="proj_res",
    )(*acts, *([w] * n_in), y, mod, *[job.src for job in cast_jobs])


def _rope_half_roll(x, cos2, sin2):
    return x * cos2 + pltpu.roll(x, x.shape[1] // 2, axis=1) * sin2


def _attn_kernel(*refs, n, n_ctx, rope, emit_kv, tq, n_jobs):
    q_ref, k_ref, v_ref, qw_ref, kw_ref = refs[:5]
    pos = 5
    if rope:
        cos_ref, sin_ref, ck_ref, cv_ref = refs[pos:pos + 4]
        pos += 4
    job_src = refs[pos:pos + n_jobs]
    pos += n_jobs
    o_ref = refs[pos]
    pos += 1
    if emit_kv:
        nk_ref, nv_ref = refs[pos:pos + 2]
        pos += 2
    job_dst = refs[pos:pos + n_jobs]
    pos += n_jobs
    kall, vall = refs[pos:pos + 2]
    _run_cast_jobs(job_src, job_dst)

    kn = _rms(k_ref[...].astype(F32), kw_ref[...])
    vv = v_ref[...]
    if emit_kv:
        nk_ref[...] = kn
        nv_ref[...] = vv.astype(F32)
    if rope:
        kn = _rope_half_roll(kn, cos_ref[...], sin_ref[...])
        kall[0:n_ctx, :] = ck_ref[...].astype(BF16)
        vall[0:n_ctx, :] = cv_ref[...].astype(BF16)
    kall[n_ctx:n_ctx + n, :] = kn.astype(BF16)
    vall[n_ctx:n_ctx + n, :] = vv.astype(BF16)

    scale = A_HEAD_DIM ** -0.5

    def chunk(c, carry):
        r0 = pl.multiple_of(c * tq, tq)
        rows = pl.ds(r0, tq)
        heads = [slice(g * A_HEAD_DIM, (g + 1) * A_HEAD_DIM) for g in range(A_GROUP)]
        qs = []
        for cols in heads:
            qh = _rms(q_ref[rows, cols].astype(F32), qw_ref[...])
            if rope:
                qh = _rope_half_roll(qh, cos_ref[rows, :], sin_ref[rows, :])
            qs.append((qh * scale).astype(BF16))
        ss = [_dot_nt(qh, kall[...]) for qh in qs]
        ps = [jnp.exp(s - jnp.max(s, axis=-1, keepdims=True)) for s in ss]
        ls = [jnp.sum(p, axis=-1, keepdims=True) for p in ps]
        os = [_dot(p.astype(BF16), vall[...]) / l for p, l in zip(ps, ls)]
        for cols, o in zip(heads, os):
            o_ref[rows, cols] = o.astype(o_ref.dtype)
        return carry

    lax.fori_loop(0, n // tq, chunk, 0, unroll=True)


def _attention(proj, n_seq, n, q_norm_w, k_norm_w, e, rope_tabs, cache_k, cache_v, emit_kv,
               cast_jobs=()):
    rope = rope_tabs is not None
    n_ctx = cache_k.shape[2] if rope else 0
    hd = A_HEAD_DIM
    qcols = A_GROUP * hd
    in_specs = [
        pl.BlockSpec((n, qcols), lambda b, kv: (b, kv)),
        pl.BlockSpec((n, hd), lambda b, kv: (b, A_Q // hd + kv)),
        pl.BlockSpec((n, hd), lambda b, kv: (b, (A_Q + A_KV) // hd + kv)),
        pl.BlockSpec((None, 1, hd), lambda b, kv: (e, 0, 0)),
        pl.BlockSpec((None, 1, hd), lambda b, kv: (e, 0, 0)),
    ]
    args = [proj, proj, proj, q_norm_w, k_norm_w]
    if rope:
        cos2, sin2 = rope_tabs
        in_specs += [
            pl.BlockSpec((n, hd), lambda b, kv: (0, 0)),
            pl.BlockSpec((n, hd), lambda b, kv: (0, 0)),
            pl.BlockSpec((None, None, n_ctx, hd), lambda b, kv: (b, e, 0, kv)),
            pl.BlockSpec((None, None, n_ctx, hd), lambda b, kv: (b, e, 0, kv)),
        ]
        ck = cache_k.reshape(cache_k.shape[0], cache_k.shape[1], n_ctx, A_KV)
        cv = cache_v.reshape(cache_v.shape[0], cache_v.shape[1], n_ctx, A_KV)
        args += [cos2, sin2, ck, cv]
    out_specs = [pl.BlockSpec((n, qcols), lambda b, kv: (b, kv))]
    out_shape = [jax.ShapeDtypeStruct((n_seq * n, A_Q), BF16)]
    if emit_kv:
        out_specs += [pl.BlockSpec((n, hd), lambda b, kv: (b, kv))] * 2
        out_shape += [jax.ShapeDtypeStruct((n_seq * n, A_KV), F32)] * 2
    grid = (n_seq, A_KV_HEADS)
    job_in, job_out, job_shape = _cast_job_specs(cast_jobs, grid)
    in_specs += job_in
    args += [job.src for job in cast_jobs]
    out_specs += job_out
    out_shape += job_shape
    return pl.pallas_call(
        functools.partial(_attn_kernel, n=n, n_ctx=n_ctx, rope=rope, emit_kv=emit_kv,
                          tq=min(n, ATTN_Q_ROWS), n_jobs=len(cast_jobs)),
        grid=grid,
        in_specs=in_specs,
        out_specs=out_specs,
        out_shape=out_shape,
        scratch_shapes=[pltpu.VMEM((n_ctx + n, hd), BF16), pltpu.VMEM((n_ctx + n, hd), BF16)],
        compiler_params=_cparams("parallel", "arbitrary"),
        name="attention",
    )(*args)


def _gla_levels(c, rev):
    row = lax.broadcasted_iota(jnp.int32, (c, c), 0)
    col = lax.broadcasted_iota(jnp.int32, (c, c), 1)
    shift = HGRN_BASE.bit_length() - 1
    x = (row >> shift) ^ (col >> shift)
    lvl = jnp.zeros((c, c), jnp.int32)
    for l in range(1, (c // HGRN_BASE).bit_length()):
        lvl = jnp.where(x >= (1 << (l - 1)), l, lvl)
    causal = (col >= row) if rev else (col <= row)
    return jnp.where(causal, lvl, -1)


def _gla_blocks(chains):
    c, dk = chains[0]["q"].shape
    each = lambda fn: [fn(ch) for ch in chains]

    def split3(ch):
        hi = ch["lf"].astype(BF16)
        r1 = ch["lf"] - hi.astype(F32)
        mid = r1.astype(BF16)
        ch["parts"] = (hi, mid, (r1 - mid.astype(F32)).astype(BF16))

    def cumulate(ch):
        hi, mid, lo = ch["parts"]
        ch["b"] = _dot(ch["tri"], hi) + _dot(ch["tri"], mid) + _dot(ch["tri"], lo)
        ch["tot"] = ch["b"][0:1, :] if ch["rev"] else ch["b"][c - 1:c, :]

    def inter(ch):
        ch["o"] = _dot_nt((ch["q"] * jnp.exp(ch["b"])).astype(BF16), ch["st"].astype(BF16))
        ch["khat"] = (ch["k"] * jnp.exp(ch["tot"] - ch["b"])).astype(BF16)
        ch["vb"] = ch["v"].astype(BF16)

    def level0(ch):
        b3 = ch["b"].reshape(c // HGRN_BASE, HGRN_BASE, dk)
        mid_row = HGRN_BASE // 2 if ch["rev"] else HGRN_BASE // 2 - 1
        a = jnp.clip(b3 - b3[:, mid_row:mid_row + 1, :], -EXP_CLAMP, EXP_CLAMP).reshape(c, dk)
        p = _dot_nt((ch["q"] * jnp.exp(a)).astype(BF16), (ch["k"] * jnp.exp(-a)).astype(BF16))
        ch["scores"] = jnp.where(ch["lvl"] == 0, p, 0.0)

    def upper_level(ch, h, level):
        b3 = ch["b"].reshape(c // (2 * h), 2 * h, dk)
        ref_row = h if ch["rev"] else h - 1
        e = jnp.exp(-jnp.abs(b3 - b3[:, ref_row:ref_row + 1, :])).reshape(c, dk)
        p = _dot_nt((ch["q"] * e).astype(BF16), (ch["k"] * e).astype(BF16))
        ch["scores"] = jnp.where(ch["lvl"] == level, p, ch["scores"])

    def combine(ch):
        o = ch["o"] + _dot(ch["scores"].astype(BF16), ch["vb"])
        st_new = ch["st"] * jnp.exp(ch["tot"]) + _dot_tn(ch["vb"], ch["khat"])
        return o, st_new

    each(split3)
    each(cumulate)
    each(inter)
    each(level0)
    h, level = HGRN_BASE, 1
    while h < c:
        each(functools.partial(upper_level, h=h, level=level))
        h, level = 2 * h, level + 1
    return each(combine)


def _hgrn_kernel(*refs, n, has_state, emit_state, n_jobs):
    q_ref, i_ref, zf_ref, zb_ref, g_ref, lb_ref, nw_ref = refs[:7]
    pos = 7
    if has_state:
        s0f_ref, s0b_ref = refs[pos:pos + 2]
        pos += 2
    job_src = refs[pos:pos + n_jobs]
    pos += n_jobs
    o_ref = refs[pos]
    pos += 1
    if emit_state:
        sf_ref, sb_ref = refs[pos:pos + 2]
        pos += 2
    job_dst = refs[pos:pos + n_jobs]
    pos += n_jobs
    of_scr, ob_scr, st_scr = refs[pos:pos + 3]
    _run_cast_jobs(job_src, job_dst)

    c = MIX_BLOCK
    nb = n // c
    qscale = B_DK ** -0.5
    lvl_f = _gla_levels(c, False)
    lvl_b = _gla_levels(c, True)
    tri_f = jnp.where(lvl_f >= 0, 1.0, 0.0).astype(BF16)
    tri_b = jnp.where(lvl_b >= 0, 1.0, 0.0).astype(BF16)
    heads = [slice(hh * B_DK, (hh + 1) * B_DK) for hh in range(HGRN_HEADS_PER_STEP)]

    def chain(z_ref, rows, hh, rev):
        cols = heads[hh]
        lb = lb_ref[hh]
        f = lb + (1.0 - lb) * _sigmoid(z_ref[rows, cols].astype(F32))
        return dict(q=q_ref[rows, cols].astype(F32) * qscale, k=1.0 - f,
                    v=i_ref[rows, cols].astype(F32), lf=jnp.log(f),
                    st=st_scr[2 * hh + int(rev)], rev=rev,
                    lvl=lvl_b if rev else lvl_f, tri=tri_b if rev else tri_f)

    def step(j, carry):
        rows_f = pl.ds(pl.multiple_of(j * c, c), c)
        rows_b = pl.ds(pl.multiple_of((nb - 1 - j) * c, c), c)
        chains = []
        for hh in range(len(heads)):
            chains += [chain(zf_ref, rows_f, hh, False), chain(zb_ref, rows_b, hh, True)]
        outs = _gla_blocks(chains)
        for hh, cols in enumerate(heads):
            (o_f, st_f), (o_b, st_b) = outs[2 * hh], outs[2 * hh + 1]
            of_scr[rows_f, cols] = o_f
            ob_scr[rows_b, cols] = o_b
            st_scr[2 * hh] = st_f
            st_scr[2 * hh + 1] = st_b
        return carry

    def finish(j, carry):
        rows = pl.ds(pl.multiple_of(j * c, c), c)
        for cols in heads:
            y = _rms(of_scr[rows, cols] + ob_scr[rows, cols], nw_ref[...])
            g = g_ref[rows, cols].astype(F32)
            o_ref[rows, cols] = (y * (g * _sigmoid(g))).astype(o_ref.dtype)
        return carry

    for hh in range(len(heads)):
        if has_state:
            st_scr[2 * hh] = s0f_ref[hh].T
            st_scr[2 * hh + 1] = s0b_ref[hh].T
        else:
            st_scr[2 * hh] = jnp.zeros((B_DV, B_DK), F32)
            st_scr[2 * hh + 1] = jnp.zeros((B_DV, B_DK), F32)
    lax.fori_loop(0, nb, step, 0, unroll=2)
    lax.fori_loop(0, nb, finish, 0, unroll=2)
    if emit_state:
        for hh in range(len(heads)):
            sf_ref[hh] = st_scr[2 * hh].T
            sb_ref[hh] = st_scr[2 * hh + 1].T


def _hgrn(proj, n_seq, n, lb, o_norm_w, e, state_f, state_b, emit_state, cast_jobs=()):
    has_state = state_f is not None
    hp = HGRN_HEADS_PER_STEP
    d = B_DK * hp
    base = (A_Q + 2 * A_KV) // d
    col = lambda k: (lambda b, h: (b, base + k * (B_HEADS // hp) + h))
    in_specs = [pl.BlockSpec((n, d), col(k)) for k in range(5)]
    in_specs += [
        pl.BlockSpec((hp, 1, B_DK), lambda b, h: (h, 0, 0)),
        pl.BlockSpec((None, 1, B_DV), lambda b, h: (e, 0, 0)),
    ]
    args = [proj] * 5 + [lb, o_norm_w]
    if has_state:
        st_spec = pl.BlockSpec((None, None, hp, B_DK, B_DV), lambda b, h: (b, e, h, 0, 0))
        in_specs += [st_spec, st_spec]
        args += [state_f, state_b]
    out_specs = [pl.BlockSpec((n, d), lambda b, h: (b, h))]
    out_shape = [jax.ShapeDtypeStruct((n_seq * n, B_V), BF16)]
    if emit_state:
        so = pl.BlockSpec((None, hp, B_DK, B_DV), lambda b, h: (b, h, 0, 0))
        out_specs += [so, so]
        out_shape += [jax.ShapeDtypeStruct((n_seq, B_HEADS, B_DK, B_DV), F32)] * 2
    grid = (n_seq, B_HEADS // hp)
    job_in, job_out, job_shape = _cast_job_specs(cast_jobs, grid)
    in_specs += job_in
    args += [job.src for job in cast_jobs]
    out_specs += job_out
    out_shape += job_shape
    return pl.pallas_call(
        functools.partial(_hgrn_kernel, n=n, has_state=has_state, emit_state=emit_state,
                          n_jobs=len(cast_jobs)),
        grid=grid,
        in_specs=in_specs,
        out_specs=out_specs,
        out_shape=out_shape,
        scratch_shapes=[pltpu.VMEM((n, d), F32), pltpu.VMEM((n, d), F32),
                        pltpu.VMEM((2 * hp, B_DV, B_DK), F32)],
        compiler_params=_cparams("parallel", "arbitrary"),
        name="hgrn2",
    )(*args)


def _rope_split(x, cos, sin):
    half = x.shape[1] // 2
    x1, x2 = x[:, :half], x[:, half:]
    return jnp.concatenate([x1 * cos - x2 * sin, x1 * sin + x2 * cos], axis=1)


def _ret_kernel(*refs, n, n_seqs, rope, has_state, emit_state, n_jobs):
    q_ref, k_ref, v_ref, g_ref, lgf_ref, lgb_ref, nw_ref = refs[:7]
    pos = 7
    if rope:
        cos_ref, sin_ref = refs[pos:pos + 2]
        pos += 2
    if has_state:
        s0f_ref, s0b_ref = refs[pos:pos + 2]
        pos += 2
    job_src = refs[pos:pos + n_jobs]
    pos += n_jobs
    o_ref = refs[pos]
    pos += 1
    if emit_state:
        sf_ref, sb_ref = refs[pos:pos + 2]
        pos += 2
    job_dst = refs[pos:pos + n_jobs]
    pos += n_jobs
    _run_cast_jobs(job_src, job_dst)
    qs_scr, ks_scr, of_scr, ob_scr, stf_scr, stb_scr, dm_scr, qd_scr, kd_scr = refs[pos:pos + 9]

    c = min(RET_BLOCK, n)
    nb = n // c
    seqs = range(n_seqs)
    skip_inter = (not has_state) and nb == 1
    kscale = C_DK ** -0.5
    rowi = lax.broadcasted_iota(jnp.int32, (c, c), 0)
    coli = lax.broadcasted_iota(jnp.int32, (c, c), 1)
    rowq = lax.broadcasted_iota(jnp.int32, (c, C_DK), 0).astype(F32)

    def prep(j, carry):
        rows = pl.ds(pl.multiple_of(j * MIX_BLOCK, MIX_BLOCK), MIX_BLOCK)
        q = q_ref[rows, :].astype(F32)
        k = k_ref[rows, :].astype(F32)
        if rope:
            pos = pl.ds(pl.multiple_of((j % (n // MIX_BLOCK)) * MIX_BLOCK, MIX_BLOCK), MIX_BLOCK)
            q = _rope_split(q, cos_ref[pos, :], sin_ref[pos, :])
            k = _rope_split(k, cos_ref[pos, :], sin_ref[pos, :])
        qs_scr[rows, :] = q
        ks_scr[rows, :] = k * kscale
        return carry

    @pl.when(pl.program_id(1) == 0)
    def _():
        lgs = (lgf_ref[...], lgb_ref[...])
        dist = (rowi - coli).astype(F32)
        dm_scr[...] = (
            jnp.where(dist >= 0.0, jnp.exp(lgs[0][:, :c] * jnp.maximum(dist, 0.0)), 0.0)
            + jnp.where(dist <= 0.0, jnp.exp(lgs[1][:, :c] * jnp.maximum(-dist, 0.0)), 0.0))
        for d, rev in enumerate((False, True)):
            lgq = lgs[d][:, :C_DK]
            qd_scr[d] = jnp.exp(lgq * ((c - rowq) if rev else (rowq + 1.0)))
            kd_scr[d] = jnp.exp(lgq * (rowq if rev else (c - 1.0 - rowq)))

    def block_rows(s, j):
        return pl.ds(pl.multiple_of(s * n + j * c, c), c)

    def sweep_block(s, rows, d, lg_ref, st_scr, o_scr):
        vb = v_ref[rows, :]
        u = _dot_tn((ks_scr[rows, :] * kd_scr[d]).astype(BF16), vb)
        if skip_inter:
            st_scr[s] = u
        else:
            st = st_scr[s]
            o_scr[rows, :] = _dot((qs_scr[rows, :] * qd_scr[d]).astype(BF16), st.astype(BF16))
            st_scr[s] = st * jnp.exp(lg_ref[...] * float(c)) + u

    def sweep(j, carry):
        for s in seqs:
            sweep_block(s, block_rows(s, j), 0, lgf_ref, stf_scr, of_scr)
            sweep_block(s, block_rows(s, nb - 1 - j), 1, lgb_ref, stb_scr, ob_scr)
        return carry

    def finish(j, carry):
        pieces = []
        for s in seqs:
            keys = block_rows(s, j)
            kb, vb = ks_scr[keys, :].astype(BF16), v_ref[keys, :]
            for p in range(c // MIX_BLOCK):
                rows = pl.ds(pl.multiple_of(s * n + j * c + p * MIX_BLOCK, MIX_BLOCK), MIX_BLOCK)
                pieces.append((rows, slice(p * MIX_BLOCK, (p + 1) * MIX_BLOCK), kb, vb))
        ss = [_dot_nt(qs_scr[rows, :].astype(BF16), kb) * dm_scr[within, :]
              for rows, within, kb, _ in pieces]
        os = [_dot(sc.astype(BF16), vb) for sc, (_, _, _, vb) in zip(ss, pieces)]
        if not skip_inter:
            os = [o + of_scr[rows, :] + ob_scr[rows, :] for o, (rows, *_) in zip(os, pieces)]
        ys = [_rms(o, nw_ref[...]) for o in os]
        for y, (rows, *_) in zip(ys, pieces):
            g = g_ref[rows, :].astype(F32)
            o_ref[rows, :] = (y * (g * _sigmoid(g))).astype(o_ref.dtype)
        return carry

    lax.fori_loop(0, n_seqs * n // MIX_BLOCK, prep, 0)
    if has_state:
        stf_scr[...] = s0f_ref[...]
        stb_scr[...] = s0b_ref[...]
    elif not skip_inter:
        stf_scr[...] = jnp.zeros(stf_scr.shape, F32)
        stb_scr[...] = jnp.zeros(stb_scr.shape, F32)
    if emit_state or not skip_inter:
        lax.fori_loop(0, nb, sweep, 0, unroll=True)
    lax.fori_loop(0, nb, finish, 0, unroll=True)
    if emit_state:
        sf_ref[...] = stf_scr[...]
        sb_ref[...] = stb_scr[...]


def _retention(proj, n_seq, n, lg_f, lg_b, o_norm_w, o_idx, rope_tabs, state_f, state_b,
               emit_state, cast_jobs=()):
    rope = rope_tabs is not None
    has_state = state_f is not None
    nq = C_QK // C_DK
    c = min(RET_BLOCK, n)
    per_step = max(RET_MIN_SEQS, ROW_TILE // n)
    assert n_seq % per_step == 0
    rows = per_step * n
    in_specs = [
        pl.BlockSpec((rows, C_DK), lambda h, b: (b, h)),
        pl.BlockSpec((rows, C_DK), lambda h, b: (b, nq + h)),
        pl.BlockSpec((rows, C_DV), lambda h, b: (b, 2 * C_QK // C_DV + h)),
        pl.BlockSpec((rows, C_DV), lambda h, b: (b, (2 * C_QK + C_V) // C_DV + h)),
        pl.BlockSpec((None, 1, C_DV), lambda h, b: (h, 0, 0)),
        pl.BlockSpec((None, 1, C_DV), lambda h, b: (h, 0, 0)),
        pl.BlockSpec((None, 1, C_DV), lambda h, b: (o_idx, 0, 0)),
    ]
    args = [proj] * 4 + [lg_f, lg_b, o_norm_w]
    if rope:
        in_specs += [pl.BlockSpec((n, C_DK // 2), lambda h, b: (0, 0))] * 2
        args += list(rope_tabs)
    if has_state:
        st_spec = pl.BlockSpec((per_step, None, None, C_DK, C_DV),
                               lambda h, b: (b, o_idx, h, 0, 0))
        in_specs += [st_spec, st_spec]
        args += [state_f, state_b]
    out_specs = [pl.BlockSpec((rows, C_DV), lambda h, b: (b, h))]
    out_shape = [jax.ShapeDtypeStruct((n_seq * n, C_V), BF16)]
    if emit_state:
        so = pl.BlockSpec((per_step, None, C_DK, C_DV), lambda h, b: (b, h, 0, 0))
        out_specs += [so, so]
        out_shape += [jax.ShapeDtypeStruct((n_seq, C_HEADS, C_DK, C_DV), F32)] * 2
    grid = (C_HEADS, n_seq // per_step)
    job_in, job_out, job_shape = _cast_job_specs(cast_jobs, grid)
    in_specs += job_in
    args += [job.src for job in cast_jobs]
    out_specs += job_out
    out_shape += job_shape
    return pl.pallas_call(
        functools.partial(_ret_kernel, n=n, n_seqs=per_step, rope=rope, has_state=has_state,
                          emit_state=emit_state, n_jobs=len(cast_jobs)),
        grid=grid,
        in_specs=in_specs,
        out_specs=out_specs,
        out_shape=out_shape,
        scratch_shapes=[pltpu.VMEM((rows, C_DK), F32), pltpu.VMEM((rows, C_DK), F32),
                        pltpu.VMEM((rows, C_DV), F32), pltpu.VMEM((rows, C_DV), F32),
                        pltpu.VMEM((per_step, C_DK, C_DV), F32),
                        pltpu.VMEM((per_step, C_DK, C_DV), F32),
                        pltpu.VMEM((c, c), F32), pltpu.VMEM((2, c, C_DK), F32),
                        pltpu.VMEM((2, c, C_DK), F32)],
        compiler_params=_cparams("arbitrary", "arbitrary"),
        name="retention",
    )(*args)


def _rope_tables(n_tokens, head_dim):
    rows = n_tokens // GRID_W
    row = jnp.repeat(jnp.arange(rows, dtype=F32), GRID_W)
    col = jnp.tile(jnp.arange(GRID_W, dtype=F32), rows)
    quarter = head_dim // 4
    inv_freq = jnp.power(ROPE_BASE, -jnp.arange(quarter, dtype=F32) / quarter)
    ang = jnp.concatenate([row[:, None] * inv_freq, col[:, None] * inv_freq], axis=-1)
    return jnp.cos(ang), jnp.sin(ang)


def kernel(x_prompt, x_sample, cache_attn_k, cache_attn_v, state_hgrn_fwd, state_hgrn_bwd,
           state_ret_fwd, state_ret_bwd, c, c_ctx, w_mod, b_mod, norm_mix_w, norm_ffn_w,
           w_in_even, w_out_even, attn_q_norm_w, attn_k_norm_w, hgrn_lb, hgrn_o_norm_w,
           w_in_odd, w_out_odd, ret_decay_fwd, ret_decay_bwd, ret_o_norm_w,
           w_up, conv_w, conv_b, w_down):
    depth, d_model = norm_mix_w.shape
    bp, np_, _ = x_prompt.shape
    bs, ns, _ = x_sample.shape

    lb_all = jnp.cumsum(jax.nn.softmax(hgrn_lb.astype(F32), axis=0), axis=0)
    lg_f = jnp.broadcast_to(jax.nn.log_sigmoid(ret_decay_fwd.astype(F32))[:, :, None, None],
                            ret_decay_fwd.shape + (1, C_DV))
    lg_b = jnp.broadcast_to(jax.nn.log_sigmoid(ret_decay_bwd.astype(F32))[:, :, None, None],
                            ret_decay_bwd.shape + (1, C_DV))
    cos_a, sin_a = _rope_tables(ns, A_HEAD_DIM)
    rope_a = (jnp.concatenate([cos_a, cos_a], axis=1), jnp.concatenate([-sin_a, sin_a], axis=1))
    rope_c = _rope_tables(ns, C_DK)
    nmw = norm_mix_w.reshape(depth, 1, d_model)
    nfw = norm_ffn_w.reshape(depth, 1, d_model)
    qnw = attn_q_norm_w.reshape(-1, 1, A_HEAD_DIM)
    knw = attn_k_norm_w.reshape(-1, 1, A_HEAD_DIM)
    hnw = hgrn_o_norm_w.reshape(-1, 1, B_DV)
    rnw = ret_o_norm_w.reshape(-1, 1, C_DV)
    w_in = {0: w_in_even[0:1].astype(BF16)}
    w_out, w_gate_val, w_down_b = {}, {}, {}
    n_cond = 1 + bs
    pad = (-n_cond) % 8
    cond = jnp.concatenate([c_ctx[None, :], c, jnp.zeros((pad, d_model), F32)], axis=0)
    mod_all = _modulation(cond, w_mod, b_mod).reshape(depth, n_cond + pad, 6, d_model)

    groups = (
        dict(x=x_prompt.reshape(bp * np_, d_model), n_seq=bp, n=np_, row0=0,
             rows_per_cond=bp * np_, latent=False),
        dict(x=x_sample.reshape(bs * ns, d_model), n_seq=bs, n=ns, row0=1,
             rows_per_cond=ns, latent=True),
    )
    def w_in_f32(l):
        return (w_in_even, l // 2) if l % 2 == 0 else (w_in_odd, l // 2)

    results = []
    for grp in groups:
        y, n_seq, n = grp["x"], grp["n_seq"], grp["n"]
        row0, rpc, latent = grp["row0"], grp["rows_per_cond"], grp["latent"]
        new = dict(k=[], v=[], hf=[], hb=[], rf=[], rb=[])
        for l in range(depth):
            mod = mod_all[l]
            first = l not in w_out
            next_in = l + 1 < depth and l + 1 not in w_in
            jobs = [_CastJob(*w_in_f32(l + 1))] if next_in else []
            if l % 2 == 1 and first:
                jobs.append(_CastJob(w_up, l))
            proj, *copies = _norm_proj(y, mod, row0, rpc, nmw, l, w_in[l], 0, row=0, cast_jobs=jobs)
            if next_in:
                w_in[l + 1] = copies.pop(0)[None]
            if copies:
                w_gate_val[l] = copies.pop(0)[None]
            if l % 2 == 0:
                e = l // 2
                att = _attention(proj, n_seq, n, qnw, knw, e, rope_a if latent else None,
                                 cache_attn_k, cache_attn_v, emit_kv=not latent,
                                 cast_jobs=[_CastJob(w_out_even, e)] if first else ())
                lb = lb_all[e].reshape(B_HEADS, 1, B_DK)
                hg = _hgrn(proj, n_seq, n, lb, hnw, e,
                           state_hgrn_fwd if latent else None,
                           state_hgrn_bwd if latent else None, emit_state=not latent,
                           cast_jobs=[_CastJob(w_up, l), _CastJob(w_down, l)] if first else ())
                if first:
                    w_out[l], w_gate_val[l], w_down_b[l] = att[-1][None], hg[-2][None], hg[-1][None]
                if not latent:
                    new["k"].append(att[1].reshape(n_seq, n, A_KV_HEADS, A_HEAD_DIM))
                    new["v"].append(att[2].reshape(n_seq, n, A_KV_HEADS, A_HEAD_DIM))
                    new["hf"].append(hg[1])
                    new["hb"].append(hg[2])
                mixed, out_jobs = [att[0], hg[0]], []
            else:
                o = l // 2
                rt = _retention(proj, n_seq, n, lg_f[o], lg_b[o], rnw, o,
                                rope_c if latent else None,
                                state_ret_fwd if latent else None,
                                state_ret_bwd if latent else None, emit_state=not latent,
                                cast_jobs=[_CastJob(w_out_odd, o)] if first else ())
                if first:
                    w_out[l] = rt[-1][None]
                if not latent:
                    new["rf"].append(rt[1])
                    new["rb"].append(rt[2])
                mixed, out_jobs = [rt[0]], ([_CastJob(w_down, l)] if first else [])
            y, *copies = _proj_res(mixed, w_out[l], 0, y, mod, row0, rpc, row=2, cast_jobs=out_jobs)
            if copies:
                w_down_b[l] = copies[0][None]
            act = _ffn_up(y, mod, row0, rpc, nfw, l, w_gate_val[l], 0, conv_w, conv_b, n)
            y, = _proj_res([act], w_down_b[l], 0, y, mod, row0, rpc, row=5)
        results.append((y.reshape(n_seq, n, d_model), new))

    (y_p, new), (y_s, _) = results
    stack = lambda xs: jnp.stack(xs, axis=1)
    return (y_p, y_s, stack(new["k"]), stack(new["v"]), stack(new["hf"]), stack(new["hb"]),
            stack(new["rf"]), stack(new["rb"]))
```

```python
import functools
from typing import NamedTuple

import jax
import jax.numpy as jnp
from jax import lax
from jax.experimental import pallas as pl
from jax.experimental.pallas import tpu as pltpu

F32 = jnp.float32
BF16 = jnp.bfloat16

GRID_W = 64
A_HEADS = 8
A_KV_HEADS = 2
A_HEAD_DIM = 128
ROPE_BASE = 10000.0
B_HEADS = 8
B_DK = 128
B_DV = 128
C_HEADS = 8
C_DK = 256
C_DV = 512
EPS = 1e-6

A_Q = A_HEADS * A_HEAD_DIM
A_KV = A_KV_HEADS * A_HEAD_DIM
A_GROUP = A_HEADS // A_KV_HEADS
B_QK = B_HEADS * B_DK
B_V = B_HEADS * B_DV
C_QK = C_HEADS * C_DK
C_V = C_HEADS * C_DV

VMEM_LIMIT_BYTES = 56 * 1024 * 1024
ROW_TILE = 1024
COL_TILE = 512
ATTN_Q_ROWS = 256
MIX_BLOCK = 128
RET_BLOCK = 256
RET_MIN_SEQS = 2
NORM_ROWS = 16
HGRN_BASE = 16
HGRN_HEADS_PER_STEP = 4
EXP_CLAMP = 80.0


def _cparams(*sem):
    return pltpu.CompilerParams(dimension_semantics=sem, vmem_limit_bytes=VMEM_LIMIT_BYTES)


def _dot(a, b):
    return jnp.dot(a, b, preferred_element_type=F32)


def _dot_nt(a, b):
    return lax.dot_general(a, b, (((1,), (1,)), ((), ())), preferred_element_type=F32)


def _dot_tn(a, b):
    return lax.dot_general(a, b, (((0,), (0,)), ((), ())), preferred_element_type=F32)


def _sigmoid(x):
    return 1.0 / (1.0 + jnp.exp(-x))


def _rms(x, w):
    return x * lax.rsqrt(jnp.mean(x * x, axis=-1, keepdims=True) + EPS) * w


class _CastJob(NamedTuple):
    src: jax.Array
    lead: int


def _cast_job_specs(jobs, grid):
    n_steps = grid[0] * grid[1]
    in_specs, out_specs, out_shape = [], [], []
    for job in jobs:
        _, r, c = job.src.shape
        n_blocks = max(nb for nb in range(1, n_steps + 1)
                       if r % nb == 0 and (r // nb) % 16 == 0)
        rows, rep = r // n_blocks, n_steps // n_blocks
        blk = lambda i, j, rep=rep, last=n_blocks - 1: jnp.minimum((i * grid[1] + j) // rep, last)
        in_specs.append(pl.BlockSpec((None, rows, c),
                                     lambda i, j, job=job, blk=blk: (job.lead, blk(i, j), 0)))
        out_specs.append(pl.BlockSpec((rows, c), lambda i, j, blk=blk: (blk(i, j), 0)))
        out_shape.append(jax.ShapeDtypeStruct((r, c), BF16))
    return in_specs, out_specs, out_shape


def _run_cast_jobs(src_refs, dst_refs):
    for src_ref, dst_ref in zip(src_refs, dst_refs):
        dst_ref[...] = src_ref[...].astype(BF16)


def _mod_kernel(c_ref, w_ref, b_ref, o_ref):
    c = c_ref[...]
    s = (c * _sigmoid(c)).astype(BF16)
    o_ref[...] = _dot(s, w_ref[...].astype(BF16)) + b_ref[...]


def _modulation(cond, w_mod, b_mod):
    n_layers, d, n_out = w_mod.shape
    r = cond.shape[0]
    tn = 2 * COL_TILE
    return pl.pallas_call(
        _mod_kernel,
        grid=(n_layers, n_out // tn),
        in_specs=[
            pl.BlockSpec((r, d), lambda l, j: (0, 0)),
            pl.BlockSpec((None, d, tn), lambda l, j: (l, 0, j)),
            pl.BlockSpec((None, 1, tn), lambda l, j: (l, 0, j)),
        ],
        out_specs=pl.BlockSpec((None, r, tn), lambda l, j: (l, 0, j)),
        out_shape=jax.ShapeDtypeStruct((n_layers, r, n_out), F32),
        compiler_params=_cparams("arbitrary", "arbitrary"),
        name="modulation",
    )(cond, w_mod, b_mod.reshape(n_layers, 1, n_out))


def _norm_mod(x_ref, mod_ref, nw_ref, h_scr, row):
    gain = nw_ref[...] * (1.0 + mod_ref[row + 1:row + 2, :])
    shift = mod_ref[row:row + 1, :]

    def chunk(c, carry):
        rows = pl.ds(pl.multiple_of(c * NORM_ROWS, NORM_ROWS), NORM_ROWS)
        x = x_ref[rows, :]
        r = lax.rsqrt(jnp.mean(x * x, axis=-1, keepdims=True) + EPS)
        h_scr[rows, :] = (x * r * gain + shift).astype(BF16)
        return carry

    lax.fori_loop(0, x_ref.shape[0] // NORM_ROWS, chunk, 0, unroll=8)


def _norm_proj_kernel(*refs, row, n_jobs):
    x_ref, mod_ref, nw_ref, w_ref = refs[:4]
    job_src = refs[4:4 + n_jobs]
    o_ref = refs[4 + n_jobs]
    job_dst = refs[5 + n_jobs:5 + 2 * n_jobs]
    h_scr = refs[-1]
    _run_cast_jobs(job_src, job_dst)

    @pl.when(pl.program_id(1) == 0)
    def _():
        _norm_mod(x_ref, mod_ref, nw_ref, h_scr, row)

    o_ref[...] = _dot(h_scr[...], w_ref[...]).astype(o_ref.dtype)


def _norm_proj(x, mod, mod_row0, rows_per_cond, norm_w, layer, w, w_idx, *, row, cast_jobs=()):
    m, d = x.shape
    n_out = w.shape[-1]
    tm = ROW_TILE
    tn = 2 * COL_TILE if n_out % (2 * COL_TILE) == 0 else COL_TILE
    grid = (m // tm, n_out // tn)
    cond_of = lambda i: mod_row0 + (i * tm) // rows_per_cond
    job_in, job_out, job_shape = _cast_job_specs(cast_jobs, grid)
    return pl.pallas_call(
        functools.partial(_norm_proj_kernel, row=row, n_jobs=len(cast_jobs)),
        grid=grid,
        in_specs=[
            pl.BlockSpec((tm, d), lambda i, j: (i, 0)),
            pl.BlockSpec((None, 6, d), lambda i, j: (cond_of(i), 0, 0)),
            pl.BlockSpec((None, 1, d), lambda i, j: (layer, 0, 0)),
            pl.BlockSpec((None, d, tn), lambda i, j: (w_idx, 0, j)),
        ] + job_in,
        out_specs=[pl.BlockSpec((tm, tn), lambda i, j: (i, j))] + job_out,
        out_shape=[jax.ShapeDtypeStruct((m, n_out), BF16)] + job_shape,
        scratch_shapes=[pltpu.VMEM((tm, d), BF16)],
        compiler_params=_cparams("parallel", "arbitrary"),
        name="norm_proj",
    )(x, mod, norm_w, w, *[job.src for job in cast_jobs])


def _conv3(u, cw, cb, first, last):
    t = u.shape[0]
    left = jnp.where(first, 0.0, pltpu.roll(u, 1, axis=0))
    right = jnp.where(last, 0.0, pltpu.roll(u, t - 1, axis=0))
    return left * cw[0:1, :] + u * cw[1:2, :] + right * cw[2:3, :] + cb


def _ffn_up_kernel(x_ref, mod_ref, nw_ref, wa_ref, wv_ref, cwa_ref, cwv_ref, cba_ref, cbv_ref,
                   o_ref, h_scr, *, seq_len):
    @pl.when(pl.program_id(1) == 0)
    def _():
        _norm_mod(x_ref, mod_ref, nw_ref, h_scr, 3)

    h = h_scr[...]
    tm = h.shape[0]
    pos = lax.broadcasted_iota(jnp.int32, (tm, 1), 0) % seq_len
    first = pos == 0
    last = pos == seq_len - 1
    a = _conv3(_dot(h, wa_ref[...]), cwa_ref[...], cba_ref[...], first, last)
    v = _conv3(_dot(h, wv_ref[...]), cwv_ref[...], cbv_ref[...], first, last)
    o_ref[...] = (a * _sigmoid(a) * v).astype(o_ref.dtype)


def _ffn_up(x, mod, mod_row0, rows_per_cond, norm_w, layer, w_up, w_idx, conv_w, conv_b,
            seq_len):
    m, d = x.shape
    d_ff = w_up.shape[-1] // 2
    tm, tn = ROW_TILE, COL_TILE
    nj = d_ff // tn
    cond_of = lambda i: mod_row0 + (i * tm) // rows_per_cond
    conv_b3 = conv_b.reshape(conv_b.shape[0], 1, 2 * d_ff)
    return pl.pallas_call(
        functools.partial(_ffn_up_kernel, seq_len=seq_len),
        grid=(m // tm, nj),
        in_specs=[
            pl.BlockSpec((tm, d), lambda i, j: (i, 0)),
            pl.BlockSpec((None, 6, d), lambda i, j: (cond_of(i), 0, 0)),
            pl.BlockSpec((None, 1, d), lambda i, j: (layer, 0, 0)),
            pl.BlockSpec((None, d, tn), lambda i, j: (w_idx, 0, j)),
            pl.BlockSpec((None, d, tn), lambda i, j: (w_idx, 0, nj + j)),
            pl.BlockSpec((None, 3, tn), lambda i, j: (layer, 0, j)),
            pl.BlockSpec((None, 3, tn), lambda i, j: (layer, 0, nj + j)),
            pl.BlockSpec((None, 1, tn), lambda i, j: (layer, 0, j)),
            pl.BlockSpec((None, 1, tn), lambda i, j: (layer, 0, nj + j)),
        ],
        out_specs=pl.BlockSpec((tm, tn), lambda i, j: (i, j)),
        out_shape=jax.ShapeDtypeStruct((m, d_ff), BF16),
        scratch_shapes=[pltpu.VMEM((tm, d), BF16)],
        compiler_params=_cparams("parallel", "arbitrary"),
        name="ffn_up",
    )(x, mod, norm_w, w_up, w_up, conv_w, conv_w, conv_b3, conv_b3)


def _proj_res_kernel(*refs, n_in, row, n_jobs):
    a_refs = refs[:n_in]
    w_refs = refs[n_in:2 * n_in]
    y_ref, mod_ref = refs[2 * n_in:2 * n_in + 2]
    job_src = refs[2 * n_in + 2:2 * n_in + 2 + n_jobs]
    o_ref = refs[2 * n_in + 2 + n_jobs]
    job_dst = refs[2 * n_in + 3 + n_jobs:]
    _run_cast_jobs(job_src, job_dst)
    acc = _dot(a_refs[0][...], w_refs[0][...])
    for a_ref, w_ref in zip(a_refs[1:], w_refs[1:]):
        acc += _dot(a_ref[...], w_ref[...])
    o_ref[...] = y_ref[...] + mod_ref[row:row + 1, :] * acc


def _proj_res(acts, w, w_idx, y, mod, mod_row0, rows_per_cond, *, row, cast_jobs=()):
    m, d = y.shape
    tm = ROW_TILE
    n_in = len(acts)
    k_blk = acts[0].shape[1]
    assert all(a.shape[1] == k_blk for a in acts)

    def vmem_bytes(tn):
        return 2 * 2 * n_in * k_blk * (tm + tn) + 2 * 2 * 4 * tm * tn

    tn = next(t for t in (2 * COL_TILE, COL_TILE) if t == COL_TILE
              or (d % t == 0 and not cast_jobs and vmem_bytes(t) <= 2 * VMEM_LIMIT_BYTES // 3))
    cond_of = lambda i: mod_row0 + (i * tm) // rows_per_cond
    in_specs = [pl.BlockSpec((tm, a.shape[1]), lambda i, j: (i, 0)) for a in acts]
    for k in range(n_in):
        in_specs.append(pl.BlockSpec((None, k_blk, tn), lambda i, j, k=k: (w_idx, k, j)))
    in_specs += [
        pl.BlockSpec((tm, tn), lambda i, j: (i, j)),
        pl.BlockSpec((None, 6, tn), lambda i, j: (cond_of(i), 0, j)),
    ]
    grid = (m // tm, d // tn)
    job_in, job_out, job_shape = _cast_job_specs(cast_jobs, grid)
    return pl.pallas_call(
        functools.partial(_proj_res_kernel, n_in=n_in, row=row, n_jobs=len(cast_jobs)),
        grid=grid,
        in_specs=in_specs + job_in,
        out_specs=[pl.BlockSpec((tm, tn), lambda i, j: (i, j))] + job_out,
        out_shape=[jax.ShapeDtypeStruct((m, d), F32)] + job_shape,
        compiler_params=_cparams("parallel", "arbitrary"),
        name="proj_res",
    )(*acts, *([w] * n_in), y, mod, *[job.src for job in cast_jobs])


def _rope_half_roll(x, cos2, sin2):
    return x * cos2 + pltpu.roll(x, x.shape[1] // 2, axis=1) * sin2


def _attn_kernel(*refs, n, n_ctx, rope, emit_kv, tq, n_jobs):
    q_ref, k_ref, v_ref, qw_ref, kw_ref = refs[:5]
    pos = 5
    if rope:
        cos_ref, sin_ref, ck_ref, cv_ref = refs[pos:pos + 4]
        pos += 4
    job_src = refs[pos:pos + n_jobs]
    pos += n_jobs
    o_ref = refs[pos]
    pos += 1
    if emit_kv:
        nk_ref, nv_ref = refs[pos:pos + 2]
        pos += 2
    job_dst = refs[pos:pos + n_jobs]
    pos += n_jobs
    kall, vall = refs[pos:pos + 2]
    _run_cast_jobs(job_src, job_dst)

    kn = _rms(k_ref[...].astype(F32), kw_ref[...])
    vv = v_ref[...]
    if emit_kv:
        nk_ref[...] = kn
        nv_ref[...] = vv.astype(F32)
    if rope:
        kn = _rope_half_roll(kn, cos_ref[...], sin_ref[...])
        kall[0:n_ctx, :] = ck_ref[...].astype(BF16)
        vall[0:n_ctx, :] = cv_ref[...].astype(BF16)
    kall[n_ctx:n_ctx + n, :] = kn.astype(BF16)
    vall[n_ctx:n_ctx + n, :] = vv.astype(BF16)

    scale = A_HEAD_DIM ** -0.5

    def chunk(c, carry):
        r0 = pl.multiple_of(c * tq, tq)
        rows = pl.ds(r0, tq)
        heads = [slice(g * A_HEAD_DIM, (g + 1) * A_HEAD_DIM) for g in range(A_GROUP)]
        qs = []
        for cols in heads:
            qh = _rms(q_ref[rows, cols].astype(F32), qw_ref[...])
            if rope:
                qh = _rope_half_roll(qh, cos_ref[rows, :], sin_ref[rows, :])
            qs.append((qh * scale).astype(BF16))
        ss = [_dot_nt(qh, kall[...]) for qh in qs]
        ps = [jnp.exp(s - jnp.max(s, axis=-1, keepdims=True)) for s in ss]
        ls = [jnp.sum(p, axis=-1, keepdims=True) for p in ps]
        os = [_dot(p.astype(BF16), vall[...]) / l for p, l in zip(ps, ls)]
        for cols, o in zip(heads, os):
            o_ref[rows, cols] = o.astype(o_ref.dtype)
        return carry

    lax.fori_loop(0, n // tq, chunk, 0, unroll=True)


def _attention(proj, n_seq, n, q_norm_w, k_norm_w, e, rope_tabs, cache_k, cache_v, emit_kv,
               cast_jobs=()):
    rope = rope_tabs is not None
    n_ctx = cache_k.shape[2] if rope else 0
    hd = A_HEAD_DIM
    qcols = A_GROUP * hd
    in_specs = [
        pl.BlockSpec((n, qcols), lambda b, kv: (b, kv)),
        pl.BlockSpec((n, hd), lambda b, kv: (b, A_Q // hd + kv)),
        pl.BlockSpec((n, hd), lambda b, kv: (b, (A_Q + A_KV) // hd + kv)),
        pl.BlockSpec((None, 1, hd), lambda b, kv: (e, 0, 0)),
        pl.BlockSpec((None, 1, hd), lambda b, kv: (e, 0, 0)),
    ]
    args = [proj, proj, proj, q_norm_w, k_norm_w]
    if rope:
        cos2, sin2 = rope_tabs
        in_specs += [
            pl.BlockSpec((n, hd), lambda b, kv: (0, 0)),
            pl.BlockSpec((n, hd), lambda b, kv: (0, 0)),
            pl.BlockSpec((None, None, n_ctx, hd), lambda b, kv: (b, e, 0, kv)),
            pl.BlockSpec((None, None, n_ctx, hd), lambda b, kv: (b, e, 0, kv)),
        ]
        ck = cache_k.reshape(cache_k.shape[0], cache_k.shape[1], n_ctx, A_KV)
        cv = cache_v.reshape(cache_v.shape[0], cache_v.shape[1], n_ctx, A_KV)
        args += [cos2, sin2, ck, cv]
    out_specs = [pl.BlockSpec((n, qcols), lambda b, kv: (b, kv))]
    out_shape = [jax.ShapeDtypeStruct((n_seq * n, A_Q), BF16)]
    if emit_kv:
        out_specs += [pl.BlockSpec((n, hd), lambda b, kv: (b, kv))] * 2
        out_shape += [jax.ShapeDtypeStruct((n_seq * n, A_KV), F32)] * 2
    grid = (n_seq, A_KV_HEADS)
    job_in, job_out, job_shape = _cast_job_specs(cast_jobs, grid)
    in_specs += job_in
    args += [job.src for job in cast_jobs]
    out_specs += job_out
    out_shape += job_shape
    return pl.pallas_call(
        functools.partial(_attn_kernel, n=n, n_ctx=n_ctx, rope=rope, emit_kv=emit_kv,
                          tq=min(n, ATTN_Q_ROWS), n_jobs=len(cast_jobs)),
        grid=grid,
        in_specs=in_specs,
        out_specs=out_specs,
        out_shape=out_shape,
        scratch_shapes=[pltpu.VMEM((n_ctx + n, hd), BF16), pltpu.VMEM((n_ctx + n, hd), BF16)],
        compiler_params=_cparams("parallel", "arbitrary"),
        name="attention",
    )(*args)


def _gla_levels(c, rev):
    row = lax.broadcasted_iota(jnp.int32, (c, c), 0)
    col = lax.broadcasted_iota(jnp.int32, (c, c), 1)
    shift = HGRN_BASE.bit_length() - 1
    x = (row >> shift) ^ (col >> shift)
    lvl = jnp.zeros((c, c), jnp.int32)
    for l in range(1, (c // HGRN_BASE).bit_length()):
        lvl = jnp.where(x >= (1 << (l - 1)), l, lvl)
    causal = (col >= row) if rev else (col <= row)
    return jnp.where(causal, lvl, -1)


def _gla_blocks(chains):
    c, dk = chains[0]["q"].shape
    each = lambda fn: [fn(ch) for ch in chains]

    def split3(ch):
        hi = ch["lf"].astype(BF16)
        r1 = ch["lf"] - hi.astype(F32)
        mid = r1.astype(BF16)
        ch["parts"] = (hi, mid, (r1 - mid.astype(F32)).astype(BF16))

    def cumulate(ch):
        hi, mid, lo = ch["parts"]
        ch["b"] = _dot(ch["tri"], hi) + _dot(ch["tri"], mid) + _dot(ch["tri"], lo)
        ch["tot"] = ch["b"][0:1, :] if ch["rev"] else ch["b"][c - 1:c, :]

    def inter(ch):
        ch["o"] = _dot_nt((ch["q"] * jnp.exp(ch["b"])).astype(BF16), ch["st"].astype(BF16))
        ch["khat"] = (ch["k"] * jnp.exp(ch["tot"] - ch["b"])).astype(BF16)
        ch["vb"] = ch["v"].astype(BF16)

    def level0(ch):
        b3 = ch["b"].reshape(c // HGRN_BASE, HGRN_BASE, dk)
        mid_row = HGRN_BASE // 2 if ch["rev"] else HGRN_BASE // 2 - 1
        a = jnp.clip(b3 - b3[:, mid_row:mid_row + 1, :], -EXP_CLAMP, EXP_CLAMP).reshape(c, dk)
        p = _dot_nt((ch["q"] * jnp.exp(a)).astype(BF16), (ch["k"] * jnp.exp(-a)).astype(BF16))
        ch["scores"] = jnp.where(ch["lvl"] == 0, p, 0.0)

    def upper_level(ch, h, level):
        b3 = ch["b"].reshape(c // (2 * h), 2 * h, dk)
        ref_row = h if ch["rev"] else h - 1
        e = jnp.exp(-jnp.abs(b3 - b3[:, ref_row:ref_row + 1, :])).reshape(c, dk)
        p = _dot_nt((ch["q"] * e).astype(BF16), (ch["k"] * e).astype(BF16))
        ch["scores"] = jnp.where(ch["lvl"] == level, p, ch["scores"])

    def combine(ch):
        o = ch["o"] + _dot(ch["scores"].astype(BF16), ch["vb"])
        st_new = ch["st"] * jnp.exp(ch["tot"]) + _dot_tn(ch["vb"], ch["khat"])
        return o, st_new

    each(split3)
    each(cumulate)
    each(inter)
    each(level0)
    h, level = HGRN_BASE, 1
    while h < c:
        each(functools.partial(upper_level, h=h, level=level))
        h, level = 2 * h, level + 1
    return each(combine)


def _hgrn_kernel(*refs, n, has_state, emit_state, n_jobs):
    q_ref, i_ref, zf_ref, zb_ref, g_ref, lb_ref, nw_ref = refs[:7]
    pos = 7
    if has_state:
        s0f_ref, s0b_ref = refs[pos:pos + 2]
        pos += 2
    job_src = refs[pos:pos + n_jobs]
    pos += n_jobs
    o_ref = refs[pos]
    pos += 1
    if emit_state:
        sf_ref, sb_ref = refs[pos:pos + 2]
        pos += 2
    job_dst = refs[pos:pos + n_jobs]
    pos += n_jobs
    of_scr, ob_scr, st_scr = refs[pos:pos + 3]
    _run_cast_jobs(job_src, job_dst)

    c = MIX_BLOCK
    nb = n // c
    qscale = B_DK ** -0.5
    lvl_f = _gla_levels(c, False)
    lvl_b = _gla_levels(c, True)
    tri_f = jnp.where(lvl_f >= 0, 1.0, 0.0).astype(BF16)
    tri_b = jnp.where(lvl_b >= 0, 1.0, 0.0).astype(BF16)
    heads = [slice(hh * B_DK, (hh + 1) * B_DK) for hh in range(HGRN_HEADS_PER_STEP)]

    def chain(z_ref, rows, hh, rev):
        cols = heads[hh]
        lb = lb_ref[hh]
        f = lb + (1.0 - lb) * _sigmoid(z_ref[rows, cols].astype(F32))
        return dict(q=q_ref[rows, cols].astype(F32) * qscale, k=1.0 - f,
                    v=i_ref[rows, cols].astype(F32), lf=jnp.log(f),
                    st=st_scr[2 * hh + int(rev)], rev=rev,
                    lvl=lvl_b if rev else lvl_f, tri=tri_b if rev else tri_f)

    def step(j, carry):
        rows_f = pl.ds(pl.multiple_of(j * c, c), c)
        rows_b = pl.ds(pl.multiple_of((nb - 1 - j) * c, c), c)
        chains = []
        for hh in range(len(heads)):
            chains += [chain(zf_ref, rows_f, hh, False), chain(zb_ref, rows_b, hh, True)]
        outs = _gla_blocks(chains)
        for hh, cols in enumerate(heads):
            (o_f, st_f), (o_b, st_b) = outs[2 * hh], outs[2 * hh + 1]
            of_scr[rows_f, cols] = o_f
            ob_scr[rows_b, cols] = o_b
            st_scr[2 * hh] = st_f
            st_scr[2 * hh + 1] = st_b
        return carry

    def finish(j, carry):
        rows = pl.ds(pl.multiple_of(j * c, c), c)
        for cols in heads:
            y = _rms(of_scr[rows, cols] + ob_scr[rows, cols], nw_ref[...])
            g = g_ref[rows, cols].astype(F32)
            o_ref[rows, cols] = (y * (g * _sigmoid(g))).astype(o_ref.dtype)
        return carry

    for hh in range(len(heads)):
        if has_state:
            st_scr[2 * hh] = s0f_ref[hh].T
            st_scr[2 * hh + 1] = s0b_ref[hh].T
        else:
            st_scr[2 * hh] = jnp.zeros((B_DV, B_DK), F32)
            st_scr[2 * hh + 1] = jnp.zeros((B_DV, B_DK), F32)
    lax.fori_loop(0, nb, step, 0, unroll=2)
    lax.fori_loop(0, nb, finish, 0, unroll=2)
    if emit_state:
        for hh in range(len(heads)):
            sf_ref[hh] = st_scr[2 * hh].T
            sb_ref[hh] = st_scr[2 * hh + 1].T


def _hgrn(proj, n_seq, n, lb, o_norm_w, e, state_f, state_b, emit_state, cast_jobs=()):
    has_state = state_f is not None
    hp = HGRN_HEADS_PER_STEP
    d = B_DK * hp
    base = (A_Q + 2 * A_KV) // d
    col = lambda k: (lambda b, h: (b, base + k * (B_HEADS // hp) + h))
    in_specs = [pl.BlockSpec((n, d), col(k)) for k in range(5)]
    in_specs += [
        pl.BlockSpec((hp, 1, B_DK), lambda b, h: (h, 0, 0)),
        pl.BlockSpec((None, 1, B_DV), lambda b, h: (e, 0, 0)),
    ]
    args = [proj] * 5 + [lb, o_norm_w]
    if has_state:
        st_spec = pl.BlockSpec((None, None, hp, B_DK, B_DV), lambda b, h: (b, e, h, 0, 0))
        in_specs += [st_spec, st_spec]
        args += [state_f, state_b]
    out_specs = [pl.BlockSpec((n, d), lambda b, h: (b, h))]
    out_shape = [jax.ShapeDtypeStruct((n_seq * n, B_V), BF16)]
    if emit_state:
        so = pl.BlockSpec((None, hp, B_DK, B_DV), lambda b, h: (b, h, 0, 0))
        out_specs += [so, so]
        out_shape += [jax.ShapeDtypeStruct((n_seq, B_HEADS, B_DK, B_DV), F32)] * 2
    grid = (n_seq, B_HEADS // hp)
    job_in, job_out, job_shape = _cast_job_specs(cast_jobs, grid)
    in_specs += job_in
    args += [job.src for job in cast_jobs]
    out_specs += job_out
    out_shape += job_shape
    return pl.pallas_call(
        functools.partial(_hgrn_kernel, n=n, has_state=has_state, emit_state=emit_state,
                          n_jobs=len(cast_jobs)),
        grid=grid,
        in_specs=in_specs,
        out_specs=out_specs,
        out_shape=out_shape,
        scratch_shapes=[pltpu.VMEM((n, d), F32), pltpu.VMEM((n, d), F32),
                        pltpu.VMEM((2 * hp, B_DV, B_DK), F32)],
        compiler_params=_cparams("parallel", "arbitrary"),
        name="hgrn2",
    )(*args)


def _rope_split(x, cos, sin):
    half = x.shape[1] // 2
    x1, x2 = x[:, :half], x[:, half:]
    return jnp.concatenate([x1 * cos - x2 * sin, x1 * sin + x2 * cos], axis=1)


def _ret_kernel(*refs, n, n_seqs, rope, has_state, emit_state, n_jobs):
    q_ref, k_ref, v_ref, g_ref, lgf_ref, lgb_ref, nw_ref = refs[:7]
    pos = 7
    if rope:
        cos_ref, sin_ref = refs[pos:pos + 2]
        pos += 2
    if has_state:
        s0f_ref, s0b_ref = refs[pos:pos + 2]
        pos += 2
    job_src = refs[pos:pos + n_jobs]
    pos += n_jobs
    o_ref = refs[pos]
    pos += 1
    if emit_state:
        sf_ref, sb_ref = refs[pos:pos + 2]
        pos += 2
    job_dst = refs[pos:pos + n_jobs]
    pos += n_jobs
    _run_cast_jobs(job_src, job_dst)
    qs_scr, ks_scr, of_scr, ob_scr, stf_scr, stb_scr, dm_scr, qd_scr, kd_scr = refs[pos:pos + 9]

    c = min(RET_BLOCK, n)
    nb = n // c
    seqs = range(n_seqs)
    skip_inter = (not has_state) and nb == 1
    kscale = C_DK ** -0.5
    rowi = lax.broadcasted_iota(jnp.int32, (c, c), 0)
    coli = lax.broadcasted_iota(jnp.int32, (c, c), 1)
    rowq = lax.broadcasted_iota(jnp.int32, (c, C_DK), 0).astype(F32)

    def prep(j, carry):
        rows = pl.ds(pl.multiple_of(j * MIX_BLOCK, MIX_BLOCK), MIX_BLOCK)
        q = q_ref[rows, :].astype(F32)
        k = k_ref[rows, :].astype(F32)
        if rope:
            pos = pl.ds(pl.multiple_of((j % (n // MIX_BLOCK)) * MIX_BLOCK, MIX_BLOCK), MIX_BLOCK)
            q = _rope_split(q, cos_ref[pos, :], sin_ref[pos, :])
            k = _rope_split(k, cos_ref[pos, :], sin_ref[pos, :])
        qs_scr[rows, :] = q
        ks_scr[rows, :] = k * kscale
        return carry

    @pl.when(pl.program_id(1) == 0)
    def _():
        lgs = (lgf_ref[...], lgb_ref[...])
        dist = (rowi - coli).astype(F32)
        dm_scr[...] = (
            jnp.where(dist >= 0.0, jnp.exp(lgs[0][:, :c] * jnp.maximum(dist, 0.0)), 0.0)
            + jnp.where(dist <= 0.0, jnp.exp(lgs[1][:, :c] * jnp.maximum(-dist, 0.0)), 0.0))
        for d, rev in enumerate((False, True)):
            lgq = lgs[d][:, :C_DK]
            qd_scr[d] = jnp.exp(lgq * ((c - rowq) if rev else (rowq + 1.0)))
            kd_scr[d] = jnp.exp(lgq * (rowq if rev else (c - 1.0 - rowq)))

    def block_rows(s, j):
        return pl.ds(pl.multiple_of(s * n + j * c, c), c)

    def sweep_block(s, rows, d, lg_ref, st_scr, o_scr):
        vb = v_ref[rows, :]
        u = _dot_tn((ks_scr[rows, :] * kd_scr[d]).astype(BF16), vb)
        if skip_inter:
            st_scr[s] = u
        else:
            st = st_scr[s]
            o_scr[rows, :] = _dot((qs_scr[rows, :] * qd_scr[d]).astype(BF16), st.astype(BF16))
            st_scr[s] = st * jnp.exp(lg_ref[...] * float(c)) + u

    def sweep(j, carry):
        for s in seqs:
            sweep_block(s, block_rows(s, j), 0, lgf_ref, stf_scr, of_scr)
            sweep_block(s, block_rows(s, nb - 1 - j), 1, lgb_ref, stb_scr, ob_scr)
        return carry

    def finish(j, carry):
        pieces = []
        for s in seqs:
            keys = block_rows(s, j)
            kb, vb = ks_scr[keys, :].astype(BF16), v_ref[keys, :]
            for p in range(c // MIX_BLOCK):
                rows = pl.ds(pl.multiple_of(s * n + j * c + p * MIX_BLOCK, MIX_BLOCK), MIX_BLOCK)
                pieces.append((rows, slice(p * MIX_BLOCK, (p + 1) * MIX_BLOCK), kb, vb))
        ss = [_dot_nt(qs_scr[rows, :].astype(BF16), kb) * dm_scr[within, :]
              for rows, within, kb, _ in pieces]
        os = [_dot(sc.astype(BF16), vb) for sc, (_, _, _, vb) in zip(ss, pieces)]
        if not skip_inter:
            os = [o + of_scr[rows, :] + ob_scr[rows, :] for o, (rows, *_) in zip(os, pieces)]
        ys = [_rms(o, nw_ref[...]) for o in os]
        for y, (rows, *_) in zip(ys, pieces):
            g = g_ref[rows, :].astype(F32)
            o_ref[rows, :] = (y * (g * _sigmoid(g))).astype(o_ref.dtype)
        return carry

    lax.fori_loop(0, n_seqs * n // MIX_BLOCK, prep, 0)
    if has_state:
        stf_scr[...] = s0f_ref[...]
        stb_scr[...] = s0b_ref[...]
    elif not skip_inter:
        stf_scr[...] = jnp.zeros(stf_scr.shape, F32)
        stb_scr[...] = jnp.zeros(stb_scr.shape, F32)
    if emit_state or not skip_inter:
        lax.fori_loop(0, nb, sweep, 0, unroll=True)
    lax.fori_loop(0, nb, finish, 0, unroll=True)
    if emit_state:
        sf_ref[...] = stf_scr[...]
        sb_ref[...] = stb_scr[...]


def _retention(proj, n_seq, n, lg_f, lg_b, o_norm_w, o_idx, rope_tabs, state_f, state_b,
               emit_state, cast_jobs=()):
    rope = rope_tabs is not None
    has_state = state_f is not None
    nq = C_QK // C_DK
    c = min(RET_BLOCK, n)
    per_step = max(RET_MIN_SEQS, ROW_TILE // n)
    assert n_seq % per_step == 0
    rows = per_step * n
    in_specs = [
        pl.BlockSpec((rows, C_DK), lambda h, b: (b, h)),
        pl.BlockSpec((rows, C_DK), lambda h, b: (b, nq + h)),
        pl.BlockSpec((rows, C_DV), lambda h, b: (b, 2 * C_QK // C_DV + h)),
        pl.BlockSpec((rows, C_DV), lambda h, b: (b, (2 * C_QK + C_V) // C_DV + h)),
        pl.BlockSpec((None, 1, C_DV), lambda h, b: (h, 0, 0)),
        pl.BlockSpec((None, 1, C_DV), lambda h, b: (h, 0, 0)),
        pl.BlockSpec((None, 1, C_DV), lambda h, b: (o_idx, 0, 0)),
    ]
    args = [proj] * 4 + [lg_f, lg_b, o_norm_w]
    if rope:
        in_specs += [pl.BlockSpec((n, C_DK // 2), lambda h, b: (0, 0))] * 2
        args += list(rope_tabs)
    if has_state:
        st_spec = pl.BlockSpec((per_step, None, None, C_DK, C_DV),
                               lambda h, b: (b, o_idx, h, 0, 0))
        in_specs += [st_spec, st_spec]
        args += [state_f, state_b]
    out_specs = [pl.BlockSpec((rows, C_DV), lambda h, b: (b, h))]
    out_shape = [jax.ShapeDtypeStruct((n_seq * n, C_V), BF16)]
    if emit_state:
        so = pl.BlockSpec((per_step, None, C_DK, C_DV), lambda h, b: (b, h, 0, 0))
        out_specs += [so, so]
        out_shape += [jax.ShapeDtypeStruct((n_seq, C_HEADS, C_DK, C_DV), F32)] * 2
    grid = (C_HEADS, n_seq // per_step)
    job_in, job_out, job_shape = _cast_job_specs(cast_jobs, grid)
    in_specs += job_in
    args += [job.src for job in cast_jobs]
    out_specs += job_out
    out_shape += job_shape
    return pl.pallas_call(
        functools.partial(_ret_kernel, n=n, n_seqs=per_step, rope=rope, has_state=has_state,
                          emit_state=emit_state, n_jobs=len(cast_jobs)),
        grid=grid,
        in_specs=in_specs,
        out_specs=out_specs,
        out_shape=out_shape,
        scratch_shapes=[pltpu.VMEM((rows, C_DK), F32), pltpu.VMEM((rows, C_DK), F32),
                        pltpu.VMEM((rows, C_DV), F32), pltpu.VMEM((rows, C_DV), F32),
                        pltpu.VMEM((per_step, C_DK, C_DV), F32),
                        pltpu.VMEM((per_step, C_DK, C_DV), F32),
                        pltpu.VMEM((c, c), F32), pltpu.VMEM((2, c, C_DK), F32),
                        pltpu.VMEM((2, c, C_DK), F32)],
        compiler_params=_cparams("arbitrary", "arbitrary"),
        name="retention",
    )(*args)


def _rope_tables(n_tokens, head_dim):
    rows = n_tokens // GRID_W
    row = jnp.repeat(jnp.arange(rows, dtype=F32), GRID_W)
    col = jnp.tile(jnp.arange(GRID_W, dtype=F32), rows)
    quarter = head_dim // 4
    inv_freq = jnp.power(ROPE_BASE, -jnp.arange(quarter, dtype=F32) / quarter)
    ang = jnp.concatenate([row[:, None] * inv_freq, col[:, None] * inv_freq], axis=-1)
    return jnp.cos(ang), jnp.sin(ang)


def kernel(x_prompt, x_sample, cache_attn_k, cache_attn_v, state_hgrn_fwd, state_hgrn_bwd,
           state_ret_fwd, state_ret_bwd, c, c_ctx, w_mod, b_mod, norm_mix_w, norm_ffn_w,
           w_in_even, w_out_even, attn_q_norm_w, attn_k_norm_w, hgrn_lb, hgrn_o_norm_w,
           w_in_odd, w_out_odd, ret_decay_fwd, ret_decay_bwd, ret_o_norm_w,
           w_up, conv_w, conv_b, w_down):
    depth, d_model = norm_mix_w.shape
    bp, np_, _ = x_prompt.shape
    bs, ns, _ = x_sample.shape

    lb_all = jnp.cumsum(jax.nn.softmax(hgrn_lb.astype(F32), axis=0), axis=0)
    lg_f = jnp.broadcast_to(jax.nn.log_sigmoid(ret_decay_fwd.astype(F32))[:, :, None, None],
                            ret_decay_fwd.shape + (1, C_DV))
    lg_b = jnp.broadcast_to(jax.nn.log_sigmoid(ret_decay_bwd.astype(F32))[:, :, None, None],
                            ret_decay_bwd.shape + (1, C_DV))
    cos_a, sin_a = _rope_tables(ns, A_HEAD_DIM)
    rope_a = (jnp.concatenate([cos_a, cos_a], axis=1), jnp.concatenate([-sin_a, sin_a], axis=1))
    rope_c = _rope_tables(ns, C_DK)
    nmw = norm_mix_w.reshape(depth, 1, d_model)
    nfw = norm_ffn_w.reshape(depth, 1, d_model)
    qnw = attn_q_norm_w.reshape(-1, 1, A_HEAD_DIM)
    knw = attn_k_norm_w.reshape(-1, 1, A_HEAD_DIM)
    hnw = hgrn_o_norm_w.reshape(-1, 1, B_DV)
    rnw = ret_o_norm_w.reshape(-1, 1, C_DV)
    w_in = {0: w_in_even[0:1].astype(BF16)}
    w_out, w_gate_val, w_down_b = {}, {}, {}
    n_cond = 1 + bs
    pad = (-n_cond) % 8
    cond = jnp.concatenate([c_ctx[None, :], c, jnp.zeros((pad, d_model), F32)], axis=0)
    mod_all = _modulation(cond, w_mod, b_mod).reshape(depth, n_cond + pad, 6, d_model)

    groups = (
        dict(x=x_prompt.reshape(bp * np_, d_model), n_seq=bp, n=np_, row0=0,
             rows_per_cond=bp * np_, latent=False),
        dict(x=x_sample.reshape(bs * ns, d_model), n_seq=bs, n=ns, row0=1,
             rows_per_cond=ns, latent=True),
    )
    def w_in_f32(l):
        return (w_in_even, l // 2) if l % 2 == 0 else (w_in_odd, l // 2)

    results = []
    for grp in groups:
        y, n_seq, n = grp["x"], grp["n_seq"], grp["n"]
        row0, rpc, latent = grp["row0"], grp["rows_per_cond"], grp["latent"]
        new = dict(k=[], v=[], hf=[], hb=[], rf=[], rb=[])
        for l in range(depth):
            mod = mod_all[l]
            first = l not in w_out
            next_in = l + 1 < depth and l + 1 not in w_in
            jobs = [_CastJob(*w_in_f32(l + 1))] if next_in else []
            if l % 2 == 1 and first:
                jobs.append(_CastJob(w_up, l))
            proj, *copies = _norm_proj(y, mod, row0, rpc, nmw, l, w_in[l], 0, row=0, cast_jobs=jobs)
            if next_in:
                w_in[l + 1] = copies.pop(0)[None]
            if copies:
                w_gate_val[l] = copies.pop(0)[None]
            if l % 2 == 0:
                e = l // 2
                att = _attention(proj, n_seq, n, qnw, knw, e, rope_a if latent else None,
                                 cache_attn_k, cache_attn_v, emit_kv=not latent,
                                 cast_jobs=[_CastJob(w_out_even, e)] if first else ())
                lb = lb_all[e].reshape(B_HEADS, 1, B_DK)
                hg = _hgrn(proj, n_seq, n, lb, hnw, e,
                           state_hgrn_fwd if latent else None,
                           state_hgrn_bwd if latent else None, emit_state=not latent,
                           cast_jobs=[_CastJob(w_up, l), _CastJob(w_down, l)] if first else ())
                if first:
                    w_out[l], w_gate_val[l], w_down_b[l] = att[-1][None], hg[-2][None], hg[-1][None]
                if not latent:
                    new["k"].append(att[1].reshape(n_seq, n, A_KV_HEADS, A_HEAD_DIM))
                    new["v"].append(att[2].reshape(n_seq, n, A_KV_HEADS, A_HEAD_DIM))
                    new["hf"].append(hg[1])
                    new["hb"].append(hg[2])
                mixed, out_jobs = [att[0], hg[0]], []
            else:
                o = l // 2
                rt = _retention(proj, n_seq, n, lg_f[o], lg_b[o], rnw, o,
                                rope_c if latent else None,
                                state_ret_fwd if latent else None,
                                state_ret_bwd if latent else None, emit_state=not latent,
                                cast_jobs=[_CastJob(w_out_odd, o)] if first else ())
                if first:
                    w_out[l] = rt[-1][None]
                if not latent:
                    new["rf"].append(rt[1])
                    new["rb"].append(rt[2])
                mixed, out_jobs = [rt[0]], ([_CastJob(w_down, l)] if first else [])
            y, *copies = _proj_res(mixed, w_out[l], 0, y, mod, row0, rpc, row=2, cast_jobs=out_jobs)
            if copies:
                w_down_b[l] = copies[0][None]
            act = _ffn_up(y, mod, row0, rpc, nfw, l, w_gate_val[l], 0, conv_w, conv_b, n)
            y, = _proj_res([act], w_down_b[l], 0, y, mod, row0, rpc, row=5)
        results.append((y.reshape(n_seq, n, d_model), new))

    (y_p, new), (y_s, _) = results
    stack = lambda xs: jnp.stack(xs, axis=1)
    return (y_p, y_s, stack(new["k"]), stack(new["v"]), stack(new["hf"]), stack(new["hb"]),
            stack(new["rf"]), stack(new["rb"]))
```
